```python
import math
import jax, jax.numpy as jnp
from jax import lax
import numpy as np

D_MODEL = 1024
BATCH = 4
SEQ = 4096
DEPTH = 2
DEC_BATCH = 128
DEC_SEQ = 4
PAST_LEN = 2048
PAGE_SIZE = 128

N_A = DEPTH // 2
N_B = DEPTH - N_A
H_A = 4
DK_A = D_MODEL // 2 // H_A
DV_A = D_MODEL // H_A
KA = H_A * DK_A
VA = H_A * DV_A
GATE_RANK = 16
GATE_NORMALIZER = 16.0
GLA_CHUNK = 64
IN_A = 2 * KA + 2 * VA + GATE_RANK
H_B = 8
DQK_B = D_MODEL // (2 * H_B)
DV_B = 2 * DQK_B
QB = H_B * 2 * DQK_B
ZB = H_B * DV_B
Q_BLOCK = 128
EPS = 1e-6

kernel_name = "yoco_gla_diffattn_step"


def rms_norm(x, g):
    x32 = x.astype(jnp.float32)
    y = x32 * lax.rsqrt(jnp.mean(x32 * x32, axis=-1, keepdims=True) + EPS)
    return (y * g.astype(jnp.float32)).astype(x.dtype)


def ada_mod(x, c, g, w_ada, b_ada, n):
    mod = jax.nn.silu(c) @ w_ada + b_ada
    parts = [p[:, None, :] for p in jnp.split(mod, n, axis=-1)]
    h = rms_norm(x, g) * (1.0 + parts[1]) + parts[0]
    return h, parts[2:]


def gla_recurrence(q, k, v, logg, s0):
    b_, t_ = q.shape[:2]
    c_ = math.gcd(t_, GLA_CHUNK)
    n_ = t_ // c_

    def to_chunks(a):
        return a.reshape(b_, n_, c_, *a.shape[2:]).swapaxes(0, 1)

    causal = jnp.tril(jnp.ones((c_, c_), bool))[None, :, :, None, None]

    def step(s, inp):
        qc, kc, vc, gc = inp
        cum = jnp.cumsum(gc, axis=1)
        o_inter = jnp.einsum('bthk,bhkv->bthv', qc * jnp.exp(cum), s)
        rel = jnp.where(causal, cum[:, :, None] - cum[:, None, :], -jnp.inf)
        attn = jnp.einsum('bthk,bshk,btshk->bhts', qc, kc, jnp.exp(rel))
        o_intra = jnp.einsum('bhts,bshv->bthv', attn, vc)
        last = cum[:, -1]
        s_new = jnp.exp(last)[..., None] * s + jnp.einsum(
            'bshk,bshv->bhkv', kc * jnp.exp(last[:, None] - cum), vc)
        return s_new, o_inter + o_intra

    s_fin, o = lax.scan(step, s0, (to_chunks(q), to_chunks(k), to_chunks(v), to_chunks(logg)))
    return o.swapaxes(0, 1).reshape(b_, t_, H_A, DV_A), s_fin


def gla_block(x, c, s0, norm_g, ada_w, ada_b, w_in, w_g2, b_g, onorm_g, w_out):
    f32 = jnp.float32
    h, (gate,) = ada_mod(x, c, norm_g, ada_w, ada_b, 3)
    b_, t_, _ = h.shape
    q, k, v, z, glow = jnp.split(h @ w_in, [KA, 2 * KA, 2 * KA + VA, 2 * KA + 2 * VA], axis=-1)
    q = q.reshape(b_, t_, H_A, DK_A).astype(f32) * (DK_A ** -0.5)
    k = k.reshape(b_, t_, H_A, DK_A).astype(f32)
    v = v.reshape(b_, t_, H_A, DV_A).astype(f32)
    logg = (jax.nn.log_sigmoid((glow @ w_g2 + b_g).astype(f32)) / GATE_NORMALIZER).reshape(b_, t_, H_A, DK_A)
    o, s_fin = gla_recurrence(q, k, v, logg, s0.astype(f32))
    o = rms_norm(o, onorm_g).reshape(b_, t_, VA).astype(x.dtype)
    out = (o * jax.nn.silu(z)) @ w_out
    return x + gate * out, s_fin.astype(s0.dtype)


def shared_kv(x, c, g_kv, ada_w_kv, ada_b_kv, w_k, w_v, g_k):
    h, _ = ada_mod(x, c, g_kv, ada_w_kv, ada_b_kv, 2)
    b_, t_, _ = h.shape
    k = rms_norm((h @ w_k).reshape(b_, t_, H_B, 2, DQK_B), g_k)
    v = (h @ w_v).reshape(b_, t_, H_B, DV_B)
    return k, v


def diff_scores(q, k, q_pos, k_pos):
    s = jnp.einsum('bqhcd,bkhcd->bhcqk', q, k, preferred_element_type=jnp.float32) * (DQK_B ** -0.5)
    slopes = 2.0 ** (-8.0 * jnp.arange(1, H_B + 1, dtype=jnp.float32) / H_B)
    dist = (q_pos[:, None] - k_pos[None, :]).astype(jnp.float32)
    s = s - (slopes[:, None, None] * dist)[None, :, None]
    return jnp.where(k_pos[None, :] <= q_pos[:, None], s, -jnp.inf)


def diff_weights(s, lam):
    p = jax.nn.softmax(s, axis=-1)
    return p[:, :, 0] - lam * p[:, :, 1]


def diff_attn_prompt(q, k, v, lam):
    b_, t_ = q.shape[:2]
    nb = t_ // Q_BLOCK
    k_pos = jnp.arange(t_)
    v32 = v.astype(jnp.float32)
    qb = q.reshape(b_, nb, Q_BLOCK, H_B, 2, DQK_B).swapaxes(0, 1)

    def block(args):
        qi, i = args
        q_pos = i * Q_BLOCK + jnp.arange(Q_BLOCK)
        w = diff_weights(diff_scores(qi, k, q_pos, k_pos), lam)
        return jnp.einsum('bhqk,bkhe->bqhe', w, v32)

    o = lax.map(block, (qb, jnp.arange(nb)))
    return o.swapaxes(0, 1).reshape(b_, t_, H_B, DV_B)


def diff_attn_sample(q, k_new, v_new, k_past, v_past, lam):
    t_ = q.shape[1]
    past = k_past.shape[1]
    q_pos = past + jnp.arange(t_)
    s = jnp.concatenate([diff_scores(q, k_past, q_pos, jnp.arange(past)),
                         diff_scores(q, k_new, q_pos, q_pos)], axis=-1)
    w = diff_weights(s, lam)
    return (jnp.einsum('bhqk,bkhe->bqhe', w[..., :past], v_past.astype(jnp.float32))
            + jnp.einsum('bhqk,bkhe->bqhe', w[..., past:], v_new.astype(jnp.float32)))


def diff_block(x, c, attend, lam, lam_init, norm_g, ada_w, ada_b, w_in, g_q, g_sub, w_out):
    h, (gate,) = ada_mod(x, c, norm_g, ada_w, ada_b, 3)
    b_, t_, _ = h.shape
    q, z = jnp.split(h @ w_in, [QB], axis=-1)
    q = rms_norm(q.reshape(b_, t_, H_B, 2, DQK_B), g_q)
    o = attend(q, lam)
    o = rms_norm(o, g_sub) * (1.0 - lam_init)
    out = (o.reshape(b_, t_, ZB).astype(x.dtype) * jax.nn.silu(z)) @ w_out
    return x + gate * out


def setup_inputs(seed: int = 0) -> dict:
    key = jax.random.key(seed)
    keys = iter(jax.random.split(key, 48))
    f32 = jnp.float32

    def nrm(shape, scale):
        return jax.random.normal(next(keys), shape, f32) * scale

    def gain(shape):
        return 1.0 + nrm(shape, 0.02)

    D = D_MODEL
    n_pages = PAST_LEN // PAGE_SIZE
    n_pool = (DEC_BATCH * n_pages * 5 + 3) // 4
    perm = jax.random.permutation(next(keys), n_pool)[: DEC_BATCH * n_pages]
    page_table = perm.reshape(DEC_BATCH, n_pages).astype(jnp.int32)
    return {
        "x_prompt": nrm((BATCH, SEQ, D), 1.0),
        "x_sample": nrm((DEC_BATCH, DEC_SEQ, D), 1.0),
        "c_prompt": nrm((BATCH, D), 1.0),
        "c_sample": nrm((DEC_BATCH, D), 1.0),
        "state_gla": nrm((N_A, DEC_BATCH, H_A, DK_A, DV_A), 1.0),
        "cache_k": nrm((n_pool, PAGE_SIZE, H_B, 2, DQK_B), 1.0),
        "cache_v": nrm((n_pool, PAGE_SIZE, H_B, DV_B), 1.0),
        "page_table": page_table,
        "norm_a": gain((N_A, D)),
        "ada_w_a": nrm((N_A, D, 3 * D), 0.5 * D ** -0.5),
        "ada_b_a": nrm((N_A, 3 * D), 0.01),
        "w_in_a": nrm((N_A, D, IN_A), D ** -0.5),
        "w_g2_a": nrm((N_A, GATE_RANK, KA), GATE_RANK ** -0.5),
        "b_g_a": nrm((N_A, KA), 0.1),
        "onorm_a": gain((N_A, DV_A)),
        "w_out_a": nrm((N_A, VA, D), VA ** -0.5),
        "norm_kv": gain((D,)),
        "ada_w_kv": nrm((D, 2 * D), 0.5 * D ** -0.5),
        "ada_b_kv": nrm((2 * D,), 0.01),
        "w_k": nrm((D, QB), D ** -0.5),
        "w_v": nrm((D, ZB), D ** -0.5),
        "g_k": gain((DQK_B,)),
        "norm_b": gain((N_B, D)),
        "ada_w_b": nrm((N_B, D, 3 * D), 0.5 * D ** -0.5),
        "ada_b_b": nrm((N_B, 3 * D), 0.01),
        "w_in_b": nrm((N_B, D, QB + ZB), D ** -0.5),
        "g_q": gain((N_B, DQK_B)),
        "lambda_q1": nrm((N_B, DQK_B), 0.1),
        "lambda_k1": nrm((N_B, DQK_B), 0.1),
        "lambda_q2": nrm((N_B, DQK_B), 0.1),
        "lambda_k2": nrm((N_B, DQK_B), 0.1),
        "subln_b": gain((N_B, DV_B)),
        "w_out_b": nrm((N_B, ZB, D), ZB ** -0.5),
    }


def reference(x_prompt, x_sample, c_prompt, c_sample, state_gla, cache_k, cache_v, page_table,
              norm_a, ada_w_a, ada_b_a, w_in_a, w_g2_a, b_g_a, onorm_a, w_out_a,
              norm_kv, ada_w_kv, ada_b_kv, w_k, w_v, g_k,
              norm_b, ada_w_b, ada_b_b, w_in_b, g_q, lambda_q1, lambda_k1, lambda_q2, lambda_k2,
              subln_b, w_out_b):
    f32 = jnp.float32
    yp, ys = x_prompt, x_sample
    s0_prompt = jnp.zeros((x_prompt.shape[0], H_A, DK_A, DV_A), state_gla.dtype)
    gla_p, gla_s = [], []
    for l in range(N_A):
        prm = (norm_a[l], ada_w_a[l], ada_b_a[l], w_in_a[l], w_g2_a[l], b_g_a[l], onorm_a[l], w_out_a[l])
        yp, sp = gla_block(yp, c_prompt, s0_prompt, *prm)
        ys, ss = gla_block(ys, c_sample, state_gla[l], *prm)
        gla_p.append(sp)
        gla_s.append(ss)
    kp, vp = shared_kv(yp, c_prompt, norm_kv, ada_w_kv, ada_b_kv, w_k, w_v, g_k)
    ks, vs = shared_kv(ys, c_sample, norm_kv, ada_w_kv, ada_b_kv, w_k, w_v, g_k)
    nb_, n_pages = page_table.shape
    past = n_pages * PAGE_SIZE
    k_past = cache_k[page_table].reshape(nb_, past, H_B, 2, DQK_B)
    v_past = cache_v[page_table].reshape(nb_, past, H_B, DV_B)
    for j in range(N_B):
        lam_init = 0.8 - 0.6 * math.exp(-0.3 * (N_A + j))
        lam = (jnp.exp(jnp.sum(lambda_q1[j].astype(f32) * lambda_k1[j].astype(f32)))
               - jnp.exp(jnp.sum(lambda_q2[j].astype(f32) * lambda_k2[j].astype(f32))) + lam_init)
        prm = (norm_b[j], ada_w_b[j], ada_b_b[j], w_in_b[j], g_q[j], subln_b[j], w_out_b[j])
        yp = diff_block(yp, c_prompt, lambda q, lm: diff_attn_prompt(q, kp, vp, lm), lam, lam_init, *prm)
        ys = diff_block(ys, c_sample, lambda q, lm: diff_attn_sample(q, ks, vs, k_past, v_past, lm),
                        lam, lam_init, *prm)
    return (yp, ys, jnp.stack(gla_p), jnp.stack(gla_s), kp, vp, ks, vs)
```

```python
import functools
import math

import jax
import jax.numpy as jnp
from jax import lax
from jax.experimental import pallas as pl
from jax.experimental.pallas import tpu as pltpu

F32 = jnp.float32
BF16 = jnp.bfloat16
EPS = 1e-6
GATE_NORMALIZER = 16.0
LANES = 128
SUBLANES = 8
VMEM_LIMIT = 56 * 1024 * 1024

NT_DIMS = (((1,), (1,)), ((), ()))
TN_DIMS = (((0,), (0,)), ((), ()))


def _params(*sem):
    return pltpu.CompilerParams(dimension_semantics=sem, vmem_limit_bytes=VMEM_LIMIT)


def _silu(x):
    return x / (1.0 + jnp.exp(-x))


def _rms_rows(x):
    return x * lax.rsqrt(jnp.mean(x * x, axis=-1, keepdims=True) + EPS)


def _const_spec(shape):
    zeros = (0,) * len(shape)
    return pl.BlockSpec(shape, lambda *_: zeros, pipeline_mode=pl.Buffered(1))


def _ada_kernel(c_ref, w_ref, b_ref, o_ref):
    s = _silu(c_ref[...]).astype(BF16)
    o_ref[...] = jnp.dot(s, w_ref[...].astype(BF16), preferred_element_type=F32) + b_ref[...]


def _ada(c, w, b, tn=512):
    n, d = c.shape
    m = w.shape[1]
    return pl.pallas_call(
        _ada_kernel,
        grid=(m // tn,),
        in_specs=[pl.BlockSpec((n, d), lambda j: (0, 0)),
                  pl.BlockSpec((d, tn), lambda j: (0, j)),
                  pl.BlockSpec((1, tn), lambda j: (0, j))],
        out_specs=pl.BlockSpec((n, tn), lambda j: (0, j)),
        out_shape=jax.ShapeDtypeStruct((n, m), F32),
        compiler_params=_params("arbitrary"),
        name="ada_mod",
    )(c, w, b.reshape(1, m))


def _mod_spec(mod3, tm, tiles_per_group):
    r = mod3.shape[1]
    w = mod3.shape[2]
    if r == 1:
        return pl.BlockSpec((None, 1, w), lambda i: (i // tiles_per_group, 0, 0))
    return pl.BlockSpec((None, tm, w), lambda i: (i, 0, 0))


def _row_spec(tm, w):
    return pl.BlockSpec((tm, w), lambda i: (i, 0))


def _gla_in_kernel(x_ref, mod_ref, g_ref, w_ref, wg2_ref, bg_ref,
                   q_ref, k_ref, v_ref, z_ref, lg_ref, *, ka, va, dk):
    x = x_ref[...]
    d = x.shape[-1]
    mod = mod_ref[...]
    h = _rms_rows(x) * g_ref[...] * (1.0 + mod[:, d:2 * d]) + mod[:, 0:d]
    p = jnp.dot(h.astype(BF16), w_ref[...], preferred_element_type=F32)
    q_ref[...] = p[:, 0:ka] * (dk ** -0.5)
    k_ref[...] = p[:, ka:2 * ka]
    v_ref[...] = p[:, 2 * ka:2 * ka + va]
    z_ref[...] = p[:, 2 * ka + va:2 * ka + 2 * va]
    glow = p[:, 2 * ka + 2 * va:]
    g2 = jnp.dot(glow.astype(BF16), wg2_ref[...], preferred_element_type=F32) + bg_ref[...]
    logsig = jnp.minimum(g2, 0.0) - jnp.log1p(jnp.exp(-jnp.abs(g2)))
    lg_ref[...] = logsig * (1.0 / GATE_NORMALIZER)


def _gla_in(x2, mod3, tiles_per_group, tm, g, w_pad, wg2_pad, bg, ka, va, dk):
    n, d = x2.shape
    wcols = w_pad.shape[1]
    kern = functools.partial(_gla_in_kernel, ka=ka, va=va, dk=dk)
    outs = [jax.ShapeDtypeStruct((n, ka), F32), jax.ShapeDtypeStruct((n, ka), F32),
            jax.ShapeDtypeStruct((n, va), F32), jax.ShapeDtypeStruct((n, va), F32),
            jax.ShapeDtypeStruct((n, ka), F32)]
    return pl.pallas_call(
        kern,
        grid=(n // tm,),
        in_specs=[_row_spec(tm, d), _mod_spec(mod3, tm, tiles_per_group),
                  _const_spec((1, d)), _const_spec((d, wcols)),
                  _const_spec(wg2_pad.shape), _const_spec((1, ka))],
        out_specs=[_row_spec(tm, ka), _row_spec(tm, ka), _row_spec(tm, va),
                   _row_spec(tm, va), _row_spec(tm, ka)],
        out_shape=outs,
        compiler_params=_params("arbitrary"),
        name="gla_in",
    )(x2, mod3, g.reshape(1, d), w_pad, wg2_pad, bg.reshape(1, ka))


def _intra_masks(row, col, c):
    same8 = (row >> 3) == (col >> 3)
    diag = jnp.where(same8, row - col, -1)
    levels = []
    n = 2 * SUBLANES
    while n <= c:
        half = n // 2
        sh = half.bit_length() - 1
        rb = row >> sh
        cb = col >> sh
        levels.append((n, jnp.where((cb & 1) == 0, rb - cb, 0) == 1))
        n *= 2
    return diag, levels


def _intra_scores(q, k, cum, diag, levels):
    c, dk = q.shape
    a = jnp.zeros((c, c), F32)
    for d in range(SUBLANES):
        if d == 0:
            p = q * k
        else:
            p = q * pltpu.roll(k, d, 0) * jnp.exp(cum - pltpu.roll(cum, d, 0))
        a = jnp.where(diag == d, jnp.sum(p, axis=-1, keepdims=True), a)
    for n, mask in levels:
        half = n // 2
        ref = jnp.concatenate(
            [jnp.broadcast_to(cum[i * n + half - 1:i * n + half, :], (n, dk)) for i in range(c // n)],
            axis=0)
        qs = (q * jnp.exp(cum - ref)).astype(BF16)
        ks = (k * jnp.exp(ref - cum)).astype(BF16)
        a = jnp.where(mask, lax.dot_general(qs, ks, NT_DIMS, preferred_element_type=F32), a)
    return a


def _gla_prompt_kernel(q_ref, k_ref, v_ref, lg_ref, o_ref, s_ref, st_scr, *, c, nh, dk, dv):
    ci = pl.program_id(1)

    @pl.when(ci == 0)
    def _():
        st_scr[...] = jnp.zeros_like(st_scr)

    row = lax.broadcasted_iota(jnp.int32, (c, c), 0)
    col = lax.broadcasted_iota(jnp.int32, (c, c), 1)
    tril = jnp.where(col <= row, 1.0, 0.0).astype(F32)
    cum_all = jnp.dot(tril, lg_ref[...], precision=lax.Precision.HIGHEST, preferred_element_type=F32)
    diag, levels = _intra_masks(row, col, c)
    for h in range(nh):
        sk = slice(h * dk, (h + 1) * dk)
        sv = slice(h * dv, (h + 1) * dv)
        qh = q_ref[:, sk]
        kh = k_ref[:, sk]
        vb = v_ref[:, sv].astype(BF16)
        cum = cum_all[:, sk]
        st = st_scr[h]
        last = cum[c - 1:c, :]
        qe = (qh * jnp.exp(cum)).astype(BF16)
        o = lax.dot_general(qe, st.astype(BF16), NT_DIMS, preferred_element_type=F32)
        a = _intra_scores(qh, kh, cum, diag, levels)
        o_ref[:, sv] = o + jnp.dot(a.astype(BF16), vb, preferred_element_type=F32)
        kd = (kh * jnp.exp(last - cum)).astype(BF16)
        st_scr[h] = jnp.exp(last) * st + lax.dot_general(vb, kd, TN_DIMS, preferred_element_type=F32)

    @pl.when(ci == pl.num_programs(1) - 1)
    def _():
        for h in range(nh):
            s_ref[h] = st_scr[h].T


def _gla_prompt(q, k, v, lg, b, t, nh, dk, dv, c=64):
    n = b * t
    nc = t // c
    kern = functools.partial(_gla_prompt_kernel, c=c, nh=nh, dk=dk, dv=dv)
    row = lambda w: pl.BlockSpec((c, w), lambda bi, ci: (bi * nc + ci, 0))
    return pl.pallas_call(
        kern,
        grid=(b, nc),
        in_specs=[row(nh * dk), row(nh * dk), row(nh * dv), row(nh * dk)],
        out_specs=[row(nh * dv),
                   pl.BlockSpec((None, nh, dk, dv), lambda bi, ci: (bi, 0, 0, 0))],
        out_shape=[jax.ShapeDtypeStruct((n, nh * dv), F32),
                   jax.ShapeDtypeStruct((b, nh, dk, dv), F32)],
        scratch_shapes=[pltpu.VMEM((nh, dv, dk), F32)],
        compiler_params=_params("arbitrary", "arbitrary"),
        name="gla_prompt",
    )(q, k, v, lg)


def _gla_sample_kernel(q_ref, k_ref, v_ref, lg_ref, s0_ref, o_ref, s_ref, *, g, nh, dk, dv):
    r = g * SUBLANES
    row = lax.broadcasted_iota(jnp.int32, (r, r), 0)
    col = lax.broadcasted_iota(jnp.int32, (r, r), 1)
    diag, _ = _intra_masks(row, col, SUBLANES)
    tril = jnp.where(diag >= 0, 1.0, 0.0).astype(F32)
    cum_all = jnp.dot(tril, lg_ref[...], precision=lax.Precision.HIGHEST, preferred_element_type=F32)
    for h in range(nh):
        sk = slice(h * dk, (h + 1) * dk)
        sv = slice(h * dv, (h + 1) * dv)
        qh = q_ref[:, sk]
        kh = k_ref[:, sk]
        vh = v_ref[:, sv]
        cum = cum_all[:, sk]
        a = _intra_scores(qh, kh, cum, diag, [])
        o_intra = jnp.dot(a.astype(BF16), vh.astype(BF16), preferred_element_type=F32)
        qe = qh * jnp.exp(cum)
        for i in range(g):
            rs = slice(i * SUBLANES, (i + 1) * SUBLANES)
            st = s0_ref[i, h].T
            o_ref[rs, sv] = o_intra[rs] + lax.dot_general(
                qe[rs].astype(BF16), st.astype(BF16), NT_DIMS, preferred_element_type=F32)
            last = cum[(i + 1) * SUBLANES - 1:(i + 1) * SUBLANES, :]
            kd = (kh[rs] * jnp.exp(last - cum[rs])).astype(BF16)
            new = jnp.exp(last) * st + lax.dot_general(
                vh[rs].astype(BF16), kd, TN_DIMS, preferred_element_type=F32)
            s_ref[i, h] = new.T


def _gla_sample(q, k, v, lg, s0, nh, dk, dv, g=8):
    nseq = s0.shape[0]
    r = g * SUBLANES
    kern = functools.partial(_gla_sample_kernel, g=g, nh=nh, dk=dk, dv=dv)
    st_spec = pl.BlockSpec((g, nh, dk, dv), lambda i: (i, 0, 0, 0))
    return pl.pallas_call(
        kern,
        grid=(nseq // g,),
        in_specs=[_row_spec(r, nh * dk), _row_spec(r, nh * dk), _row_spec(r, nh * dv),
                  _row_spec(r, nh * dk), st_spec],
        out_specs=[_row_spec(r, nh * dv), st_spec],
        out_shape=[jax.ShapeDtypeStruct((nseq * SUBLANES, nh * dv), F32),
                   jax.ShapeDtypeStruct(s0.shape, F32)],
        compiler_params=_params("arbitrary"),
        name="gla_sample",
    )(q, k, v, lg, s0)


def _rms_groups64(x, lane_lo):
    outs = []
    for j in range(x.shape[-1] // LANES):
        xj = x[:, j * LANES:(j + 1) * LANES]
        sq = xj * xj
        s0 = jnp.sum(jnp.where(lane_lo, sq, 0.0), axis=-1, keepdims=True)
        s1 = jnp.sum(jnp.where(lane_lo, 0.0, sq), axis=-1, keepdims=True)
        ms = jnp.where(lane_lo, s0, s1) * (2.0 / LANES)
        outs.append(xj * lax.rsqrt(ms + EPS))
    return jnp.concatenate(outs, axis=-1)


def _mid_kernel(o_ref, z_ref, x_ref, moda_ref, modkv_ref, modb_ref,
                onorm_ref, wout_ref, gkv_ref, wk_ref, wv_ref, gk_ref, gb_ref, winb_ref, gq_ref,
                y_ref, k_ref, v_ref, kb_ref, vb_ref, qb_ref, zb_ref, *, nh, dv, qk_scale):
    d = x_ref.shape[-1]
    o = o_ref[...]
    on = jnp.concatenate([_rms_rows(o[:, h * dv:(h + 1) * dv]) for h in range(nh)], axis=-1)
    on = on * onorm_ref[...]
    u = (on * _silu(z_ref[...])).astype(BF16)
    out = jnp.dot(u, wout_ref[...], preferred_element_type=F32)
    y = x_ref[...] + moda_ref[:, 2 * d:3 * d] * out
    y_ref[...] = y
    yn = _rms_rows(y)
    lane_lo = lax.broadcasted_iota(jnp.int32, (1, LANES), 1) < (LANES // 2)

    modkv = modkv_ref[...]
    h2 = (yn * gkv_ref[...] * (1.0 + modkv[:, d:2 * d]) + modkv[:, 0:d]).astype(BF16)
    kk = _rms_groups64(jnp.dot(h2, wk_ref[...], preferred_element_type=F32), lane_lo) * gk_ref[...]
    k_ref[...] = kk
    kb_ref[...] = kk.astype(BF16)
    vv = jnp.dot(h2, wv_ref[...], preferred_element_type=F32)
    v_ref[...] = vv
    vb_ref[...] = vv.astype(BF16)

    modb = modb_ref[...]
    h3 = (yn * gb_ref[...] * (1.0 + modb[:, d:2 * d]) + modb[:, 0:d]).astype(BF16)
    qz = jnp.dot(h3, winb_ref[...], preferred_element_type=F32)
    qb = qz.shape[-1] - zb_ref.shape[-1]
    qq = _rms_groups64(qz[:, 0:qb], lane_lo) * gq_ref[...]
    qb_ref[...] = (qq * qk_scale).astype(BF16)
    zb_ref[...] = qz[:, qb:]


def _mid(o, z, x2, moda3, modkv3, modb3, tiles_per_group, tm,
         onorm_t, wout, gkv, wk, wv, gk_t, gb, winb, gq_t, nh, dv, qk_scale):
    n, d = x2.shape
    qb = wk.shape[1]
    zb = winb.shape[1] - qb
    kern = functools.partial(_mid_kernel, nh=nh, dv=dv, qk_scale=qk_scale)
    ms = lambda m: _mod_spec(m, tm, tiles_per_group)
    outs = [jax.ShapeDtypeStruct((n, d), F32), jax.ShapeDtypeStruct((n, qb), F32),
            jax.ShapeDtypeStruct((n, zb), F32), jax.ShapeDtypeStruct((n, qb), BF16),
            jax.ShapeDtypeStruct((n, zb), BF16), jax.ShapeDtypeStruct((n, qb), BF16),
            jax.ShapeDtypeStruct((n, zb), F32)]
    return pl.pallas_call(
        kern,
        grid=(n // tm,),
        in_specs=[_row_spec(tm, o.shape[1]), _row_spec(tm, z.shape[1]), _row_spec(tm, d),
                  ms(moda3), ms(modkv3), ms(modb3),
                  _const_spec(onorm_t.shape), _const_spec(wout.shape), _const_spec(gkv.shape),
                  _const_spec(wk.shape), _const_spec(wv.shape), _const_spec(gk_t.shape),
                  _const_spec(gb.shape), _const_spec(winb.shape), _const_spec(gq_t.shape)],
        out_specs=[_row_spec(tm, d), _row_spec(tm, qb), _row_spec(tm, zb), _row_spec(tm, qb),
                   _row_spec(tm, zb), _row_spec(tm, qb), _row_spec(tm, zb)],
        out_shape=outs,
        compiler_params=_params("arbitrary"),
        name="mid_proj",
    )(o, z, x2, moda3, modkv3, modb3, onorm_t, wout, gkv, wk, wv, gk_t, gb, winb, gq_t)


def _lambda_full(l_ref, lam_init):
    lv = l_ref[...]
    s1 = jnp.sum(lv[0:1] * lv[1:2], axis=-1, keepdims=True)
    s2 = jnp.sum(lv[2:3] * lv[3:4], axis=-1, keepdims=True)
    return jnp.exp(s1) - jnp.exp(s2) + lam_init


def _split_components(q):
    lane_lo = lax.broadcasted_iota(jnp.int32, (1, LANES), 1) < (LANES // 2)
    q = q.astype(F32)
    return jnp.concatenate([jnp.where(lane_lo, q, 0.0), jnp.where(lane_lo, 0.0, q)], axis=0).astype(BF16)


def _attn_prompt_kernel(slope_ref, l_ref, q_ref, k_ref, v_ref, o_ref, *, tq, tk, lam_init):
    h = pl.program_id(1)
    qi = pl.program_id(2)
    slope = slope_ref[h]
    qq = _split_components(q_ref[...])
    ii = lax.broadcasted_iota(jnp.int32, (2 * tq, tk), 0)
    ii = jnp.where(ii >= tq, ii - tq, ii)
    jj = lax.broadcasted_iota(jnp.int32, (2 * tq, tk), 1)
    rel = ii - jj

    def body(j, carry):
        m, l, acc = carry
        start = pl.multiple_of(j * tk, tk)
        kj = k_ref[pl.ds(start, tk), :]
        vj = v_ref[pl.ds(start, tk), :]
        s = lax.dot_general(qq, kj, NT_DIMS, preferred_element_type=F32)
        dist = rel + (qi * tq - j * tk)
        s = jnp.where(dist >= 0, s - slope * dist.astype(F32), -jnp.inf)
        m_new = jnp.maximum(m, jnp.max(s, axis=-1, keepdims=True))
        alpha = jnp.exp(m - m_new)
        p = jnp.exp(s - m_new)
        l = alpha * l + jnp.sum(p, axis=-1, keepdims=True)
        acc = alpha * acc + jnp.dot(p.astype(BF16), vj, preferred_element_type=F32)
        return m_new, l, acc

    init = (jnp.full((2 * tq, 1), -jnp.inf, F32), jnp.zeros((2 * tq, 1), F32),
            jnp.zeros((2 * tq, LANES), F32))
    n_kv = ((qi + 1) * tq + tk - 1) // tk
    _, l, acc = lax.fori_loop(0, n_kv, body, init)
    r = acc / l
    o_ref[...] = r[0:tq] - _lambda_full(l_ref, lam_init) * r[tq:2 * tq]


def _attn_prompt(qb, kb, vb, slopes, lvec, nh, lam_init, tq=256, tk=256):
    b, t, w = qb.shape
    kern = functools.partial(_attn_prompt_kernel, tq=tq, tk=tk, lam_init=lam_init)
    kv_spec = pl.BlockSpec((None, t, LANES), lambda bi, hi, qi: (bi, 0, hi))
    return pl.pallas_call(
        kern,
        grid=(b, nh, t // tq),
        in_specs=[pl.BlockSpec(memory_space=pltpu.SMEM), _const_spec(lvec.shape),
                  pl.BlockSpec((None, tq, LANES), lambda bi, hi, qi: (bi, qi, hi)),
                  kv_spec, kv_spec],
        out_specs=pl.BlockSpec((None, tq, LANES), lambda bi, hi, qi: (bi, qi, hi)),
        out_shape=jax.ShapeDtypeStruct((b, t, w), F32),
        compiler_params=_params("arbitrary", "arbitrary", "arbitrary"),
        name="attn_prompt",
    )(slopes, lvec, qb, kb, vb)


def _attn_sample_kernel(pt_ref, l_ref, slope_ref, qpos_ref, q_ref, kc_ref, vc_ref, kn_ref, vn_ref,
                        o_ref, qbd_scr, m_scr, l_scr, acc_scr, *, nh, nt, past, page, lam_init):
    del pt_ref
    p = pl.program_id(1)
    n_pages = past // page
    groups = 2 * nh
    w = q_ref.shape[-1]

    @pl.when(p == 0)
    def _():
        q = q_ref[...].astype(F32)
        lane_grp = lax.broadcasted_iota(jnp.int32, (groups, w), 1) // (LANES // 2)
        own = lane_grp == lax.broadcasted_iota(jnp.int32, (groups, w), 0)
        for t in range(nt):
            qt = jnp.broadcast_to(q[t:t + 1, :], (groups, w))
            qbd_scr[t * groups:(t + 1) * groups, :] = jnp.where(own, qt, 0.0).astype(BF16)
        m_scr[...] = jnp.full_like(m_scr, -jnp.inf)
        l_scr[...] = jnp.zeros_like(l_scr)
        acc_scr[...] = jnp.zeros_like(acc_scr)

    def update(kblk, vblk, kpos, valid):
        s = lax.dot_general(qbd_scr[...], kblk, NT_DIMS, preferred_element_type=F32)
        dist = qpos_ref[...] - kpos
        s = s - slope_ref[...] * dist.astype(F32)
        if valid is not None:
            s = jnp.where(valid & (dist >= 0), s, -jnp.inf)
        m = m_scr[...]
        m_new = jnp.maximum(m, jnp.max(s, axis=-1, keepdims=True))
        alpha = jnp.exp(m - m_new)
        pr = jnp.exp(s - m_new)
        l_scr[...] = alpha * l_scr[...] + jnp.sum(pr, axis=-1, keepdims=True)
        acc_scr[...] = alpha * acc_scr[...] + jnp.dot(pr.astype(BF16), vblk, preferred_element_type=F32)
        m_scr[...] = m_new

    @pl.when(p < n_pages)
    def _():
        kpos = p * page + lax.broadcasted_iota(jnp.int32, (1, page), 1)
        update(kc_ref[...].astype(BF16), vc_ref[...].astype(BF16), kpos, None)

    @pl.when(p == n_pages)
    def _():
        rows = kn_ref.shape[0]
        idx = lax.broadcasted_iota(jnp.int32, (1, rows), 1)
        update(kn_ref[...], vn_ref[...], past + idx, idx < nt)
        r = acc_scr[...] / l_scr[...]
        lam = _lambda_full(l_ref, lam_init)
        grp = lax.broadcasted_iota(jnp.int32, (groups, w), 0)
        head = lax.broadcasted_iota(jnp.int32, (groups, w), 1) // LANES
        wgt = jnp.where(grp == 2 * head, 1.0, 0.0) - jnp.where(grp == 2 * head + 1, 1.0, 0.0) * lam
        outs = [jnp.sum(r[t * groups:(t + 1) * groups, :] * wgt, axis=0, keepdims=True) for t in range(nt)]
        outs.append(jnp.zeros((o_ref.shape[0] - nt, w), F32))
        o_ref[...] = jnp.concatenate(outs, axis=0)


def _attn_sample(q8, cache_k, cache_v, page_table, kn, vn, lvec, slope_col, qpos_col, nh, nt, lam_init):
    nseq, _, w = q8.shape
    n_pool, page, _ = cache_k.shape
    n_pages = page_table.shape[1]
    past = n_pages * page
    rows = kn.shape[1]
    groups = 2 * nh
    kern = functools.partial(_attn_sample_kernel, nh=nh, nt=nt, past=past, page=page, lam_init=lam_init)
    cache_spec = pl.BlockSpec(
        (None, page, w), lambda b, p, pt: (pt[b, jnp.minimum(p, n_pages - 1)], 0, 0))
    seq_spec = lambda r: pl.BlockSpec((None, r, w), lambda b, p, pt: (b, 0, 0))
    const = lambda shape: pl.BlockSpec(shape, lambda b, p, pt: (0,) * len(shape))
    grid_spec = pltpu.PrefetchScalarGridSpec(
        num_scalar_prefetch=1,
        grid=(nseq, n_pages + 1),
        in_specs=[const(lvec.shape), const(slope_col.shape), const(qpos_col.shape),
                  seq_spec(SUBLANES), cache_spec, cache_spec, seq_spec(rows), seq_spec(rows)],
        out_specs=seq_spec(SUBLANES),
        scratch_shapes=[pltpu.VMEM((nt * groups, w), BF16), pltpu.VMEM((nt * groups, 1), F32),
                        pltpu.VMEM((nt * groups, 1), F32), pltpu.VMEM((nt * groups, w), F32)],
    )
    return pl.pallas_call(
        kern,
        grid_spec=grid_spec,
        out_shape=jax.ShapeDtypeStruct((nseq, SUBLANES, w), F32),
        compiler_params=_params("arbitrary", "arbitrary"),
        name="attn_sample",
    )(page_table, lvec, slope_col, qpos_col, q8, cache_k, cache_v, kn, vn)


def _final_kernel(o_ref, z_ref, y_ref, modb_ref, gsub_ref, wout_ref, out_ref, *, dvb, post_scale):
    d = y_ref.shape[-1]
    o = o_ref[...]
    on = jnp.concatenate([_rms_rows(o[:, h * dvb:(h + 1) * dvb]) for h in range(o.shape[-1] // dvb)], axis=-1)
    on = on * gsub_ref[...] * post_scale
    u = (on * _silu(z_ref[...])).astype(BF16)
    out = jnp.dot(u, wout_ref[...], preferred_element_type=F32)
    out_ref[...] = y_ref[...] + modb_ref[:, 2 * d:3 * d] * out


def _final(o, z, y, modb3, tiles_per_group, tm, gsub_t, wout, dvb, post_scale):
    n, d = y.shape
    kern = functools.partial(_final_kernel, dvb=dvb, post_scale=post_scale)
    return pl.pallas_call(
        kern,
        grid=(n // tm,),
        in_specs=[_row_spec(tm, o.shape[1]), _row_spec(tm, z.shape[1]), _row_spec(tm, d),
                  _mod_spec(modb3, tm, tiles_per_group), _const_spec(gsub_t.shape), _const_spec(wout.shape)],
        out_specs=_row_spec(tm, d),
        out_shape=jax.ShapeDtypeStruct((n, d), F32),
        compiler_params=_params("arbitrary"),
        name="final_proj",
    )(o, z, y, modb3, gsub_t, wout)


def _tile(vec, reps):
    return jnp.tile(vec.astype(F32), reps).reshape(1, -1)


def kernel(x_prompt, x_sample, c_prompt, c_sample, state_gla, cache_k, cache_v, page_table, norm_a, ada_w_a, ada_b_a, w_in_a, w_g2_a, b_g_a, onorm_a, w_out_a, norm_kv, ada_w_kv, ada_b_kv, w_k, w_v, g_k, norm_b, ada_w_b, ada_b_b, w_in_b, g_q, lambda_q1, lambda_k1, lambda_q2, lambda_k2, subln_b, w_out_b):
    b, t, d = x_prompt.shape
    nseq, nt, _ = x_sample.shape
    n_a, _, nh_a, dk_a, dv_a = state_gla.shape
    n_b = norm_b.shape[0]
    ka, va = nh_a * dk_a, nh_a * dv_a
    nh_b, dv_b = cache_v.shape[2], cache_v.shape[3]
    dqk_b = cache_k.shape[4]
    n_pool, page = cache_k.shape[0], cache_k.shape[1]
    tm = 256
    assert n_a == 1 and n_b == 1, "one GLA layer and one differential-attention layer"
    assert nt <= SUBLANES and t % tm == 0 and (nseq * nt) % tm == 0

    c_all = jnp.concatenate([c_prompt, c_sample], axis=0)
    xp = x_prompt.reshape(b * t, d)
    xs = x_sample.reshape(nseq * nt, d)
    tiles_p = t // tm

    def split_mod(mod):
        mp = mod[:b].reshape(b, 1, -1)
        ms = jnp.repeat(mod[b:], nt, axis=0).reshape((nseq * nt) // tm, tm, -1)
        return mp, ms

    pad8 = lambda a: jnp.pad(a.reshape(nseq, nt, -1), ((0, 0), (0, SUBLANES - nt), (0, 0)))

    l = 0
    moda_p, moda_s = split_mod(_ada(c_all, ada_w_a[l], ada_b_a[l]))
    rank = w_g2_a.shape[1]
    w_in_pad = jnp.pad(w_in_a[l], ((0, 0), (0, LANES - rank))).astype(BF16)
    wg2_pad = jnp.pad(w_g2_a[l], ((0, LANES - rank), (0, 0))).astype(BF16)
    gla_args = (norm_a[l], w_in_pad, wg2_pad, b_g_a[l], ka, va, dk_a)
    qp, kp_, vp_, zp, lgp = _gla_in(xp, moda_p, tiles_p, tm, *gla_args)
    qs, ks_, vs_, zs, lgs = _gla_in(xs, moda_s, 1, tm, *gla_args)
    op, state_p = _gla_prompt(qp, kp_, vp_, lgp, b, t, nh_a, dk_a, dv_a)
    os8, state_s = _gla_sample(pad8(qs).reshape(-1, ka), pad8(ks_).reshape(-1, ka),
                               pad8(vs_).reshape(-1, va), pad8(lgs).reshape(-1, ka),
                               state_gla[l], nh_a, dk_a, dv_a)
    os_ = os8.reshape(nseq, SUBLANES, va)[:, :nt].reshape(nseq * nt, va)

    j = 0
    lam_init = 0.8 - 0.6 * math.exp(-0.3 * (n_a + j))
    modkv_p, modkv_s = split_mod(_ada(c_all, ada_w_kv, ada_b_kv))
    modb_p, modb_s = split_mod(_ada(c_all, ada_w_b[j], ada_b_b[j]))
    mid_w = (_tile(onorm_a[l], nh_a), w_out_a[l].astype(BF16), norm_kv.reshape(1, d),
             w_k.astype(BF16), w_v.astype(BF16), _tile(g_k, 2 * nh_b), norm_b[j].reshape(1, d),
             w_in_b[j].astype(BF16), _tile(g_q[j], 2 * nh_b), nh_a, dv_a, dqk_b ** -0.5)
    y1p, kp, vp, kbp, vbp, qbp, zbp = _mid(op, zp, xp, moda_p, modkv_p, modb_p, tiles_p, tm, *mid_w)
    y1s, ks, vs, kbs, vbs, qbs, zbs = _mid(os_, zs, xs, moda_s, modkv_s, modb_s, 1, tm, *mid_w)

    w = nh_b * dv_b
    lvec = jnp.stack([lambda_q1[j], lambda_k1[j], lambda_q2[j], lambda_k2[j]]).astype(F32)
    slopes = 2.0 ** (-8.0 * jnp.arange(1, nh_b + 1, dtype=F32) / nh_b)
    ap = _attn_prompt(qbp.reshape(b, t, w), kbp.reshape(b, t, w), vbp.reshape(b, t, w),
                      slopes, lvec, nh_b, lam_init)
    past = page_table.shape[1] * page
    groups = 2 * nh_b
    slope_col = jnp.tile(jnp.repeat(slopes, 2), nt).reshape(nt * groups, 1)
    qpos_col = (past + jnp.repeat(jnp.arange(nt, dtype=jnp.int32), groups)).reshape(nt * groups, 1)
    pad_rows = lambda a, r: jnp.pad(a.reshape(nseq, nt, w), ((0, 0), (0, r - nt), (0, 0)))
    as8 = _attn_sample(pad_rows(qbs, SUBLANES), cache_k.reshape(n_pool, page, w),
                       cache_v.reshape(n_pool, page, w), page_table,
                       pad_rows(kbs, 2 * SUBLANES), pad_rows(vbs, 2 * SUBLANES),
                       lvec, slope_col, qpos_col, nh_b, nt, lam_init)
    as_ = as8[:, :nt].reshape(nseq * nt, w)

    fin_w = (_tile(subln_b[j], nh_b), w_out_b[j].astype(BF16), dv_b, 1.0 - lam_init)
    yp = _final(ap.reshape(b * t, w), zbp, y1p, modb_p, tiles_p, tm, *fin_w)
    ys = _final(as_, zbs, y1s, modb_s, 1, tm, *fin_w)

    return (yp.reshape(b, t, d), ys.reshape(nseq, nt, d), state_p[None], state_s[None],
            kp.reshape(b, t, nh_b, 2, dqk_b), vp.reshape(b, t, nh_b, dv_b),
            ks.reshape(nseq, nt, nh_b, 2, dqk_b), vs.reshape(nseq, nt, nh_b, dv_b))
```

```python
import functools
import math

import jax
import jax.numpy as jnp
from jax import lax
from jax.experimental import pallas as pl
from jax.experimental.pallas import tpu as pltpu

F32 = jnp.float32
BF16 = jnp.bfloat16
EPS = 1e-6
GATE_NORMALIZER = 16.0
LOG2E = math.log2(math.e)
LANES = 128
SUBLANES = 8
VMEM_LIMIT = 56 * 1024 * 1024

NT_DIMS = (((1,), (1,)), ((), ()))
TN_DIMS = (((0,), (0,)), ((), ()))


def _params(*sem):
    return pltpu.CompilerParams(dimension_semantics=sem, vmem_limit_bytes=VMEM_LIMIT)


def _silu(x):
    return x / (1.0 + jnp.exp(-x))


def _rms_rows(x):
    return x * lax.rsqrt(jnp.mean(x * x, axis=-1, keepdims=True) + EPS)


def _const_spec(shape):
    zeros = (0,) * len(shape)
    return pl.BlockSpec(shape, lambda *_: zeros, pipeline_mode=pl.Buffered(1))


def _ada_kernel(c_ref, w_ref, b_ref, o_ref):
    s = _silu(c_ref[...]).astype(BF16)
    o_ref[...] = jnp.dot(s, w_ref[...].astype(BF16), preferred_element_type=F32) + b_ref[...]


def _ada(c, w, b, tn=512):
    n, d = c.shape
    m = w.shape[1]
    return pl.pallas_call(
        _ada_kernel,
        grid=(m // tn,),
        in_specs=[pl.BlockSpec((n, d), lambda j: (0, 0)),
                  pl.BlockSpec((d, tn), lambda j: (0, j)),
                  pl.BlockSpec((1, tn), lambda j: (0, j))],
        out_specs=pl.BlockSpec((n, tn), lambda j: (0, j)),
        out_shape=jax.ShapeDtypeStruct((n, m), F32),
        compiler_params=_params("arbitrary"),
        name="ada_mod",
    )(c, w, b.reshape(1, m))


def _mod_spec(mod3, tm, tiles_per_group):
    r = mod3.shape[1]
    w = mod3.shape[2]
    if r == 1:
        return pl.BlockSpec((None, 1, w), lambda i: (i // tiles_per_group, 0, 0))
    return pl.BlockSpec((None, tm, w), lambda i: (i, 0, 0))


def _row_spec(tm, w):
    return pl.BlockSpec((tm, w), lambda i: (i, 0))


def _gla_in_kernel(x_ref, mod_ref, g_ref, w_ref, wg2_ref, bg_ref,
                   q_ref, k_ref, v_ref, z_ref, lg_ref, *, ka, va, dk):
    x = x_ref[...]
    d = x.shape[-1]
    mod = mod_ref[...]
    h = _rms_rows(x) * g_ref[...] * (1.0 + mod[:, d:2 * d]) + mod[:, 0:d]
    p = jnp.dot(h.astype(BF16), w_ref[...], preferred_element_type=F32)
    q_ref[...] = p[:, 0:ka] * (dk ** -0.5)
    k_ref[...] = p[:, ka:2 * ka]
    v_ref[...] = p[:, 2 * ka:2 * ka + va]
    z_ref[...] = p[:, 2 * ka + va:2 * ka + 2 * va]
    glow = p[:, 2 * ka + 2 * va:]
    g2 = jnp.dot(glow.astype(BF16), wg2_ref[...], preferred_element_type=F32) + bg_ref[...]
    logsig = jnp.minimum(g2, 0.0) - jnp.log1p(jnp.exp(-jnp.abs(g2)))
    lg_ref[...] = logsig * (1.0 / GATE_NORMALIZER)


def _gla_in(x2, mod3, tiles_per_group, tm, g, w_pad, wg2_pad, bg, ka, va, dk):
    n, d = x2.shape
    wcols = w_pad.shape[1]
    kern = functools.partial(_gla_in_kernel, ka=ka, va=va, dk=dk)
    outs = [jax.ShapeDtypeStruct((n, ka), F32), jax.ShapeDtypeStruct((n, ka), F32),
            jax.ShapeDtypeStruct((n, va), F32), jax.ShapeDtypeStruct((n, va), F32),
            jax.ShapeDtypeStruct((n, ka), F32)]
    return pl.pallas_call(
        kern,
        grid=(n // tm,),
        in_specs=[_row_spec(tm, d), _mod_spec(mod3, tm, tiles_per_group),
                  _const_spec((1, d)), _const_spec((d, wcols)),
                  _const_spec(wg2_pad.shape), _const_spec((1, ka))],
        out_specs=[_row_spec(tm, ka), _row_spec(tm, ka), _row_spec(tm, va),
                   _row_spec(tm, va), _row_spec(tm, ka)],
        out_shape=outs,
        compiler_params=_params("arbitrary"),
        name="gla_in",
    )(x2, mod3, g.reshape(1, d), w_pad, wg2_pad, bg.reshape(1, ka))


def _intra_masks(row, col, c):
    same8 = (row >> 3) == (col >> 3)
    diag = jnp.where(same8, row - col, -1)
    levels = []
    n = 2 * SUBLANES
    while n <= c:
        half = n // 2
        sh = half.bit_length() - 1
        rb = row >> sh
        cb = col >> sh
        levels.append((n, jnp.where((cb & 1) == 0, rb - cb, 0) == 1))
        n *= 2
    return diag, levels


def _intra_scores(q, k, cum, diag, levels):
    c, dk = q.shape
    a = jnp.zeros((c, c), F32)
    for d in range(SUBLANES):
        if d == 0:
            p = q * k
        else:
            p = q * pltpu.roll(k, d, 0) * jnp.exp(cum - pltpu.roll(cum, d, 0))
        a = jnp.where(diag == d, jnp.sum(p, axis=-1, keepdims=True), a)
    for n, mask in levels:
        half = n // 2
        ref = jnp.concatenate(
            [jnp.broadcast_to(cum[i * n + half - 1:i * n + half, :], (n, dk)) for i in range(c // n)],
            axis=0)
        qs = (q * jnp.exp(cum - ref)).astype(BF16)
        ks = (k * jnp.exp(ref - cum)).astype(BF16)
        a = jnp.where(mask, lax.dot_general(qs, ks, NT_DIMS, preferred_element_type=F32), a)
    return a


def _gla_prompt_kernel(q_ref, k_ref, v_ref, lg_ref, o_ref, s_ref, st_scr, *, c, nh, dk, dv):
    ci = pl.program_id(1)

    @pl.when(ci == 0)
    def _():
        st_scr[...] = jnp.zeros_like(st_scr)

    row = lax.broadcasted_iota(jnp.int32, (c, c), 0)
    col = lax.broadcasted_iota(jnp.int32, (c, c), 1)
    tril = jnp.where(col <= row, 1.0, 0.0).astype(F32)
    cum_all = jnp.dot(tril, lg_ref[...], precision=lax.Precision.HIGHEST, preferred_element_type=F32)
    diag, levels = _intra_masks(row, col, c)
    for h in range(nh):
        sk = slice(h * dk, (h + 1) * dk)
        sv = slice(h * dv, (h + 1) * dv)
        qh = q_ref[:, sk]
        kh = k_ref[:, sk]
        vb = v_ref[:, sv].astype(BF16)
        cum = cum_all[:, sk]
        st = st_scr[h]
        last = cum[c - 1:c, :]
        qe = (qh * jnp.exp(cum)).astype(BF16)
        o = lax.dot_general(qe, st.astype(BF16), NT_DIMS, preferred_element_type=F32)
        a = _intra_scores(qh, kh, cum, diag, levels)
        o_ref[:, sv] = o + jnp.dot(a.astype(BF16), vb, preferred_element_type=F32)
        kd = (kh * jnp.exp(last - cum)).astype(BF16)
        st_scr[h] = jnp.exp(last) * st + lax.dot_general(vb, kd, TN_DIMS, preferred_element_type=F32)

    @pl.when(ci == pl.num_programs(1) - 1)
    def _():
        for h in range(nh):
            s_ref[h] = st_scr[h].T


def _gla_prompt(q, k, v, lg, b, t, nh, dk, dv, c=64):
    n = b * t
    nc = t // c
    kern = functools.partial(_gla_prompt_kernel, c=c, nh=nh, dk=dk, dv=dv)
    row = lambda w: pl.BlockSpec((c, w), lambda bi, ci: (bi * nc + ci, 0))
    return pl.pallas_call(
        kern,
        grid=(b, nc),
        in_specs=[row(nh * dk), row(nh * dk), row(nh * dv), row(nh * dk)],
        out_specs=[row(nh * dv),
                   pl.BlockSpec((None, nh, dk, dv), lambda bi, ci: (bi, 0, 0, 0))],
        out_shape=[jax.ShapeDtypeStruct((n, nh * dv), F32),
                   jax.ShapeDtypeStruct((b, nh, dk, dv), F32)],
        scratch_shapes=[pltpu.VMEM((nh, dv, dk), F32)],
        compiler_params=_params("arbitrary", "arbitrary"),
        name="gla_prompt",
    )(q, k, v, lg)


def _gla_sample_kernel(q_ref, k_ref, v_ref, lg_ref, s0_ref, o_ref, s_ref, *, g, nh, dk, dv):
    r = g * SUBLANES
    row = lax.broadcasted_iota(jnp.int32, (r, r), 0)
    col = lax.broadcasted_iota(jnp.int32, (r, r), 1)
    diag, _ = _intra_masks(row, col, SUBLANES)
    tril = jnp.where(diag >= 0, 1.0, 0.0).astype(F32)
    cum_all = jnp.dot(tril, lg_ref[...], precision=lax.Precision.HIGHEST, preferred_element_type=F32)
    for h in range(nh):
        sk = slice(h * dk, (h + 1) * dk)
        sv = slice(h * dv, (h + 1) * dv)
        qh = q_ref[:, sk]
        kh = k_ref[:, sk]
        vh = v_ref[:, sv]
        cum = cum_all[:, sk]
        a = _intra_scores(qh, kh, cum, diag, [])
        o_intra = jnp.dot(a.astype(BF16), vh.astype(BF16), preferred_element_type=F32)
        qe = qh * jnp.exp(cum)
        for i in range(g):
            rs = slice(i * SUBLANES, (i + 1) * SUBLANES)
            st = s0_ref[i, h].T
            o_ref[rs, sv] = o_intra[rs] + lax.dot_general(
                qe[rs].astype(BF16), st.astype(BF16), NT_DIMS, preferred_element_type=F32)
            last = cum[(i + 1) * SUBLANES - 1:(i + 1) * SUBLANES, :]
            kd = (kh[rs] * jnp.exp(last - cum[rs])).astype(BF16)
            new = jnp.exp(last) * st + lax.dot_general(
                vh[rs].astype(BF16), kd, TN_DIMS, preferred_element_type=F32)
            s_ref[i, h] = new.T


def _gla_sample(q, k, v, lg, s0, nh, dk, dv, g=8):
    nseq = s0.shape[0]
    r = g * SUBLANES
    kern = functools.partial(_gla_sample_kernel, g=g, nh=nh, dk=dk, dv=dv)
    st_spec = pl.BlockSpec((g, nh, dk, dv), lambda i: (i, 0, 0, 0))
    return pl.pallas_call(
        kern,
        grid=(nseq // g,),
        in_specs=[_row_spec(r, nh * dk), _row_spec(r, nh * dk), _row_spec(r, nh * dv),
                  _row_spec(r, nh * dk), st_spec],
        out_specs=[_row_spec(r, nh * dv), st_spec],
        out_shape=[jax.ShapeDtypeStruct((nseq * SUBLANES, nh * dv), F32),
                   jax.ShapeDtypeStruct(s0.shape, F32)],
        compiler_params=_params("arbitrary"),
        name="gla_sample",
    )(q, k, v, lg, s0)


def _rms_groups64(x, lane_lo):
    outs = []
    for j in range(x.shape[-1] // LANES):
        xj = x[:, j * LANES:(j + 1) * LANES]
        sq = xj * xj
        s0 = jnp.sum(jnp.where(lane_lo, sq, 0.0), axis=-1, keepdims=True)
        s1 = jnp.sum(jnp.where(lane_lo, 0.0, sq), axis=-1, keepdims=True)
        ms = jnp.where(lane_lo, s0, s1) * (2.0 / LANES)
        outs.append(xj * lax.rsqrt(ms + EPS))
    return jnp.concatenate(outs, axis=-1)


def _mid_kernel(o_ref, z_ref, x_ref, moda_ref, modkv_ref, modb_ref,
                onorm_ref, wout_ref, gkv_ref, wk_ref, wv_ref, gk_ref, gb_ref, winb_ref, gq_ref,
                y_ref, k_ref, v_ref, kb_ref, vb_ref, qb_ref, zb_ref, *, nh, dv, qk_scale, token_minor):
    d = x_ref.shape[-1]
    o = o_ref[...]
    on = jnp.concatenate([_rms_rows(o[:, h * dv:(h + 1) * dv]) for h in range(nh)], axis=-1)
    on = on * onorm_ref[...]
    u = (on * _silu(z_ref[...])).astype(BF16)
    out = jnp.dot(u, wout_ref[...], preferred_element_type=F32)
    y = x_ref[...] + moda_ref[:, 2 * d:3 * d] * out
    y_ref[...] = y
    yn = _rms_rows(y)
    lane_lo = lax.broadcasted_iota(jnp.int32, (1, LANES), 1) < (LANES // 2)

    modkv = modkv_ref[...]
    h2 = (yn * gkv_ref[...] * (1.0 + modkv[:, d:2 * d]) + modkv[:, 0:d]).astype(BF16)
    kk = _rms_groups64(jnp.dot(h2, wk_ref[...], preferred_element_type=F32), lane_lo) * gk_ref[...]
    k_ref[...] = kk.T if token_minor else kk
    kb_ref[...] = kk.astype(BF16)
    vv = jnp.dot(h2, wv_ref[...], preferred_element_type=F32)
    v_ref[...] = vv
    vb_ref[...] = (vv.T if token_minor else vv).astype(BF16)

    modb = modb_ref[...]
    h3 = (yn * gb_ref[...] * (1.0 + modb[:, d:2 * d]) + modb[:, 0:d]).astype(BF16)
    qz = jnp.dot(h3, winb_ref[...], preferred_element_type=F32)
    qb = qz.shape[-1] - zb_ref.shape[-1]
    qq = _rms_groups64(qz[:, 0:qb], lane_lo) * (gq_ref[...] * qk_scale)
    qb_ref[...] = (qq.T if token_minor else qq).astype(BF16)
    zb_ref[...] = qz[:, qb:]


def _mid(o, z, x2, moda3, modkv3, modb3, tiles_per_group, tm,
         onorm_t, wout, gkv, wk, wv, gk_t, gb, winb, gq_t, nh, dv, qk_scale, token_minor):
    n, d = x2.shape
    qb = wk.shape[1]
    zb = winb.shape[1] - qb
    kern = functools.partial(_mid_kernel, nh=nh, dv=dv, qk_scale=qk_scale, token_minor=token_minor)
    ms = lambda m: _mod_spec(m, tm, tiles_per_group)
    if token_minor:
        groups = n // (tm * tiles_per_group)
        t = tm * tiles_per_group
        feat = lambda w, dt: jax.ShapeDtypeStruct((groups, w, t), dt)
        feat_spec = lambda w: pl.BlockSpec(
            (None, w, tm), lambda i: (i // tiles_per_group, 0, i % tiles_per_group))
    else:
        feat = lambda w, dt: jax.ShapeDtypeStruct((n, w), dt)
        feat_spec = lambda w: _row_spec(tm, w)
    outs = [jax.ShapeDtypeStruct((n, d), F32), feat(qb, F32),
            jax.ShapeDtypeStruct((n, zb), F32), jax.ShapeDtypeStruct((n, qb), BF16),
            feat(zb, BF16), feat(qb, BF16),
            jax.ShapeDtypeStruct((n, zb), F32)]
    out_specs = [_row_spec(tm, d), feat_spec(qb), _row_spec(tm, zb), _row_spec(tm, qb),
                 feat_spec(zb), feat_spec(qb), _row_spec(tm, zb)]
    return pl.pallas_call(
        kern,
        grid=(n // tm,),
        in_specs=[_row_spec(tm, o.shape[1]), _row_spec(tm, z.shape[1]), _row_spec(tm, d),
                  ms(moda3), ms(modkv3), ms(modb3),
                  _const_spec(onorm_t.shape), _const_spec(wout.shape), _const_spec(gkv.shape),
                  _const_spec(wk.shape), _const_spec(wv.shape), _const_spec(gk_t.shape),
                  _const_spec(gb.shape), _const_spec(winb.shape), _const_spec(gq_t.shape)],
        out_specs=out_specs,
        out_shape=outs,
        compiler_params=_params("arbitrary"),
        name="mid_proj",
    )(o, z, x2, moda3, modkv3, modb3, onorm_t, wout, gkv, wk, wv, gk_t, gb, winb, gq_t)


def _lambda_full(l_ref, lam_init):
    lv = l_ref[...]
    s1 = jnp.sum(lv[0:1] * lv[1:2], axis=-1, keepdims=True)
    s2 = jnp.sum(lv[2:3] * lv[3:4], axis=-1, keepdims=True)
    return jnp.exp(s1) - jnp.exp(s2) + lam_init


def _attn_prompt_kernel(slope_ref, l_ref, qt_ref, k_ref, vt_ref, o_ref, *, tq, tk, hps, lam_init):
    hg = pl.program_id(1)
    qi = pl.program_id(2)
    half = LANES // 2
    zeros = jnp.zeros((half, tq), BF16)
    krow = lax.broadcasted_iota(jnp.int32, (tk, LANES), 0).astype(F32)
    slopes, qqs, kbiases = [], [], []
    for hh in range(hps):
        slope = slope_ref[hg * hps + hh]
        qt = qt_ref[hh * LANES:(hh + 1) * LANES, :]
        qqs.append(jnp.concatenate([jnp.concatenate([qt[0:half], zeros], axis=0),
                                    jnp.concatenate([zeros, qt[half:LANES]], axis=0)], axis=1))
        kbiases.append(jnp.concatenate([slope * krow] * (2 * tq // LANES), axis=1))
        slopes.append(slope)

    def tile(j, carries, causal):
        start = pl.multiple_of(j * tk, tk)
        heads = range(hps)
        cols = [slice(hh * LANES, (hh + 1) * LANES) for hh in heads]
        offset = (j * tk - qi * tq).astype(F32)
        scores = [jnp.dot(k_ref[pl.ds(start, tk), cols[hh]], qqs[hh], preferred_element_type=F32)
                  + kbiases[hh] for hh in heads]
        if causal:
            kk = lax.broadcasted_iota(jnp.int32, (tk, 2 * tq), 0)
            qpos = lax.broadcasted_iota(jnp.int32, (tk, 2 * tq), 1)
            keep = kk <= jnp.where(qpos >= tq, qpos - tq, qpos)
            scores = [jnp.where(keep, s, -jnp.inf) for s in scores]
        stats, probs = [], []
        for hh in heads:
            m, l, _ = carries[hh]
            shift = slopes[hh] * offset
            m_new = jnp.maximum(m, jnp.max(scores[hh], axis=0, keepdims=True) + shift)
            alpha = jnp.exp2(m - m_new)
            p = jnp.exp2(scores[hh] - (m_new - shift))
            stats.append((m_new, alpha * l + jnp.sum(p, axis=0, keepdims=True), alpha))
            probs.append(p.astype(BF16))
        pvs = [jnp.dot(vt_ref[cols[hh], pl.ds(start, tk)], probs[hh], preferred_element_type=F32)
               for hh in heads]
        return tuple((stats[hh][0], stats[hh][1], stats[hh][2] * carries[hh][2] + pvs[hh]) for hh in heads)

    init = tuple((jnp.full((1, 2 * tq), -jnp.inf, F32), jnp.zeros((1, 2 * tq), F32),
                  jnp.zeros((LANES, 2 * tq), F32)) for _ in range(hps))
    carries = lax.fori_loop(0, qi, lambda j, c: tile(j, c, False), init)
    carries = tile(qi, carries, True)
    lam = _lambda_full(l_ref, lam_init)
    for hh in range(hps):
        _, l, acc = carries[hh]
        r = acc * (1.0 / l)
        o_ref[:, hh * LANES:(hh + 1) * LANES] = (r[:, 0:tq] - lam * r[:, tq:2 * tq]).T


def _attn_prompt(qt, kb, vt, slopes, lvec, nh, lam_init, tq=256, tk=256, hps=4):
    b, t, w = kb.shape
    assert tq == tk, "the causal tile is peeled as the last key tile"
    assert nh % hps == 0
    gw = hps * LANES
    kern = functools.partial(_attn_prompt_kernel, tq=tq, tk=tk, hps=hps, lam_init=lam_init)
    return pl.pallas_call(
        kern,
        grid=(b, nh // hps, t // tq),
        in_specs=[pl.BlockSpec(memory_space=pltpu.SMEM), _const_spec(lvec.shape),
                  pl.BlockSpec((None, gw, tq), lambda bi, hi, qi: (bi, hi, qi)),
                  pl.BlockSpec((None, t, gw), lambda bi, hi, qi: (bi, 0, hi)),
                  pl.BlockSpec((None, gw, t), lambda bi, hi, qi: (bi, hi, 0))],
        out_specs=pl.BlockSpec((None, tq, gw), lambda bi, hi, qi: (bi, qi, hi)),
        out_shape=jax.ShapeDtypeStruct((b, t, w), F32),
        compiler_params=_params("arbitrary", "arbitrary", "arbitrary"),
        name="attn_prompt",
    )(slopes, lvec, qt, kb, vt)


def _attn_sample_kernel(pt_ref, l_ref, slope_ref, qpos_ref, q_ref, *refs,
                        nh, nt, pp, past, page, lam_init):
    del pt_ref
    k_refs, v_refs = refs[0:pp], refs[pp:2 * pp]
    kn_ref, vn_ref, o_ref, qbd_scr, spread_scr, own_scr, m_scr, l_scr, acc_scr = refs[2 * pp:]
    p = pl.program_id(1)
    n_steps = past // (page * pp)
    rows, w = qbd_scr.shape
    hr = 2 * nt
    dv = acc_scr.shape[-1]

    @pl.when(p == 0)
    def _():
        q = q_ref[...].astype(F32)
        qt = jnp.concatenate([q] * (rows // SUBLANES), axis=0)
        lane_grp = lax.broadcasted_iota(jnp.int32, (rows, w), 1) // (LANES // 2)
        row_grp = lax.broadcasted_iota(jnp.int32, (rows, w), 0) // nt
        qbd_scr[...] = jnp.where(lane_grp == row_grp, qt, 0.0).astype(BF16)
        tok = lax.broadcasted_iota(jnp.int32, (page, page * nh), 0)
        slot = lax.broadcasted_iota(jnp.int32, (page, page * nh), 1)
        spread_scr[...] = jnp.where(slot // nh == tok, 1.0, 0.0).astype(BF16)
        slot_head = lax.broadcasted_iota(jnp.int32, (rows, page * nh), 1) % nh
        row_head = lax.broadcasted_iota(jnp.int32, (rows, page * nh), 0) // hr
        own_scr[...] = jnp.where(slot_head == row_head, 1.0, 0.0)
        m_scr[...] = jnp.full_like(m_scr, -jnp.inf)
        l_scr[...] = jnp.zeros_like(l_scr)
        acc_scr[...] = jnp.zeros_like(acc_scr)

    def softmax_step(s):
        m = m_scr[...]
        m_new = jnp.maximum(m, jnp.max(s, axis=-1, keepdims=True))
        alpha = jnp.exp2(m - m_new)
        pr = jnp.exp2(s - m_new)
        l_scr[...] = alpha * l_scr[...] + jnp.sum(pr, axis=-1, keepdims=True)
        m_scr[...] = m_new
        return alpha, pr

    @pl.when(p < n_steps)
    def _():
        qbd = qbd_scr[...]
        s = jnp.concatenate(
            [jnp.dot(qbd, k_refs[i][...].astype(BF16), preferred_element_type=F32) for i in range(pp)],
            axis=1)
        kpos = p * (pp * page) + lax.broadcasted_iota(jnp.int32, (1, pp * page), 1)
        s = s - slope_ref[...] * (qpos_ref[...] - kpos).astype(F32)
        alpha, pr = softmax_step(s)
        prb = pr.astype(BF16)
        pv = jnp.zeros((rows, dv), F32)
        for i in range(pp):
            wide = jnp.dot(prb[:, i * page:(i + 1) * page], spread_scr[...], preferred_element_type=F32)
            wide = (wide * own_scr[...]).astype(BF16)
            pv = pv + jnp.dot(wide, v_refs[i][...].astype(BF16), preferred_element_type=F32)
        acc_scr[...] = alpha * acc_scr[...] + pv

    @pl.when(p == n_steps)
    def _():
        nrows = kn_ref.shape[0]
        idx = lax.broadcasted_iota(jnp.int32, (1, nrows), 1)
        dist = qpos_ref[...] - (past + idx)
        s = lax.dot_general(qbd_scr[...], kn_ref[...], NT_DIMS, preferred_element_type=F32)
        s = jnp.where((idx < nt) & (dist >= 0), s - slope_ref[...] * dist.astype(F32), -jnp.inf)
        alpha, pr = softmax_step(s)
        pv = jnp.dot(pr.astype(BF16), vn_ref[...], preferred_element_type=F32)
        own = jnp.concatenate([pv[h * hr:(h + 1) * hr, h * dv:(h + 1) * dv] for h in range(nh)], axis=0)
        r = (alpha * acc_scr[...] + own) / l_scr[...]
        diff = r - _lambda_full(l_ref, lam_init) * pltpu.roll(r, rows - nt, 0)
        for h in range(nh):
            o_ref[:, h * dv:(h + 1) * dv] = diff[h * hr:(h + 1) * hr]


def _attn_sample(q8, k_t, v_rows, page_table, kn, vn, lvec, slope_col, qpos_col, nh, nt, lam_init, pp=4):
    nseq, _, w = q8.shape
    n_pool, _, page = k_t.shape
    dv = v_rows.shape[-1]
    n_pages = page_table.shape[1]
    past = n_pages * page
    nrows = kn.shape[1]
    rows = 2 * nh * nt
    assert 2 * nt == SUBLANES and n_pages % pp == 0
    kern = functools.partial(_attn_sample_kernel, nh=nh, nt=nt, pp=pp, past=past, page=page, lam_init=lam_init)

    def page_of(i):
        return lambda b, p, pt: pt[b, jnp.minimum(p * pp + i, n_pages - 1)]

    k_specs = [pl.BlockSpec((None, w, page), lambda b, p, pt, f=page_of(i): (f(b, p, pt), 0, 0))
               for i in range(pp)]
    v_specs = [pl.BlockSpec((None, page * nh, dv), lambda b, p, pt, f=page_of(i): (f(b, p, pt), 0, 0))
               for i in range(pp)]
    seq_spec = lambda r: pl.BlockSpec((None, r, w), lambda b, p, pt: (b, 0, 0))
    const = lambda shape: pl.BlockSpec(shape, lambda b, p, pt: (0,) * len(shape))
    grid_spec = pltpu.PrefetchScalarGridSpec(
        num_scalar_prefetch=1,
        grid=(nseq, n_pages // pp + 1),
        in_specs=[const(lvec.shape), const(slope_col.shape), const(qpos_col.shape), seq_spec(SUBLANES)]
        + k_specs + v_specs + [seq_spec(nrows), seq_spec(nrows)],
        out_specs=seq_spec(SUBLANES),
        scratch_shapes=[pltpu.VMEM((rows, w), BF16), pltpu.VMEM((page, page * nh), BF16),
                        pltpu.VMEM((rows, page * nh), F32), pltpu.VMEM((rows, 1), F32),
                        pltpu.VMEM((rows, 1), F32), pltpu.VMEM((rows, dv), F32)],
    )
    return pl.pallas_call(
        kern,
        grid_spec=grid_spec,
        out_shape=jax.ShapeDtypeStruct((nseq, SUBLANES, w), F32),
        compiler_params=_params("arbitrary", "arbitrary"),
        name="attn_sample",
    )(page_table, lvec, slope_col, qpos_col, q8, *([k_t] * pp), *([v_rows] * pp), kn, vn)


def _final_kernel(o_ref, z_ref, y_ref, modb_ref, gsub_ref, wout_ref, out_ref, *, dvb, post_scale):
    d = y_ref.shape[-1]
    o = o_ref[...]
    on = jnp.concatenate([_rms_rows(o[:, h * dvb:(h + 1) * dvb]) for h in range(o.shape[-1] // dvb)], axis=-1)
    on = on * gsub_ref[...] * post_scale
    u = (on * _silu(z_ref[...])).astype(BF16)
    out = jnp.dot(u, wout_ref[...], preferred_element_type=F32)
    out_ref[...] = y_ref[...] + modb_ref[:, 2 * d:3 * d] * out


def _final(o, z, y, modb3, tiles_per_group, tm, gsub_t, wout, dvb, post_scale):
    n, d = y.shape
    kern = functools.partial(_final_kernel, dvb=dvb, post_scale=post_scale)
    return pl.pallas_call(
        kern,
        grid=(n // tm,),
        in_specs=[_row_spec(tm, o.shape[1]), _row_spec(tm, z.shape[1]), _row_spec(tm, d),
                  _mod_spec(modb3, tm, tiles_per_group), _const_spec(gsub_t.shape), _const_spec(wout.shape)],
        out_specs=_row_spec(tm, d),
        out_shape=jax.ShapeDtypeStruct((n, d), F32),
        compiler_params=_params("arbitrary"),
        name="final_proj",
    )(o, z, y, modb3, gsub_t, wout)


def _tile(vec, reps):
    return jnp.tile(vec.astype(F32), reps).reshape(1, -1)


def kernel(x_prompt, x_sample, c_prompt, c_sample, state_gla, cache_k, cache_v, page_table, norm_a, ada_w_a, ada_b_a, w_in_a, w_g2_a, b_g_a, onorm_a, w_out_a, norm_kv, ada_w_kv, ada_b_kv, w_k, w_v, g_k, norm_b, ada_w_b, ada_b_b, w_in_b, g_q, lambda_q1, lambda_k1, lambda_q2, lambda_k2, subln_b, w_out_b):
    b, t, d = x_prompt.shape
    nseq, nt, _ = x_sample.shape
    n_a, _, nh_a, dk_a, dv_a = state_gla.shape
    n_b = norm_b.shape[0]
    ka, va = nh_a * dk_a, nh_a * dv_a
    nh_b, dv_b = cache_v.shape[2], cache_v.shape[3]
    dqk_b = cache_k.shape[4]
    n_pool, page = cache_k.shape[0], cache_k.shape[1]
    tm = 256
    assert n_a == 1 and n_b == 1, "one GLA layer and one differential-attention layer"
    assert nt <= SUBLANES and t % tm == 0 and (nseq * nt) % tm == 0

    c_all = jnp.concatenate([c_prompt, c_sample], axis=0)
    xp = x_prompt.reshape(b * t, d)
    xs = x_sample.reshape(nseq * nt, d)
    tiles_p = t // tm

    def split_mod(mod):
        mp = mod[:b].reshape(b, 1, -1)
        ms = jnp.repeat(mod[b:], nt, axis=0).reshape((nseq * nt) // tm, tm, -1)
        return mp, ms

    pad8 = lambda a: jnp.pad(a.reshape(nseq, nt, -1), ((0, 0), (0, SUBLANES - nt), (0, 0)))

    l = 0
    moda_p, moda_s = split_mod(_ada(c_all, ada_w_a[l], ada_b_a[l]))
    rank = w_g2_a.shape[1]
    w_in_pad = jnp.pad(w_in_a[l], ((0, 0), (0, LANES - rank))).astype(BF16)
    wg2_pad = jnp.pad(w_g2_a[l], ((0, LANES - rank), (0, 0))).astype(BF16)
    gla_args = (norm_a[l], w_in_pad, wg2_pad, b_g_a[l], ka, va, dk_a)
    qp, kp_, vp_, zp, lgp = _gla_in(xp, moda_p, tiles_p, tm, *gla_args)
    qs, ks_, vs_, zs, lgs = _gla_in(xs, moda_s, 1, tm, *gla_args)
    op, state_p = _gla_prompt(qp, kp_, vp_, lgp, b, t, nh_a, dk_a, dv_a)
    os8, state_s = _gla_sample(pad8(qs).reshape(-1, ka), pad8(ks_).reshape(-1, ka),
                               pad8(vs_).reshape(-1, va), pad8(lgs).reshape(-1, ka),
                               state_gla[l], nh_a, dk_a, dv_a)
    os_ = os8.reshape(nseq, SUBLANES, va)[:, :nt].reshape(nseq * nt, va)

    j = 0
    lam_init = 0.8 - 0.6 * math.exp(-0.3 * (n_a + j))
    modkv_p, modkv_s = split_mod(_ada(c_all, ada_w_kv, ada_b_kv))
    modb_p, modb_s = split_mod(_ada(c_all, ada_w_b[j], ada_b_b[j]))
    mid_w = (_tile(onorm_a[l], nh_a), w_out_a[l].astype(BF16), norm_kv.reshape(1, d),
             w_k.astype(BF16), w_v.astype(BF16), _tile(g_k, 2 * nh_b), norm_b[j].reshape(1, d),
             w_in_b[j].astype(BF16), _tile(g_q[j], 2 * nh_b), nh_a, dv_a, dqk_b ** -0.5 * LOG2E)
    y1p, ktp, vp, kbp, vtp, qtp, zbp = _mid(
        op, zp, xp, moda_p, modkv_p, modb_p, tiles_p, tm, *mid_w, token_minor=True)
    y1s, ks, vs, kbs, vbs, qbs, zbs = _mid(
        os_, zs, xs, moda_s, modkv_s, modb_s, 1, tm, *mid_w, token_minor=False)

    w = nh_b * dv_b
    lvec = jnp.stack([lambda_q1[j], lambda_k1[j], lambda_q2[j], lambda_k2[j]]).astype(F32)
    slopes = 2.0 ** (-8.0 * jnp.arange(1, nh_b + 1, dtype=F32) / nh_b) * LOG2E
    ap = _attn_prompt(qtp, kbp.reshape(b, t, w), vtp, slopes, lvec, nh_b, lam_init)
    past = page_table.shape[1] * page
    groups = 2 * nh_b
    slope_col = jnp.repeat(slopes, 2 * nt).reshape(nt * groups, 1)
    qpos_col = (past + jnp.tile(jnp.arange(nt, dtype=jnp.int32), groups)).reshape(nt * groups, 1)
    pad_rows = lambda a, r: jnp.pad(a.reshape(nseq, nt, w), ((0, 0), (0, r - nt), (0, 0)))
    q_rep = jnp.tile(qbs.reshape(nseq, nt, w), (1, SUBLANES // nt, 1))
    k_t = cache_k.transpose(0, 2, 3, 4, 1).reshape(n_pool, w, page)
    v_rows = cache_v.reshape(n_pool, page * nh_b, dv_b)
    as8 = _attn_sample(q_rep, k_t, v_rows, page_table,
                       pad_rows(kbs, 2 * SUBLANES), pad_rows(vbs, 2 * SUBLANES),
                       lvec, slope_col, qpos_col, nh_b, nt, lam_init)
    as_ = as8[:, :nt].reshape(nseq * nt, w)

    fin_w = (_tile(subln_b[j], nh_b), w_out_b[j].astype(BF16), dv_b, 1.0 - lam_init)
    yp = _final(ap.reshape(b * t, w), zbp, y1p, modb_p, tiles_p, tm, *fin_w)
    ys = _final(as_, zbs, y1s, modb_s, 1, tm, *fin_w)

    kp = ktp.reshape(b, nh_b, 2, dqk_b, t).transpose(0, 4, 1, 2, 3)
    return (yp.reshape(b, t, d), ys.reshape(nseq, nt, d), state_p[None], state_s[None],
            kp, vp.reshape(b, t, nh_b, dv_b),
            ks.reshape(nseq, nt, nh_b, 2, dqk_b), vs.reshape(nseq, nt, nh_b, dv_b))
```

```python
import functools
import math

import jax
import jax.numpy as jnp
from jax import lax
from jax.experimental import pallas as pl
from jax.experimental.pallas import tpu as pltpu

F32 = jnp.float32
BF16 = jnp.bfloat16
EPS = 1e-6
GATE_NORMALIZER = 16.0
LOG2E = math.log2(math.e)
LANES = 128
SUBLANES = 8
VMEM_LIMIT = 56 * 1024 * 1024

NT_DIMS = (((1,), (1,)), ((), ()))
TN_DIMS = (((0,), (0,)), ((), ()))


def _params(*sem):
    return pltpu.CompilerParams(dimension_semantics=sem, vmem_limit_bytes=VMEM_LIMIT)


def _silu(x):
    return x / (1.0 + jnp.exp(-x))


def _rms_rows(x):
    return x * lax.rsqrt(jnp.mean(x * x, axis=-1, keepdims=True) + EPS)


def _const_spec(shape):
    zeros = (0,) * len(shape)
    return pl.BlockSpec(shape, lambda *_: zeros, pipeline_mode=pl.Buffered(1))


def _ada_kernel(c_ref, w_ref, b_ref, o_ref):
    s = _silu(c_ref[...]).astype(BF16)
    o_ref[...] = jnp.dot(s, w_ref[...].astype(BF16), preferred_element_type=F32) + b_ref[...]


def _ada(c, w, b, tn=512):
    n, d = c.shape
    m = w.shape[1]
    return pl.pallas_call(
        _ada_kernel,
        grid=(m // tn,),
        in_specs=[pl.BlockSpec((n, d), lambda j: (0, 0)),
                  pl.BlockSpec((d, tn), lambda j: (0, j)),
                  pl.BlockSpec((1, tn), lambda j: (0, j))],
        out_specs=pl.BlockSpec((n, tn), lambda j: (0, j)),
        out_shape=jax.ShapeDtypeStruct((n, m), F32),
        compiler_params=_params("arbitrary"),
        name="ada_mod",
    )(c, w, b.reshape(1, m))


def _mod_spec(mod3, tm, tiles_per_group):
    r = mod3.shape[1]
    w = mod3.shape[2]
    if r == 1:
        return pl.BlockSpec((None, 1, w), lambda i: (i // tiles_per_group, 0, 0))
    return pl.BlockSpec((None, tm, w), lambda i: (i, 0, 0))


def _row_spec(tm, w):
    return pl.BlockSpec((tm, w), lambda i: (i, 0))


def _gla_in_kernel(x_ref, mod_ref, g_ref, w_ref, wg2_ref, bg_ref,
                   q_ref, k_ref, v_ref, z_ref, lg_ref, *, ka, va, dk):
    x = x_ref[...]
    d = x.shape[-1]
    mod = mod_ref[...]
    h = _rms_rows(x) * g_ref[...] * (1.0 + mod[:, d:2 * d]) + mod[:, 0:d]
    p = jnp.dot(h.astype(BF16), w_ref[...], preferred_element_type=F32)
    q_ref[...] = p[:, 0:ka] * (dk ** -0.5)
    k_ref[...] = p[:, ka:2 * ka]
    v_ref[...] = p[:, 2 * ka:2 * ka + va]
    z_ref[...] = p[:, 2 * ka + va:2 * ka + 2 * va]
    glow = p[:, 2 * ka + 2 * va:]
    g2 = jnp.dot(glow.astype(BF16), wg2_ref[...], preferred_element_type=F32) + bg_ref[...]
    logsig = jnp.minimum(g2, 0.0) - jnp.log1p(jnp.exp(-jnp.abs(g2)))
    lg_ref[...] = logsig * (1.0 / GATE_NORMALIZER)


def _gla_in(x2, mod3, tiles_per_group, tm, g, w_pad, wg2_pad, bg, ka, va, dk):
    n, d = x2.shape
    wcols = w_pad.shape[1]
    kern = functools.partial(_gla_in_kernel, ka=ka, va=va, dk=dk)
    outs = [jax.ShapeDtypeStruct((n, ka), F32), jax.ShapeDtypeStruct((n, ka), F32),
            jax.ShapeDtypeStruct((n, va), F32), jax.ShapeDtypeStruct((n, va), F32),
            jax.ShapeDtypeStruct((n, ka), F32)]
    return pl.pallas_call(
        kern,
        grid=(n // tm,),
        in_specs=[_row_spec(tm, d), _mod_spec(mod3, tm, tiles_per_group),
                  _const_spec((1, d)), _const_spec((d, wcols)),
                  _const_spec(wg2_pad.shape), _const_spec((1, ka))],
        out_specs=[_row_spec(tm, ka), _row_spec(tm, ka), _row_spec(tm, va),
                   _row_spec(tm, va), _row_spec(tm, ka)],
        out_shape=outs,
        compiler_params=_params("arbitrary"),
        name="gla_in",
    )(x2, mod3, g.reshape(1, d), w_pad, wg2_pad, bg.reshape(1, ka))


def _intra_masks(row, col, c):
    same8 = (row >> 3) == (col >> 3)
    diag = jnp.where(same8, row - col, -1)
    levels = []
    n = 2 * SUBLANES
    while n <= c:
        half = n // 2
        sh = half.bit_length() - 1
        rb = row >> sh
        cb = col >> sh
        levels.append((n, jnp.where((cb & 1) == 0, rb - cb, 0) == 1))
        n *= 2
    return diag, levels


def _intra_scores(q, k, cum, diag, levels):
    c, dk = q.shape
    a = jnp.zeros((c, c), F32)
    for d in range(SUBLANES):
        if d == 0:
            p = q * k
        else:
            p = q * pltpu.roll(k, d, 0) * jnp.exp(cum - pltpu.roll(cum, d, 0))
        a = jnp.where(diag == d, jnp.sum(p, axis=-1, keepdims=True), a)
    for n, mask in levels:
        half = n // 2
        ref = jnp.concatenate(
            [jnp.broadcast_to(cum[i * n + half - 1:i * n + half, :], (n, dk)) for i in range(c // n)],
            axis=0)
        qs = (q * jnp.exp(cum - ref)).astype(BF16)
        ks = (k * jnp.exp(ref - cum)).astype(BF16)
        a = jnp.where(mask, lax.dot_general(qs, ks, NT_DIMS, preferred_element_type=F32), a)
    return a


def _gla_prompt_kernel(q_ref, k_ref, v_ref, lg_ref, o_ref, s_ref, st_scr, *, c, nh, dk, dv):
    ci = pl.program_id(1)

    @pl.when(ci == 0)
    def _():
        st_scr[...] = jnp.zeros_like(st_scr)

    row = lax.broadcasted_iota(jnp.int32, (c, c), 0)
    col = lax.broadcasted_iota(jnp.int32, (c, c), 1)
    tril = jnp.where(col <= row, 1.0, 0.0).astype(F32)
    cum_all = jnp.dot(tril, lg_ref[...], precision=lax.Precision.HIGHEST, preferred_element_type=F32)
    diag, levels = _intra_masks(row, col, c)
    for h in range(nh):
        sk = slice(h * dk, (h + 1) * dk)
        sv = slice(h * dv, (h + 1) * dv)
        qh = q_ref[:, sk]
        kh = k_ref[:, sk]
        vb = v_ref[:, sv].astype(BF16)
        cum = cum_all[:, sk]
        st = st_scr[h]
        last = cum[c - 1:c, :]
        qe = (qh * jnp.exp(cum)).astype(BF16)
        o = lax.dot_general(qe, st.astype(BF16), NT_DIMS, preferred_element_type=F32)
        a = _intra_scores(qh, kh, cum, diag, levels)
        o_ref[:, sv] = o + jnp.dot(a.astype(BF16), vb, preferred_element_type=F32)
        kd = (kh * jnp.exp(last - cum)).astype(BF16)
        st_scr[h] = jnp.exp(last) * st + lax.dot_general(vb, kd, TN_DIMS, preferred_element_type=F32)

    @pl.when(ci == pl.num_programs(1) - 1)
    def _():
        for h in range(nh):
            s_ref[h] = st_scr[h].T


def _gla_prompt(q, k, v, lg, b, t, nh, dk, dv, c=64):
    n = b * t
    nc = t // c
    kern = functools.partial(_gla_prompt_kernel, c=c, nh=nh, dk=dk, dv=dv)
    row = lambda w: pl.BlockSpec((c, w), lambda bi, ci: (bi * nc + ci, 0))
    return pl.pallas_call(
        kern,
        grid=(b, nc),
        in_specs=[row(nh * dk), row(nh * dk), row(nh * dv), row(nh * dk)],
        out_specs=[row(nh * dv),
                   pl.BlockSpec((None, nh, dk, dv), lambda bi, ci: (bi, 0, 0, 0))],
        out_shape=[jax.ShapeDtypeStruct((n, nh * dv), F32),
                   jax.ShapeDtypeStruct((b, nh, dk, dv), F32)],
        scratch_shapes=[pltpu.VMEM((nh, dv, dk), F32)],
        compiler_params=_params("arbitrary", "arbitrary"),
        name="gla_prompt",
    )(q, k, v, lg)


def _gla_sample_kernel(q_ref, k_ref, v_ref, lg_ref, s0_ref, o_ref, s_ref, *, g, nh, dk, dv):
    r = g * SUBLANES
    row = lax.broadcasted_iota(jnp.int32, (r, r), 0)
    col = lax.broadcasted_iota(jnp.int32, (r, r), 1)
    diag, _ = _intra_masks(row, col, SUBLANES)
    tril = jnp.where(diag >= 0, 1.0, 0.0).astype(F32)
    cum_all = jnp.dot(tril, lg_ref[...], precision=lax.Precision.HIGHEST, preferred_element_type=F32)
    for h in range(nh):
        sk = slice(h * dk, (h + 1) * dk)
        sv = slice(h * dv, (h + 1) * dv)
        qh = q_ref[:, sk]
        kh = k_ref[:, sk]
        vh = v_ref[:, sv]
        cum = cum_all[:, sk]
        a = _intra_scores(qh, kh, cum, diag, [])
        o_intra = jnp.dot(a.astype(BF16), vh.astype(BF16), preferred_element_type=F32)
        qe = qh * jnp.exp(cum)
        first_row = lax.broadcasted_iota(jnp.int32, (SUBLANES, dk), 0) == 0
        ones = jnp.ones((SUBLANES, LANES), F32)
        for i in range(g):
            rs = slice(i * SUBLANES, (i + 1) * SUBLANES)
            st = s0_ref[i, h]
            o_ref[rs, sv] = o_intra[rs] + jnp.dot(
                qe[rs].astype(BF16), st.astype(BF16), preferred_element_type=F32)
            last = cum[(i + 1) * SUBLANES - 1:(i + 1) * SUBLANES, :]
            kd = (kh[rs] * jnp.exp(last - cum[rs])).astype(BF16)
            decay_row = jnp.where(first_row, jnp.broadcast_to(jnp.exp(last), (SUBLANES, dk)), 0.0)
            decay = lax.dot_general(decay_row, ones, TN_DIMS, precision=lax.Precision.HIGHEST,
                                    preferred_element_type=F32)
            s_ref[i, h] = jnp.concatenate([decay] * (dv // LANES), axis=1) * st + lax.dot_general(
                kd, vh[rs].astype(BF16), TN_DIMS, preferred_element_type=F32)


def _gla_sample(q, k, v, lg, s0, nh, dk, dv, g=8):
    nseq = s0.shape[0]
    r = g * SUBLANES
    kern = functools.partial(_gla_sample_kernel, g=g, nh=nh, dk=dk, dv=dv)
    st_spec = pl.BlockSpec((g, nh, dk, dv), lambda i: (i, 0, 0, 0))
    return pl.pallas_call(
        kern,
        grid=(nseq // g,),
        in_specs=[_row_spec(r, nh * dk), _row_spec(r, nh * dk), _row_spec(r, nh * dv),
                  _row_spec(r, nh * dk), st_spec],
        out_specs=[_row_spec(r, nh * dv), st_spec],
        out_shape=[jax.ShapeDtypeStruct((nseq * SUBLANES, nh * dv), F32),
                   jax.ShapeDtypeStruct(s0.shape, F32)],
        compiler_params=_params("arbitrary"),
        name="gla_sample",
    )(q, k, v, lg, s0)


def _rms_groups64(x, lane_lo):
    outs = []
    for j in range(x.shape[-1] // LANES):
        xj = x[:, j * LANES:(j + 1) * LANES]
        sq = xj * xj
        s0 = jnp.sum(jnp.where(lane_lo, sq, 0.0), axis=-1, keepdims=True)
        s1 = jnp.sum(jnp.where(lane_lo, 0.0, sq), axis=-1, keepdims=True)
        ms = jnp.where(lane_lo, s0, s1) * (2.0 / LANES)
        outs.append(xj * lax.rsqrt(ms + EPS))
    return jnp.concatenate(outs, axis=-1)


def _mid_kernel(o_ref, z_ref, x_ref, moda_ref, modkv_ref, modb_ref,
                onorm_ref, wout_ref, gkv_ref, wk_ref, wv_ref, gk_ref, gb_ref, winb_ref, gq_ref,
                y_ref, k_ref, v_ref, kb_ref, vb_ref, qb_ref, zb_ref, *, nh, dv, qk_scale, token_minor):
    d = x_ref.shape[-1]
    o = o_ref[...]
    on = jnp.concatenate([_rms_rows(o[:, h * dv:(h + 1) * dv]) for h in range(nh)], axis=-1)
    on = on * onorm_ref[...]
    u = (on * _silu(z_ref[...])).astype(BF16)
    out = jnp.dot(u, wout_ref[...], preferred_element_type=F32)
    y = x_ref[...] + moda_ref[:, 2 * d:3 * d] * out
    y_ref[...] = y
    yn = _rms_rows(y)
    lane_lo = lax.broadcasted_iota(jnp.int32, (1, LANES), 1) < (LANES // 2)

    modkv = modkv_ref[...]
    h2 = (yn * gkv_ref[...] * (1.0 + modkv[:, d:2 * d]) + modkv[:, 0:d]).astype(BF16)
    kk = _rms_groups64(jnp.dot(h2, wk_ref[...], preferred_element_type=F32), lane_lo) * gk_ref[...]
    k_ref[...] = kk.T if token_minor else kk
    kb_ref[...] = kk.astype(BF16)
    vv = jnp.dot(h2, wv_ref[...], preferred_element_type=F32)
    v_ref[...] = vv
    vb_ref[...] = (vv.T if token_minor else vv).astype(BF16)

    modb = modb_ref[...]
    h3 = (yn * gb_ref[...] * (1.0 + modb[:, d:2 * d]) + modb[:, 0:d]).astype(BF16)
    qz = jnp.dot(h3, winb_ref[...], preferred_element_type=F32)
    qb = qz.shape[-1] - zb_ref.shape[-1]
    qq = _rms_groups64(qz[:, 0:qb], lane_lo) * (gq_ref[...] * qk_scale)
    qb_ref[...] = (qq.T if token_minor else qq).astype(BF16)
    zb_ref[...] = qz[:, qb:]


def _mid(o, z, x2, moda3, modkv3, modb3, tiles_per_group, tm,
         onorm_t, wout, gkv, wk, wv, gk_t, gb, winb, gq_t, nh, dv, qk_scale, token_minor):
    n, d = x2.shape
    qb = wk.shape[1]
    zb = winb.shape[1] - qb
    kern = functools.partial(_mid_kernel, nh=nh, dv=dv, qk_scale=qk_scale, token_minor=token_minor)
    ms = lambda m: _mod_spec(m, tm, tiles_per_group)
    if token_minor:
        groups = n // (tm * tiles_per_group)
        t = tm * tiles_per_group
        feat = lambda w, dt: jax.ShapeDtypeStruct((groups, w, t), dt)
        feat_spec = lambda w: pl.BlockSpec(
            (None, w, tm), lambda i: (i // tiles_per_group, 0, i % tiles_per_group))
    else:
        feat = lambda w, dt: jax.ShapeDtypeStruct((n, w), dt)
        feat_spec = lambda w: _row_spec(tm, w)
    outs = [jax.ShapeDtypeStruct((n, d), F32), feat(qb, F32),
            jax.ShapeDtypeStruct((n, zb), F32), jax.ShapeDtypeStruct((n, qb), BF16),
            feat(zb, BF16), feat(qb, BF16),
            jax.ShapeDtypeStruct((n, zb), F32)]
    out_specs = [_row_spec(tm, d), feat_spec(qb), _row_spec(tm, zb), _row_spec(tm, qb),
                 feat_spec(zb), feat_spec(qb), _row_spec(tm, zb)]
    return pl.pallas_call(
        kern,
        grid=(n // tm,),
        in_specs=[_row_spec(tm, o.shape[1]), _row_spec(tm, z.shape[1]), _row_spec(tm, d),
                  ms(moda3), ms(modkv3), ms(modb3),
                  _const_spec(onorm_t.shape), _const_spec(wout.shape), _const_spec(gkv.shape),
                  _const_spec(wk.shape), _const_spec(wv.shape), _const_spec(gk_t.shape),
                  _const_spec(gb.shape), _const_spec(winb.shape), _const_spec(gq_t.shape)],
        out_specs=out_specs,
        out_shape=outs,
        compiler_params=_params("arbitrary"),
        name="mid_proj",
    )(o, z, x2, moda3, modkv3, modb3, onorm_t, wout, gkv, wk, wv, gk_t, gb, winb, gq_t)


def _lambda_full(l_ref, lam_init):
    lv = l_ref[...]
    s1 = jnp.sum(lv[0:1] * lv[1:2], axis=-1, keepdims=True)
    s2 = jnp.sum(lv[2:3] * lv[3:4], axis=-1, keepdims=True)
    return jnp.exp(s1) - jnp.exp(s2) + lam_init


def _attn_prompt_kernel(slope_ref, l_ref, qt_ref, k_ref, vt_ref, o_ref, *, tq, tk, hps, lam_init):
    hg = pl.program_id(1)
    qi = pl.program_id(2)
    half = LANES // 2
    zeros = jnp.zeros((half, tq), BF16)
    krow = lax.broadcasted_iota(jnp.int32, (tk, LANES), 0).astype(F32)
    ones_rows = jnp.ones((2 * SUBLANES, tk), BF16)
    slopes, qqs, kbiases = [], [], []
    for hh in range(hps):
        slope = slope_ref[hg * hps + hh]
        qt = qt_ref[hh * LANES:(hh + 1) * LANES, :]
        qqs.append(jnp.concatenate([jnp.concatenate([qt[0:half], zeros], axis=0),
                                    jnp.concatenate([zeros, qt[half:LANES]], axis=0)], axis=1))
        kbiases.append(jnp.concatenate([slope * krow] * (2 * tq // LANES), axis=1))
        slopes.append(slope)

    def tile(j, carries, causal):
        start = pl.multiple_of(j * tk, tk)
        heads = range(hps)
        cols = [slice(hh * LANES, (hh + 1) * LANES) for hh in heads]
        offset = (j * tk - qi * tq).astype(F32)
        scores = [jnp.dot(k_ref[pl.ds(start, tk), cols[hh]], qqs[hh], preferred_element_type=F32)
                  + kbiases[hh] for hh in heads]
        if causal:
            kk = lax.broadcasted_iota(jnp.int32, (tk, 2 * tq), 0)
            qpos = lax.broadcasted_iota(jnp.int32, (tk, 2 * tq), 1)
            keep = kk <= jnp.where(qpos >= tq, qpos - tq, qpos)
            scores = [jnp.where(keep, s, -jnp.inf) for s in scores]
        stats, probs = [], []
        for hh in heads:
            m = carries[hh][0]
            shift = slopes[hh] * offset
            m_new = jnp.maximum(m, jnp.max(scores[hh], axis=0, keepdims=True) + shift)
            stats.append((m_new, jnp.exp2(m - m_new)))
            probs.append(jnp.exp2(scores[hh] - (m_new - shift)).astype(BF16))
        pvs = [jnp.dot(jnp.concatenate([vt_ref[cols[hh], pl.ds(start, tk)], ones_rows], axis=0), probs[hh],
                       preferred_element_type=F32) for hh in heads]
        out = []
        for hh in heads:
            m_new, alpha = stats[hh]
            _, l, acc = carries[hh]
            out.append((m_new, alpha * l + pvs[hh][LANES:LANES + 1], alpha * acc + pvs[hh][0:LANES]))
        return tuple(out)

    init = tuple((jnp.full((1, 2 * tq), -jnp.inf, F32), jnp.zeros((1, 2 * tq), F32),
                  jnp.zeros((LANES, 2 * tq), F32)) for _ in range(hps))
    carries = lax.fori_loop(0, qi, lambda j, c: tile(j, c, False), init)
    carries = tile(qi, carries, True)
    lam = _lambda_full(l_ref, lam_init)
    for hh in range(hps):
        _, l, acc = carries[hh]
        r = acc * (1.0 / l)
        o_ref[:, hh * LANES:(hh + 1) * LANES] = (r[:, 0:tq] - lam * r[:, tq:2 * tq]).T


def _attn_prompt(qt, kb, vt, slopes, lvec, nh, lam_init, tq=512, tk=512, hps=2):
    b, t, w = kb.shape
    assert tq == tk, "the causal tile is peeled as the last key tile"
    assert nh % hps == 0
    gw = hps * LANES
    kern = functools.partial(_attn_prompt_kernel, tq=tq, tk=tk, hps=hps, lam_init=lam_init)
    return pl.pallas_call(
        kern,
        grid=(b, nh // hps, t // tq),
        in_specs=[pl.BlockSpec(memory_space=pltpu.SMEM), _const_spec(lvec.shape),
                  pl.BlockSpec((None, gw, tq), lambda bi, hi, qi: (bi, hi, qi)),
                  pl.BlockSpec((None, t, gw), lambda bi, hi, qi: (bi, 0, hi)),
                  pl.BlockSpec((None, gw, t), lambda bi, hi, qi: (bi, hi, 0))],
        out_specs=pl.BlockSpec((None, tq, gw), lambda bi, hi, qi: (bi, qi, hi)),
        out_shape=jax.ShapeDtypeStruct((b, t, w), F32),
        compiler_params=_params("arbitrary", "arbitrary", "arbitrary"),
        name="attn_prompt",
    )(slopes, lvec, qt, kb, vt)


def _attn_sample_kernel(pt_ref, l_ref, slope_ref, qpos_ref, q_ref, *refs,
                        nh, nt, pp, past, page, lam_init):
    del pt_ref
    k_refs, v_refs = refs[0:pp], refs[pp:2 * pp]
    kn_ref, vn_ref, o_ref, qbd_scr, spread_scr, own_scr, m_scr, l_scr, acc_scr = refs[2 * pp:]
    p = pl.program_id(1)
    n_steps = past // (page * pp)
    rows, w = qbd_scr.shape
    hr = 2 * nt
    dv = acc_scr.shape[-1]

    @pl.when((pl.program_id(0) == 0) & (p == 0))
    def _():
        tok = lax.broadcasted_iota(jnp.int32, (page, page * nh), 0)
        slot = lax.broadcasted_iota(jnp.int32, (page, page * nh), 1)
        spread_scr[...] = jnp.where(slot // nh == tok, 1.0, 0.0).astype(BF16)
        slot_head = lax.broadcasted_iota(jnp.int32, (rows, page * nh), 1) % nh
        row_head = lax.broadcasted_iota(jnp.int32, (rows, page * nh), 0) // hr
        own_scr[...] = jnp.where(slot_head == row_head, 1.0, 0.0)

    @pl.when(p == 0)
    def _():
        q = q_ref[...].astype(F32)
        qt = jnp.concatenate([q] * (rows // SUBLANES), axis=0)
        lane_grp = lax.broadcasted_iota(jnp.int32, (rows, w), 1) // (LANES // 2)
        row_grp = lax.broadcasted_iota(jnp.int32, (rows, w), 0) // nt
        qbd_scr[...] = jnp.where(lane_grp == row_grp, qt, 0.0).astype(BF16)
        m_scr[...] = jnp.full_like(m_scr, -jnp.inf)
        l_scr[...] = jnp.zeros_like(l_scr)
        acc_scr[...] = jnp.zeros_like(acc_scr)

    def softmax_step(s):
        m = m_scr[...]
        m_new = jnp.maximum(m, jnp.max(s, axis=-1, keepdims=True))
        alpha = jnp.exp2(m - m_new)
        pr = jnp.exp2(s - m_new)
        l_scr[...] = alpha * l_scr[...] + jnp.sum(pr, axis=-1, keepdims=True)
        m_scr[...] = m_new
        return alpha, pr

    @pl.when(p < n_steps)
    def _():
        qbd = qbd_scr[...]
        s = jnp.concatenate(
            [jnp.dot(qbd, k_refs[i][...].astype(BF16), preferred_element_type=F32) for i in range(pp)],
            axis=1)
        kpos = p * (pp * page) + lax.broadcasted_iota(jnp.int32, (1, pp * page), 1)
        s = s - slope_ref[...] * (qpos_ref[...] - kpos).astype(F32)
        alpha, pr = softmax_step(s)
        prb = pr.astype(BF16)
        pv = jnp.zeros((rows, dv), F32)
        for i in range(pp):
            wide = jnp.dot(prb[:, i * page:(i + 1) * page], spread_scr[...], preferred_element_type=F32)
            wide = (wide * own_scr[...]).astype(BF16)
            pv = pv + jnp.dot(wide, v_refs[i][...].astype(BF16), preferred_element_type=F32)
        acc_scr[...] = alpha * acc_scr[...] + pv

    @pl.when(p == n_steps)
    def _():
        nrows = kn_ref.shape[0]
        idx = lax.broadcasted_iota(jnp.int32, (1, nrows), 1)
        dist = qpos_ref[...] - (past + idx)
        s = lax.dot_general(qbd_scr[...], kn_ref[...], NT_DIMS, preferred_element_type=F32)
        s = jnp.where((idx < nt) & (dist >= 0), s - slope_ref[...] * dist.astype(F32), -jnp.inf)
        alpha, pr = softmax_step(s)
        pv = jnp.dot(pr.astype(BF16), vn_ref[...], preferred_element_type=F32)
        own = jnp.concatenate([pv[h * hr:(h + 1) * hr, h * dv:(h + 1) * dv] for h in range(nh)], axis=0)
        r = (alpha * acc_scr[...] + own) / l_scr[...]
        diff = r - _lambda_full(l_ref, lam_init) * pltpu.roll(r, rows - nt, 0)
        for h in range(nh):
            o_ref[:, h * dv:(h + 1) * dv] = diff[h * hr:(h + 1) * hr]


def _attn_sample(q8, k_t, v_rows, page_table, kn, vn, lvec, slope_col, qpos_col, nh, nt, lam_init, pp=8):
    nseq, _, w = q8.shape
    n_pool, _, page = k_t.shape
    dv = v_rows.shape[-1]
    n_pages = page_table.shape[1]
    past = n_pages * page
    nrows = kn.shape[1]
    rows = 2 * nh * nt
    assert 2 * nt == SUBLANES and n_pages % pp == 0
    kern = functools.partial(_attn_sample_kernel, nh=nh, nt=nt, pp=pp, past=past, page=page, lam_init=lam_init)

    def page_of(i):
        return lambda b, p, pt: pt[b, jnp.minimum(p * pp + i, n_pages - 1)]

    k_specs = [pl.BlockSpec((None, w, page), lambda b, p, pt, f=page_of(i): (f(b, p, pt), 0, 0))
               for i in range(pp)]
    v_specs = [pl.BlockSpec((None, page * nh, dv), lambda b, p, pt, f=page_of(i): (f(b, p, pt), 0, 0))
               for i in range(pp)]
    seq_spec = lambda r: pl.BlockSpec((None, r, w), lambda b, p, pt: (b, 0, 0))
    const = lambda shape: pl.BlockSpec(shape, lambda b, p, pt: (0,) * len(shape))
    grid_spec = pltpu.PrefetchScalarGridSpec(
        num_scalar_prefetch=1,
        grid=(nseq, n_pages // pp + 1),
        in_specs=[const(lvec.shape), const(slope_col.shape), const(qpos_col.shape), seq_spec(SUBLANES)]
        + k_specs + v_specs + [seq_spec(nrows), seq_spec(nrows)],
        out_specs=seq_spec(SUBLANES),
        scratch_shapes=[pltpu.VMEM((rows, w), BF16), pltpu.VMEM((page, page * nh), BF16),
                        pltpu.VMEM((rows, page * nh), F32), pltpu.VMEM((rows, 1), F32),
                        pltpu.VMEM((rows, 1), F32), pltpu.VMEM((rows, dv), F32)],
    )
    return pl.pallas_call(
        kern,
        grid_spec=grid_spec,
        out_shape=jax.ShapeDtypeStruct((nseq, SUBLANES, w), F32),
        compiler_params=_params("arbitrary", "arbitrary"),
        name="attn_sample",
    )(page_table, lvec, slope_col, qpos_col, q8, *([k_t] * pp), *([v_rows] * pp), kn, vn)


def _final_kernel(o_ref, z_ref, y_ref, modb_ref, gsub_ref, wout_ref, out_ref, *, dvb, post_scale):
    d = y_ref.shape[-1]
    o = o_ref[...]
    on = jnp.concatenate([_rms_rows(o[:, h * dvb:(h + 1) * dvb]) for h in range(o.shape[-1] // dvb)], axis=-1)
    on = on * gsub_ref[...] * post_scale
    u = (on * _silu(z_ref[...])).astype(BF16)
    out = jnp.dot(u, wout_ref[...], preferred_element_type=F32)
    out_ref[...] = y_ref[...] + modb_ref[:, 2 * d:3 * d] * out


def _final(o, z, y, modb3, tiles_per_group, tm, gsub_t, wout, dvb, post_scale):
    n, d = y.shape
    kern = functools.partial(_final_kernel, dvb=dvb, post_scale=post_scale)
    return pl.pallas_call(
        kern,
        grid=(n // tm,),
        in_specs=[_row_spec(tm, o.shape[1]), _row_spec(tm, z.shape[1]), _row_spec(tm, d),
                  _mod_spec(modb3, tm, tiles_per_group), _const_spec(gsub_t.shape), _const_spec(wout.shape)],
        out_specs=_row_spec(tm, d),
        out_shape=jax.ShapeDtypeStruct((n, d), F32),
        compiler_params=_params("arbitrary"),
        name="final_proj",
    )(o, z, y, modb3, gsub_t, wout)


def _tile(vec, reps):
    return jnp.tile(vec.astype(F32), reps).reshape(1, -1)


def kernel(x_prompt, x_sample, c_prompt, c_sample, state_gla, cache_k, cache_v, page_table, norm_a, ada_w_a, ada_b_a, w_in_a, w_g2_a, b_g_a, onorm_a, w_out_a, norm_kv, ada_w_kv, ada_b_kv, w_k, w_v, g_k, norm_b, ada_w_b, ada_b_b, w_in_b, g_q, lambda_q1, lambda_k1, lambda_q2, lambda_k2, subln_b, w_out_b):
    b, t, d = x_prompt.shape
    nseq, nt, _ = x_sample.shape
    n_a, _, nh_a, dk_a, dv_a = state_gla.shape
    n_b = norm_b.shape[0]
    ka, va = nh_a * dk_a, nh_a * dv_a
    nh_b, dv_b = cache_v.shape[2], cache_v.shape[3]
    dqk_b = cache_k.shape[4]
    n_pool, page = cache_k.shape[0], cache_k.shape[1]
    tm = 256
    assert n_a == 1 and n_b == 1, "one GLA layer and one differential-attention layer"
    assert nt <= SUBLANES and t % tm == 0 and (nseq * nt) % tm == 0

    c_all = jnp.concatenate([c_prompt, c_sample], axis=0)
    xp = x_prompt.reshape(b * t, d)
    xs = x_sample.reshape(nseq * nt, d)
    tiles_p = t // tm

    def split_mod(mod):
        mp = mod[:b].reshape(b, 1, -1)
        ms = jnp.repeat(mod[b:], nt, axis=0).reshape((nseq * nt) // tm, tm, -1)
        return mp, ms

    pad8 = lambda a: jnp.pad(a.reshape(nseq, nt, -1), ((0, 0), (0, SUBLANES - nt), (0, 0)))

    l = 0
    moda_p, moda_s = split_mod(_ada(c_all, ada_w_a[l], ada_b_a[l]))
    rank = w_g2_a.shape[1]
    w_in_pad = jnp.pad(w_in_a[l], ((0, 0), (0, LANES - rank))).astype(BF16)
    wg2_pad = jnp.pad(w_g2_a[l], ((0, LANES - rank), (0, 0))).astype(BF16)
    gla_args = (norm_a[l], w_in_pad, wg2_pad, b_g_a[l], ka, va, dk_a)
    qp, kp_, vp_, zp, lgp = _gla_in(xp, moda_p, tiles_p, tm, *gla_args)
    qs, ks_, vs_, zs, lgs = _gla_in(xs, moda_s, 1, tm, *gla_args)
    op, state_p = _gla_prompt(qp, kp_, vp_, lgp, b, t, nh_a, dk_a, dv_a)
    os8, state_s = _gla_sample(pad8(qs).reshape(-1, ka), pad8(ks_).reshape(-1, ka),
                               pad8(vs_).reshape(-1, va), pad8(lgs).reshape(-1, ka),
                               state_gla[l], nh_a, dk_a, dv_a)
    os_ = os8.reshape(nseq, SUBLANES, va)[:, :nt].reshape(nseq * nt, va)

    j = 0
    lam_init = 0.8 - 0.6 * math.exp(-0.3 * (n_a + j))
    modkv_p, modkv_s = split_mod(_ada(c_all, ada_w_kv, ada_b_kv))
    modb_p, modb_s = split_mod(_ada(c_all, ada_w_b[j], ada_b_b[j]))
    mid_w = (_tile(onorm_a[l], nh_a), w_out_a[l].astype(BF16), norm_kv.reshape(1, d),
             w_k.astype(BF16), w_v.astype(BF16), _tile(g_k, 2 * nh_b), norm_b[j].reshape(1, d),
             w_in_b[j].astype(BF16), _tile(g_q[j], 2 * nh_b), nh_a, dv_a, dqk_b ** -0.5 * LOG2E)
    y1p, ktp, vp, kbp, vtp, qtp, zbp = _mid(
        op, zp, xp, moda_p, modkv_p, modb_p, tiles_p, tm, *mid_w, token_minor=True)
    y1s, ks, vs, kbs, vbs, qbs, zbs = _mid(
        os_, zs, xs, moda_s, modkv_s, modb_s, 1, tm, *mid_w, token_minor=False)

    w = nh_b * dv_b
    lvec = jnp.stack([lambda_q1[j], lambda_k1[j], lambda_q2[j], lambda_k2[j]]).astype(F32)
    slopes = 2.0 ** (-8.0 * jnp.arange(1, nh_b + 1, dtype=F32) / nh_b) * LOG2E
    ap = _attn_prompt(qtp, kbp.reshape(b, t, w), vtp, slopes, lvec, nh_b, lam_init)
    past = page_table.shape[1] * page
    groups = 2 * nh_b
    slope_col = jnp.repeat(slopes, 2 * nt).reshape(nt * groups, 1)
    qpos_col = (past + jnp.tile(jnp.arange(nt, dtype=jnp.int32), groups)).reshape(nt * groups, 1)
    pad_rows = lambda a, r: jnp.pad(a.reshape(nseq, nt, w), ((0, 0), (0, r - nt), (0, 0)))
    q_rep = jnp.tile(qbs.reshape(nseq, nt, w), (1, SUBLANES // nt, 1))
    k_t = cache_k.transpose(0, 2, 3, 4, 1).reshape(n_pool, w, page)
    v_rows = cache_v.reshape(n_pool, page * nh_b, dv_b)
    as8 = _attn_sample(q_rep, k_t, v_rows, page_table,
                       pad_rows(kbs, 2 * SUBLANES), pad_rows(vbs, 2 * SUBLANES),
                       lvec, slope_col, qpos_col, nh_b, nt, lam_init)
    as_ = as8[:, :nt].reshape(nseq * nt, w)

    fin_w = (_tile(subln_b[j], nh_b), w_out_b[j].astype(BF16), dv_b, 1.0 - lam_init)
    yp = _final(ap.reshape(b * t, w), zbp, y1p, modb_p, tiles_p, tm, *fin_w)
    ys = _final(as_, zbs, y1s, modb_s, 1, tm, *fin_w)

    kp = ktp.reshape(b, nh_b, 2, dqk_b, t).transpose(0, 4, 1, 2, 3)
    return (yp.reshape(b, t, d), ys.reshape(nseq, nt, d), state_p[None], state_s[None],
            kp, vp.reshape(b, t, nh_b, dv_b),
            ks.reshape(nseq, nt, nh_b, 2, dqk_b), vs.reshape(nseq, nt, nh_b, dv_b))
```

```python
import functools
import math

import jax
import jax.numpy as jnp
from jax import lax
from jax.experimental import pallas as pl
from jax.experimental.pallas import tpu as pltpu

F32 = jnp.float32
BF16 = jnp.bfloat16
EPS = 1e-6
GATE_NORMALIZER = 16.0
LOG2E = math.log2(math.e)
LANES = 128
SUBLANES = 8
VMEM_LIMIT = 56 * 1024 * 1024

NT_DIMS = (((1,), (1,)), ((), ()))
TN_DIMS = (((0,), (0,)), ((), ()))


def _params(*sem):
    return pltpu.CompilerParams(dimension_semantics=sem, vmem_limit_bytes=VMEM_LIMIT)


def _silu(x):
    return x / (1.0 + jnp.exp(-x))


def _rms_rows(x):
    return x * lax.rsqrt(jnp.mean(x * x, axis=-1, keepdims=True) + EPS)


def _const_spec(shape):
    zeros = (0,) * len(shape)
    return pl.BlockSpec(shape, lambda *_: zeros, pipeline_mode=pl.Buffered(1))


def _ada_kernel(c_ref, w_ref, b_ref, o_ref):
    s = _silu(c_ref[...]).astype(BF16)
    o_ref[...] = jnp.dot(s, w_ref[...].astype(BF16), preferred_element_type=F32) + b_ref[...]


def _ada(c, w, b, tn=512):
    n, d = c.shape
    m = w.shape[1]
    return pl.pallas_call(
        _ada_kernel,
        grid=(m // tn,),
        in_specs=[pl.BlockSpec((n, d), lambda j: (0, 0)),
                  pl.BlockSpec((d, tn), lambda j: (0, j)),
                  pl.BlockSpec((1, tn), lambda j: (0, j))],
        out_specs=pl.BlockSpec((n, tn), lambda j: (0, j)),
        out_shape=jax.ShapeDtypeStruct((n, m), F32),
        compiler_params=_params("arbitrary"),
        name="ada_mod",
    )(c, w, b.reshape(1, m))


def _mod_spec(mod3, tm, tiles_per_group):
    r = mod3.shape[1]
    w = mod3.shape[2]
    if r == 1:
        return pl.BlockSpec((None, 1, w), lambda i: (i // tiles_per_group, 0, 0))
    return pl.BlockSpec((None, tm, w), lambda i: (i, 0, 0))


def _row_spec(tm, w):
    return pl.BlockSpec((tm, w), lambda i: (i, 0))


def _gla_in_kernel(x_ref, mod_ref, g_ref, w_ref, wg2_ref, bg_ref,
                   q_ref, k_ref, v_ref, z_ref, lg_ref, *, ka, va, dk):
    x = x_ref[...]
    d = x.shape[-1]
    mod = mod_ref[...]
    h = _rms_rows(x) * g_ref[...] * (1.0 + mod[:, d:2 * d]) + mod[:, 0:d]
    p = jnp.dot(h.astype(BF16), w_ref[...], preferred_element_type=F32)
    q_ref[...] = p[:, 0:ka] * (dk ** -0.5)
    k_ref[...] = p[:, ka:2 * ka]
    v_ref[...] = p[:, 2 * ka:2 * ka + va]
    z_ref[...] = p[:, 2 * ka + va:2 * ka + 2 * va]
    glow = p[:, 2 * ka + 2 * va:]
    g2 = jnp.dot(glow.astype(BF16), wg2_ref[...], preferred_element_type=F32) + bg_ref[...]
    logsig = jnp.minimum(g2, 0.0) - jnp.log1p(jnp.exp(-jnp.abs(g2)))
    lg_ref[...] = logsig * (1.0 / GATE_NORMALIZER)


def _gla_in(x2, mod3, tiles_per_group, tm, g, w_pad, wg2_pad, bg, ka, va, dk):
    n, d = x2.shape
    wcols = w_pad.shape[1]
    kern = functools.partial(_gla_in_kernel, ka=ka, va=va, dk=dk)
    outs = [jax.ShapeDtypeStruct((n, ka), F32), jax.ShapeDtypeStruct((n, ka), F32),
            jax.ShapeDtypeStruct((n, va), F32), jax.ShapeDtypeStruct((n, va), F32),
            jax.ShapeDtypeStruct((n, ka), F32)]
    return pl.pallas_call(
        kern,
        grid=(n // tm,),
        in_specs=[_row_spec(tm, d), _mod_spec(mod3, tm, tiles_per_group),
                  _const_spec((1, d)), _const_spec((d, wcols)),
                  _const_spec(wg2_pad.shape), _const_spec((1, ka))],
        out_specs=[_row_spec(tm, ka), _row_spec(tm, ka), _row_spec(tm, va),
                   _row_spec(tm, va), _row_spec(tm, ka)],
        out_shape=outs,
        compiler_params=_params("arbitrary"),
        name="gla_in",
    )(x2, mod3, g.reshape(1, d), w_pad, wg2_pad, bg.reshape(1, ka))


def _cumsum_rows(tril_b, x):
    hi = x.astype(BF16)
    rest = x - hi.astype(F32)
    mid = rest.astype(BF16)
    lo = (rest - mid.astype(F32)).astype(BF16)
    dot = lambda part: jnp.dot(tril_b, part, preferred_element_type=F32)
    return dot(hi) + dot(mid) + dot(lo)


def _level_masks(row, col, max_block):
    levels = []
    n = 2
    while n <= max_block:
        sh = (n // 2).bit_length() - 1
        rb = row >> sh
        cb = col >> sh
        levels.append((n, jnp.where((cb & 1) == 0, rb - cb, 0) == 1))
        n *= 2
    return levels


def _midpoint_rows(cum, n, sub):
    c, dk = cum.shape
    if n == 2:
        return jnp.where((sub & 1) == 1, pltpu.roll(cum, 1, 0), cum)
    if n == 4:
        r = sub & 3
        return jnp.where(r == 0, pltpu.roll(cum, c - 1, 0),
                         jnp.where(r == 1, cum,
                                   jnp.where(r == 2, pltpu.roll(cum, 1, 0), pltpu.roll(cum, 2, 0))))
    half = n // 2
    return jnp.concatenate(
        [jnp.broadcast_to(cum[i * n + half - 1:i * n + half, :], (n, dk)) for i in range(c // n)], axis=0)


def _intra_operands(q, k, cum, levels, sub):
    ops = []
    for n, _ in levels:
        ref = _midpoint_rows(cum, n, sub)
        ops.append(((q * jnp.exp(cum - ref)).astype(BF16), (k * jnp.exp(ref - cum)).astype(BF16)))
    return ops, jnp.sum(q * k, axis=-1, keepdims=True)


def _intra_scores(ops, own, levels, eye):
    a = jnp.where(eye, own, 0.0)
    for (_, mask), (qs, ks) in zip(levels, ops):
        a = jnp.where(mask, lax.dot_general(qs, ks, NT_DIMS, preferred_element_type=F32), a)
    return a


def _gla_prompt_kernel(q_ref, k_ref, v_ref, lg_ref, o_ref, s_ref, st_scr, *, c, nc, nh, dk, dv):
    ci = pl.program_id(1)

    @pl.when(ci == 0)
    def _():
        st_scr[...] = jnp.zeros_like(st_scr)

    row = lax.broadcasted_iota(jnp.int32, (c, c), 0)
    col = lax.broadcasted_iota(jnp.int32, (c, c), 1)
    sub = lax.broadcasted_iota(jnp.int32, (c, dk), 0)
    tril_b = jnp.where(col <= row, 1.0, 0.0).astype(BF16)
    levels = _level_masks(row, col, c)
    pairs = [(j, h) for j in range(nc) for h in range(nh)]
    rows = [slice(j * c, (j + 1) * c) for j in range(nc)]
    sk = [slice(h * dk, (h + 1) * dk) for h in range(nh)]
    sv = [slice(h * dv, (h + 1) * dv) for h in range(nh)]
    cums = [_cumsum_rows(tril_b, lg_ref[rows[j], :]) for j in range(nc)]
    prep = {}
    for j, h in pairs:
        qh, kh, cum = q_ref[rows[j], sk[h]], k_ref[rows[j], sk[h]], cums[j][:, sk[h]]
        last = cum[c - 1:c, :]
        ops, own = _intra_operands(qh, kh, cum, levels, sub)
        prep[j, h] = (ops, own, (qh * jnp.exp(cum)).astype(BF16),
                      (kh * jnp.exp(last - cum)).astype(BF16), jnp.exp(last))
    local = {}
    for j, h in pairs:
        ops, own, _, kd, _ = prep[j, h]
        vb = v_ref[rows[j], sv[h]].astype(BF16)
        a = _intra_scores(ops, own, levels, row == col)
        local[j, h] = (jnp.dot(a.astype(BF16), vb, preferred_element_type=F32),
                       lax.dot_general(vb, kd, TN_DIMS, preferred_element_type=F32))
    for h in range(nh):
        st = st_scr[h]
        for j in range(nc):
            o_intra, update = local[j, h]
            _, _, qe, _, decay = prep[j, h]
            o_ref[rows[j], sv[h]] = o_intra + lax.dot_general(
                qe, st.astype(BF16), NT_DIMS, preferred_element_type=F32)
            st = decay * st + update
        st_scr[h] = st

    @pl.when(ci == pl.num_programs(1) - 1)
    def _():
        for h in range(nh):
            s_ref[h] = st_scr[h].T


def _gla_prompt(q, k, v, lg, b, t, nh, dk, dv, c=64, chunks_per_step=4):
    n = b * t
    rows = c * chunks_per_step
    assert t % rows == 0
    steps = t // rows
    kern = functools.partial(_gla_prompt_kernel, c=c, nc=chunks_per_step, nh=nh, dk=dk, dv=dv)
    row = lambda w: pl.BlockSpec((rows, w), lambda bi, ci: (bi * steps + ci, 0))
    return pl.pallas_call(
        kern,
        grid=(b, steps),
        in_specs=[row(nh * dk), row(nh * dk), row(nh * dv), row(nh * dk)],
        out_specs=[row(nh * dv),
                   pl.BlockSpec((None, nh, dk, dv), lambda bi, ci: (bi, 0, 0, 0))],
        out_shape=[jax.ShapeDtypeStruct((n, nh * dv), F32),
                   jax.ShapeDtypeStruct((b, nh, dk, dv), F32)],
        scratch_shapes=[pltpu.VMEM((nh, dv, dk), F32)],
        compiler_params=_params("arbitrary", "arbitrary"),
        name="gla_prompt",
    )(q, k, v, lg)


def _gla_sample_kernel(q_ref, k_ref, v_ref, lg_ref, s0_ref, o_ref, s_ref, *, g, nh, dk, dv):
    r = g * SUBLANES
    row = lax.broadcasted_iota(jnp.int32, (r, r), 0)
    col = lax.broadcasted_iota(jnp.int32, (r, r), 1)
    sub = lax.broadcasted_iota(jnp.int32, (r, dk), 0)
    same_seq = (row >> 3) == (col >> 3)
    cum_all = _cumsum_rows(jnp.where(same_seq & (col <= row), 1.0, 0.0).astype(BF16), lg_ref[...])
    levels = _level_masks(row, col, SUBLANES)
    for h in range(nh):
        sk = slice(h * dk, (h + 1) * dk)
        sv = slice(h * dv, (h + 1) * dv)
        qh = q_ref[:, sk]
        kh = k_ref[:, sk]
        vh = v_ref[:, sv]
        cum = cum_all[:, sk]
        ops, own = _intra_operands(qh, kh, cum, levels, sub)
        a = _intra_scores(ops, own, levels, row == col)
        o_intra = jnp.dot(a.astype(BF16), vh.astype(BF16), preferred_element_type=F32)
        qe = qh * jnp.exp(cum)
        first_row = lax.broadcasted_iota(jnp.int32, (SUBLANES, dk), 0) == 0
        ones = jnp.ones((SUBLANES, LANES), F32)
        for i in range(g):
            rs = slice(i * SUBLANES, (i + 1) * SUBLANES)
            st = s0_ref[i, h]
            o_ref[rs, sv] = o_intra[rs] + jnp.dot(
                qe[rs].astype(BF16), st.astype(BF16), preferred_element_type=F32)
            last = cum[(i + 1) * SUBLANES - 1:(i + 1) * SUBLANES, :]
            kd = (kh[rs] * jnp.exp(last - cum[rs])).astype(BF16)
            decay_row = jnp.where(first_row, jnp.broadcast_to(jnp.exp(last), (SUBLANES, dk)), 0.0)
            decay = lax.dot_general(decay_row, ones, TN_DIMS, precision=lax.Precision.HIGHEST,
                                    preferred_element_type=F32)
            s_ref[i, h] = jnp.concatenate([decay] * (dv // LANES), axis=1) * st + lax.dot_general(
                kd, vh[rs].astype(BF16), TN_DIMS, preferred_element_type=F32)


def _gla_sample(q, k, v, lg, s0, nh, dk, dv, g=8):
    nseq = s0.shape[0]
    r = g * SUBLANES
    kern = functools.partial(_gla_sample_kernel, g=g, nh=nh, dk=dk, dv=dv)
    st_spec = pl.BlockSpec((g, nh, dk, dv), lambda i: (i, 0, 0, 0))
    return pl.pallas_call(
        kern,
        grid=(nseq // g,),
        in_specs=[_row_spec(r, nh * dk), _row_spec(r, nh * dk), _row_spec(r, nh * dv),
                  _row_spec(r, nh * dk), st_spec],
        out_specs=[_row_spec(r, nh * dv), st_spec],
        out_shape=[jax.ShapeDtypeStruct((nseq * SUBLANES, nh * dv), F32),
                   jax.ShapeDtypeStruct(s0.shape, F32)],
        compiler_params=_params("arbitrary"),
        name="gla_sample",
    )(q, k, v, lg, s0)


def _rms_groups64(x, lane_lo):
    outs = []
    for j in range(x.shape[-1] // LANES):
        xj = x[:, j * LANES:(j + 1) * LANES]
        sq = xj * xj
        s0 = jnp.sum(jnp.where(lane_lo, sq, 0.0), axis=-1, keepdims=True)
        s1 = jnp.sum(jnp.where(lane_lo, 0.0, sq), axis=-1, keepdims=True)
        ms = jnp.where(lane_lo, s0, s1) * (2.0 / LANES)
        outs.append(xj * lax.rsqrt(ms + EPS))
    return jnp.concatenate(outs, axis=-1)


def _mid_kernel(o_ref, z_ref, x_ref, moda_ref, modkv_ref, modb_ref,
                onorm_ref, wout_ref, gkv_ref, wk_ref, wv_ref, gk_ref, gb_ref, winb_ref, gq_ref,
                y_ref, k_ref, v_ref, kb_ref, vb_ref, qb_ref, zb_ref, *, nh, dv, qk_scale, token_minor):
    d = x_ref.shape[-1]
    o = o_ref[...]
    on = jnp.concatenate([_rms_rows(o[:, h * dv:(h + 1) * dv]) for h in range(nh)], axis=-1)
    on = on * onorm_ref[...]
    u = (on * _silu(z_ref[...])).astype(BF16)
    out = jnp.dot(u, wout_ref[...], preferred_element_type=F32)
    y = x_ref[...] + moda_ref[:, 2 * d:3 * d] * out
    y_ref[...] = y
    yn = _rms_rows(y)
    lane_lo = lax.broadcasted_iota(jnp.int32, (1, LANES), 1) < (LANES // 2)

    modkv = modkv_ref[...]
    h2 = (yn * gkv_ref[...] * (1.0 + modkv[:, d:2 * d]) + modkv[:, 0:d]).astype(BF16)
    kk = _rms_groups64(jnp.dot(h2, wk_ref[...], preferred_element_type=F32), lane_lo) * gk_ref[...]
    k_ref[...] = kk.T if token_minor else kk
    kb_ref[...] = kk.astype(BF16)
    vv = jnp.dot(h2, wv_ref[...], preferred_element_type=F32)
    v_ref[...] = vv
    vb_ref[...] = (vv.T if token_minor else vv).astype(BF16)

    modb = modb_ref[...]
    h3 = (yn * gb_ref[...] * (1.0 + modb[:, d:2 * d]) + modb[:, 0:d]).astype(BF16)
    qz = jnp.dot(h3, winb_ref[...], preferred_element_type=F32)
    qb = qz.shape[-1] - zb_ref.shape[-1]
    qq = _rms_groups64(qz[:, 0:qb], lane_lo) * (gq_ref[...] * qk_scale)
    qb_ref[...] = (qq.T if token_minor else qq).astype(BF16)
    zb_ref[...] = qz[:, qb:]


def _mid(o, z, x2, moda3, modkv3, modb3, tiles_per_group, tm,
         onorm_t, wout, gkv, wk, wv, gk_t, gb, winb, gq_t, nh, dv, qk_scale, token_minor):
    n, d = x2.shape
    qb = wk.shape[1]
    zb = winb.shape[1] - qb
    kern = functools.partial(_mid_kernel, nh=nh, dv=dv, qk_scale=qk_scale, token_minor=token_minor)
    ms = lambda m: _mod_spec(m, tm, tiles_per_group)
    if token_minor:
        groups = n // (tm * tiles_per_group)
        t = tm * tiles_per_group
        feat = lambda w, dt: jax.ShapeDtypeStruct((groups, w, t), dt)
        feat_spec = lambda w: pl.BlockSpec(
            (None, w, tm), lambda i: (i // tiles_per_group, 0, i % tiles_per_group))
    else:
        feat = lambda w, dt: jax.ShapeDtypeStruct((n, w), dt)
        feat_spec = lambda w: _row_spec(tm, w)
    outs = [jax.ShapeDtypeStruct((n, d), F32), feat(qb, F32),
            jax.ShapeDtypeStruct((n, zb), F32), jax.ShapeDtypeStruct((n, qb), BF16),
            feat(zb, BF16), feat(qb, BF16),
            jax.ShapeDtypeStruct((n, zb), F32)]
    out_specs = [_row_spec(tm, d), feat_spec(qb), _row_spec(tm, zb), _row_spec(tm, qb),
                 feat_spec(zb), feat_spec(qb), _row_spec(tm, zb)]
    return pl.pallas_call(
        kern,
        grid=(n // tm,),
        in_specs=[_row_spec(tm, o.shape[1]), _row_spec(tm, z.shape[1]), _row_spec(tm, d),
                  ms(moda3), ms(modkv3), ms(modb3),
                  _const_spec(onorm_t.shape), _const_spec(wout.shape), _const_spec(gkv.shape),
                  _const_spec(wk.shape), _const_spec(wv.shape), _const_spec(gk_t.shape),
                  _const_spec(gb.shape), _const_spec(winb.shape), _const_spec(gq_t.shape)],
        out_specs=out_specs,
        out_shape=outs,
        compiler_params=_params("arbitrary"),
        name="mid_proj",
    )(o, z, x2, moda3, modkv3, modb3, onorm_t, wout, gkv, wk, wv, gk_t, gb, winb, gq_t)


def _lambda_full(l_ref, lam_init):
    lv = l_ref[...]
    s1 = jnp.sum(lv[0:1] * lv[1:2], axis=-1, keepdims=True)
    s2 = jnp.sum(lv[2:3] * lv[3:4], axis=-1, keepdims=True)
    return jnp.exp(s1) - jnp.exp(s2) + lam_init


def _attn_prompt_kernel(slope_ref, l_ref, qt_ref, k_ref, vt_ref, o_ref, *, tq, tk, hps, lam_init):
    hg = pl.program_id(1)
    qi = pl.program_id(2)
    half = LANES // 2
    zeros = jnp.zeros((half, tq), BF16)
    krow = lax.broadcasted_iota(jnp.int32, (tk, LANES), 0).astype(F32)
    ones_rows = jnp.ones((2 * SUBLANES, tk), BF16)
    slopes, qqs, kbiases = [], [], []
    for hh in range(hps):
        slope = slope_ref[hg * hps + hh]
        qt = qt_ref[hh * LANES:(hh + 1) * LANES, :]
        qqs.append(jnp.concatenate([jnp.concatenate([qt[0:half], zeros], axis=0),
                                    jnp.concatenate([zeros, qt[half:LANES]], axis=0)], axis=1))
        kbiases.append(jnp.concatenate([slope * krow] * (2 * tq // LANES), axis=1))
        slopes.append(slope)

    def tile(j, carries, causal):
        start = pl.multiple_of(j * tk, tk)
        heads = range(hps)
        cols = [slice(hh * LANES, (hh + 1) * LANES) for hh in heads]
        offset = (j * tk - qi * tq).astype(F32)
        scores = [jnp.dot(k_ref[pl.ds(start, tk), cols[hh]], qqs[hh], preferred_element_type=F32)
                  + kbiases[hh] for hh in heads]
        if causal:
            kk = lax.broadcasted_iota(jnp.int32, (tk, 2 * tq), 0)
            qpos = lax.broadcasted_iota(jnp.int32, (tk, 2 * tq), 1)
            keep = kk <= jnp.where(qpos >= tq, qpos - tq, qpos)
            scores = [jnp.where(keep, s, -jnp.inf) for s in scores]
        stats, probs = [], []
        for hh in heads:
            m = carries[hh][0]
            shift = slopes[hh] * offset
            m_new = jnp.maximum(m, jnp.max(scores[hh], axis=0, keepdims=True) + shift)
            stats.append((m_new, jnp.exp2(m - m_new)))
            probs.append(jnp.exp2(scores[hh] - (m_new - shift)).astype(BF16))
        pvs = [jnp.dot(jnp.concatenate([vt_ref[cols[hh], pl.ds(start, tk)], ones_rows], axis=0), probs[hh],
                       preferred_element_type=F32) for hh in heads]
        out = []
        for hh in heads:
            m_new, alpha = stats[hh]
            _, l, acc = carries[hh]
            out.append((m_new, alpha * l + pvs[hh][LANES:LANES + 1], alpha * acc + pvs[hh][0:LANES]))
        return tuple(out)

    init = tuple((jnp.full((1, 2 * tq), -jnp.inf, F32), jnp.zeros((1, 2 * tq), F32),
                  jnp.zeros((LANES, 2 * tq), F32)) for _ in range(hps))
    carries = lax.fori_loop(0, qi, lambda j, c: tile(j, c, False), init)
    carries = tile(qi, carries, True)
    lam = _lambda_full(l_ref, lam_init)
    for hh in range(hps):
        _, l, acc = carries[hh]
        r = acc * (1.0 / l)
        o_ref[:, hh * LANES:(hh + 1) * LANES] = (r[:, 0:tq] - lam * r[:, tq:2 * tq]).T


def _attn_prompt(qt, kb, vt, slopes, lvec, nh, lam_init, tq=512, tk=512, hps=2):
    b, t, w = kb.shape
    assert tq == tk, "the causal tile is peeled as the last key tile"
    assert nh % hps == 0
    gw = hps * LANES
    kern = functools.partial(_attn_prompt_kernel, tq=tq, tk=tk, hps=hps, lam_init=lam_init)
    return pl.pallas_call(
        kern,
        grid=(b, nh // hps, t // tq),
        in_specs=[pl.BlockSpec(memory_space=pltpu.SMEM), _const_spec(lvec.shape),
                  pl.BlockSpec((None, gw, tq), lambda bi, hi, qi: (bi, hi, qi)),
                  pl.BlockSpec((None, t, gw), lambda bi, hi, qi: (bi, 0, hi)),
                  pl.BlockSpec((None, gw, t), lambda bi, hi, qi: (bi, hi, 0))],
        out_specs=pl.BlockSpec((None, tq, gw), lambda bi, hi, qi: (bi, qi, hi)),
        out_shape=jax.ShapeDtypeStruct((b, t, w), F32),
        compiler_params=_params("arbitrary", "arbitrary", "arbitrary"),
        name="attn_prompt",
    )(slopes, lvec, qt, kb, vt)


def _attn_sample_kernel(pt_ref, l_ref, slope_ref, qpos_ref, q_ref, *refs,
                        nh, nt, pp, past, page, lam_init):
    del pt_ref
    k_refs, v_refs = refs[0:pp], refs[pp:2 * pp]
    kn_ref, vn_ref, o_ref, qbd_scr, spread_scr, own_scr, m_scr, l_scr, acc_scr = refs[2 * pp:]
    p = pl.program_id(1)
    n_steps = past // (page * pp)
    rows, w = qbd_scr.shape
    hr = 2 * nt
    dv = acc_scr.shape[-1]

    @pl.when((pl.program_id(0) == 0) & (p == 0))
    def _():
        tok = lax.broadcasted_iota(jnp.int32, (page, page * nh), 0)
        slot = lax.broadcasted_iota(jnp.int32, (page, page * nh), 1)
        spread_scr[...] = jnp.where(slot // nh == tok, 1.0, 0.0).astype(BF16)
        slot_head = lax.broadcasted_iota(jnp.int32, (rows, page * nh), 1) % nh
        row_head = lax.broadcasted_iota(jnp.int32, (rows, page * nh), 0) // hr
        own_scr[...] = jnp.where(slot_head == row_head, 1.0, 0.0)

    @pl.when(p == 0)
    def _():
        q = q_ref[...].astype(F32)
        qt = jnp.concatenate([q] * (rows // SUBLANES), axis=0)
        lane_grp = lax.broadcasted_iota(jnp.int32, (rows, w), 1) // (LANES // 2)
        row_grp = lax.broadcasted_iota(jnp.int32, (rows, w), 0) // nt
        qbd_scr[...] = jnp.where(lane_grp == row_grp, qt, 0.0).astype(BF16)
        m_scr[...] = jnp.full_like(m_scr, -jnp.inf)
        l_scr[...] = jnp.zeros_like(l_scr)
        acc_scr[...] = jnp.zeros_like(acc_scr)

    def softmax_step(s):
        m = m_scr[...]
        m_new = jnp.maximum(m, jnp.max(s, axis=-1, keepdims=True))
        alpha = jnp.exp2(m - m_new)
        pr = jnp.exp2(s - m_new)
        l_scr[...] = alpha * l_scr[...] + jnp.sum(pr, axis=-1, keepdims=True)
        m_scr[...] = m_new
        return alpha, pr

    @pl.when(p < n_steps)
    def _():
        qbd = qbd_scr[...]
        s = jnp.concatenate(
            [jnp.dot(qbd, k_refs[i][...].astype(BF16), preferred_element_type=F32) for i in range(pp)],
            axis=1)
        kpos = p * (pp * page) + lax.broadcasted_iota(jnp.int32, (1, pp * page), 1)
        s = s - slope_ref[...] * (qpos_ref[...] - kpos).astype(F32)
        alpha, pr = softmax_step(s)
        prb = pr.astype(BF16)
        pv = jnp.zeros((rows, dv), F32)
        for i in range(pp):
            wide = jnp.dot(prb[:, i * page:(i + 1) * page], spread_scr[...], preferred_element_type=F32)
            wide = (wide * own_scr[...]).astype(BF16)
            pv = pv + jnp.dot(wide, v_refs[i][...].astype(BF16), preferred_element_type=F32)
        acc_scr[...] = alpha * acc_scr[...] + pv

    @pl.when(p == n_steps)
    def _():
        nrows = kn_ref.shape[0]
        idx = lax.broadcasted_iota(jnp.int32, (1, nrows), 1)
        dist = qpos_ref[...] - (past + idx)
        s = lax.dot_general(qbd_scr[...], kn_ref[...], NT_DIMS, preferred_element_type=F32)
        s = jnp.where((idx < nt) & (dist >= 0), s - slope_ref[...] * dist.astype(F32), -jnp.inf)
        alpha, pr = softmax_step(s)
        pv = jnp.dot(pr.astype(BF16), vn_ref[...], preferred_element_type=F32)
        own = jnp.concatenate([pv[h * hr:(h + 1) * hr, h * dv:(h + 1) * dv] for h in range(nh)], axis=0)
        r = (alpha * acc_scr[...] + own) / l_scr[...]
        diff = r - _lambda_full(l_ref, lam_init) * pltpu.roll(r, rows - nt, 0)
        for h in range(nh):
            o_ref[:, h * dv:(h + 1) * dv] = diff[h * hr:(h + 1) * hr]


def _attn_sample(q8, k_t, v_rows, page_table, kn, vn, lvec, slope_col, qpos_col, nh, nt, lam_init, pp=8):
    nseq, _, w = q8.shape
    n_pool, _, page = k_t.shape
    dv = v_rows.shape[-1]
    n_pages = page_table.shape[1]
    past = n_pages * page
    nrows = kn.shape[1]
    rows = 2 * nh * nt
    assert 2 * nt == SUBLANES and n_pages % pp == 0
    kern = functools.partial(_attn_sample_kernel, nh=nh, nt=nt, pp=pp, past=past, page=page, lam_init=lam_init)

    def page_of(i):
        return lambda b, p, pt: pt[b, jnp.minimum(p, n_pages // pp - 1) * pp + i]

    k_specs = [pl.BlockSpec((None, w, page), lambda b, p, pt, f=page_of(i): (f(b, p, pt), 0, 0))
               for i in range(pp)]
    v_specs = [pl.BlockSpec((None, page * nh, dv), lambda b, p, pt, f=page_of(i): (f(b, p, pt), 0, 0))
               for i in range(pp)]
    seq_spec = lambda r: pl.BlockSpec((None, r, w), lambda b, p, pt: (b, 0, 0))
    const = lambda shape: pl.BlockSpec(shape, lambda b, p, pt: (0,) * len(shape))
    grid_spec = pltpu.PrefetchScalarGridSpec(
        num_scalar_prefetch=1,
        grid=(nseq, n_pages // pp + 1),
        in_specs=[const(lvec.shape), const(slope_col.shape), const(qpos_col.shape), seq_spec(SUBLANES)]
        + k_specs + v_specs + [seq_spec(nrows), seq_spec(nrows)],
        out_specs=seq_spec(SUBLANES),
        scratch_shapes=[pltpu.VMEM((rows, w), BF16), pltpu.VMEM((page, page * nh), BF16),
                        pltpu.VMEM((rows, page * nh), F32), pltpu.VMEM((rows, 1), F32),
                        pltpu.VMEM((rows, 1), F32), pltpu.VMEM((rows, dv), F32)],
    )
    return pl.pallas_call(
        kern,
        grid_spec=grid_spec,
        out_shape=jax.ShapeDtypeStruct((nseq, SUBLANES, w), F32),
        compiler_params=_params("arbitrary", "arbitrary"),
        name="attn_sample",
    )(page_table, lvec, slope_col, qpos_col, q8, *([k_t] * pp), *([v_rows] * pp), kn, vn)


def _final_kernel(o_ref, z_ref, y_ref, modb_ref, gsub_ref, wout_ref, out_ref, *, dvb, post_scale):
    d = y_ref.shape[-1]
    o = o_ref[...]
    on = jnp.concatenate([_rms_rows(o[:, h * dvb:(h + 1) * dvb]) for h in range(o.shape[-1] // dvb)], axis=-1)
    on = on * gsub_ref[...] * post_scale
    u = (on * _silu(z_ref[...])).astype(BF16)
    out = jnp.dot(u, wout_ref[...], preferred_element_type=F32)
    out_ref[...] = y_ref[...] + modb_ref[:, 2 * d:3 * d] * out


def _final(o, z, y, modb3, tiles_per_group, tm, gsub_t, wout, dvb, post_scale):
    n, d = y.shape
    kern = functools.partial(_final_kernel, dvb=dvb, post_scale=post_scale)
    return pl.pallas_call(
        kern,
        grid=(n // tm,),
        in_specs=[_row_spec(tm, o.shape[1]), _row_spec(tm, z.shape[1]), _row_spec(tm, d),
                  _mod_spec(modb3, tm, tiles_per_group), _const_spec(gsub_t.shape), _const_spec(wout.shape)],
        out_specs=_row_spec(tm, d),
        out_shape=jax.ShapeDtypeStruct((n, d), F32),
        compiler_params=_params("arbitrary"),
        name="final_proj",
    )(o, z, y, modb3, gsub_t, wout)


def _tile(vec, reps):
    return jnp.tile(vec.astype(F32), reps).reshape(1, -1)


def kernel(x_prompt, x_sample, c_prompt, c_sample, state_gla, cache_k, cache_v, page_table, norm_a, ada_w_a, ada_b_a, w_in_a, w_g2_a, b_g_a, onorm_a, w_out_a, norm_kv, ada_w_kv, ada_b_kv, w_k, w_v, g_k, norm_b, ada_w_b, ada_b_b, w_in_b, g_q, lambda_q1, lambda_k1, lambda_q2, lambda_k2, subln_b, w_out_b):
    b, t, d = x_prompt.shape
    nseq, nt, _ = x_sample.shape
    n_a, _, nh_a, dk_a, dv_a = state_gla.shape
    n_b = norm_b.shape[0]
    ka, va = nh_a * dk_a, nh_a * dv_a
    nh_b, dv_b = cache_v.shape[2], cache_v.shape[3]
    dqk_b = cache_k.shape[4]
    n_pool, page = cache_k.shape[0], cache_k.shape[1]
    tm = 256
    assert n_a == 1 and n_b == 1, "one GLA layer and one differential-attention layer"
    assert nt <= SUBLANES and t % tm == 0 and (nseq * nt) % tm == 0

    c_all = jnp.concatenate([c_prompt, c_sample], axis=0)
    xp = x_prompt.reshape(b * t, d)
    xs = x_sample.reshape(nseq * nt, d)
    tiles_p = t // tm

    def split_mod(mod):
        mp = mod[:b].reshape(b, 1, -1)
        ms = jnp.repeat(mod[b:], nt, axis=0).reshape((nseq * nt) // tm, tm, -1)
        return mp, ms

    pad8 = lambda a: jnp.pad(a.reshape(nseq, nt, -1), ((0, 0), (0, SUBLANES - nt), (0, 0)))

    l = 0
    moda_p, moda_s = split_mod(_ada(c_all, ada_w_a[l], ada_b_a[l]))
    rank = w_g2_a.shape[1]
    w_in_pad = jnp.pad(w_in_a[l], ((0, 0), (0, LANES - rank))).astype(BF16)
    wg2_pad = jnp.pad(w_g2_a[l], ((0, LANES - rank), (0, 0))).astype(BF16)
    gla_args = (norm_a[l], w_in_pad, wg2_pad, b_g_a[l], ka, va, dk_a)
    qp, kp_, vp_, zp, lgp = _gla_in(xp, moda_p, tiles_p, tm, *gla_args)
    qs, ks_, vs_, zs, lgs = _gla_in(xs, moda_s, 1, tm, *gla_args)
    op, state_p = _gla_prompt(qp, kp_, vp_, lgp, b, t, nh_a, dk_a, dv_a)
    os8, state_s = _gla_sample(pad8(qs).reshape(-1, ka), pad8(ks_).reshape(-1, ka),
                               pad8(vs_).reshape(-1, va), pad8(lgs).reshape(-1, ka),
                               state_gla[l], nh_a, dk_a, dv_a)
    os_ = os8.reshape(nseq, SUBLANES, va)[:, :nt].reshape(nseq * nt, va)

    j = 0
    lam_init = 0.8 - 0.6 * math.exp(-0.3 * (n_a + j))
    modkv_p, modkv_s = split_mod(_ada(c_all, ada_w_kv, ada_b_kv))
    modb_p, modb_s = split_mod(_ada(c_all, ada_w_b[j], ada_b_b[j]))
    mid_w = (_tile(onorm_a[l], nh_a), w_out_a[l].astype(BF16), norm_kv.reshape(1, d),
             w_k.astype(BF16), w_v.astype(BF16), _tile(g_k, 2 * nh_b), norm_b[j].reshape(1, d),
             w_in_b[j].astype(BF16), _tile(g_q[j], 2 * nh_b), nh_a, dv_a, dqk_b ** -0.5 * LOG2E)
    y1p, ktp, vp, kbp, vtp, qtp, zbp = _mid(
        op, zp, xp, moda_p, modkv_p, modb_p, tiles_p, tm, *mid_w, token_minor=True)
    y1s, ks, vs, kbs, vbs, qbs, zbs = _mid(
        os_, zs, xs, moda_s, modkv_s, modb_s, 1, tm, *mid_w, token_minor=False)

    w = nh_b * dv_b
    lvec = jnp.stack([lambda_q1[j], lambda_k1[j], lambda_q2[j], lambda_k2[j]]).astype(F32)
    slopes = 2.0 ** (-8.0 * jnp.arange(1, nh_b + 1, dtype=F32) / nh_b) * LOG2E
    ap = _attn_prompt(qtp, kbp.reshape(b, t, w), vtp, slopes, lvec, nh_b, lam_init)
    past = page_table.shape[1] * page
    groups = 2 * nh_b
    slope_col = jnp.repeat(slopes, 2 * nt).reshape(nt * groups, 1)
    qpos_col = (past + jnp.tile(jnp.arange(nt, dtype=jnp.int32), groups)).reshape(nt * groups, 1)
    pad_rows = lambda a, r: jnp.pad(a.reshape(nseq, nt, w), ((0, 0), (0, r - nt), (0, 0)))
    q_rep = jnp.tile(qbs.reshape(nseq, nt, w), (1, SUBLANES // nt, 1))
    k_t = cache_k.transpose(0, 2, 3, 4, 1).reshape(n_pool, w, page)
    v_rows = cache_v.reshape(n_pool, page * nh_b, dv_b)
    as8 = _attn_sample(q_rep, k_t, v_rows, page_table,
                       pad_rows(kbs, 2 * SUBLANES), pad_rows(vbs, 2 * SUBLANES),
                       lvec, slope_col, qpos_col, nh_b, nt, lam_init)
    as_ = as8[:, :nt].reshape(nseq * nt, w)

    fin_w = (_tile(subln_b[j], nh_b), w_out_b[j].astype(BF16), dv_b, 1.0 - lam_init)
    yp = _final(ap.reshape(b * t, w), zbp, y1p, modb_p, tiles_p, tm, *fin_w)
    ys = _final(as_, zbs, y1s, modb_s, 1, tm, *fin_w)

    kp = ktp.reshape(b, nh_b, 2, dqk_b, t).transpose(0, 4, 1, 2, 3)
    return (yp.reshape(b, t, d), ys.reshape(nseq, nt, d), state_p[None], state_s[None],
            kp, vp.reshape(b, t, nh_b, dv_b),
            ks.reshape(nseq, nt, nh_b, 2, dqk_b), vs.reshape(nseq, nt, nh_b, dv_b))
```

```python
import functools
import math

import jax
import jax.numpy as jnp
from jax import lax
from jax.experimental import pallas as pl
from jax.experimental.pallas import tpu as pltpu

F32 = jnp.float32
BF16 = jnp.bfloat16
EPS = 1e-6
GATE_NORMALIZER = 16.0
LOG2E = math.log2(math.e)
LANES = 128
SUBLANES = 8
VMEM_LIMIT = 56 * 1024 * 1024

NT_DIMS = (((1,), (1,)), ((), ()))
TN_DIMS = (((0,), (0,)), ((), ()))


def _params(*sem):
    return pltpu.CompilerParams(dimension_semantics=sem, vmem_limit_bytes=VMEM_LIMIT)


def _silu(x):
    return x / (1.0 + jnp.exp(-x))


def _rms_rows(x):
    return x * lax.rsqrt(jnp.mean(x * x, axis=-1, keepdims=True) + EPS)


def _const_spec(shape):
    zeros = (0,) * len(shape)
    return pl.BlockSpec(shape, lambda *_: zeros, pipeline_mode=pl.Buffered(1))


def _ada_kernel(c_ref, w_ref, b_ref, o_ref):
    s = _silu(c_ref[...]).astype(BF16)
    o_ref[...] = jnp.dot(s, w_ref[...].astype(BF16), preferred_element_type=F32) + b_ref[...]


def _ada(c, w, b, tn=512):
    n, d = c.shape
    m = w.shape[1]
    return pl.pallas_call(
        _ada_kernel,
        grid=(m // tn,),
        in_specs=[pl.BlockSpec((n, d), lambda j: (0, 0)),
                  pl.BlockSpec((d, tn), lambda j: (0, j)),
                  pl.BlockSpec((1, tn), lambda j: (0, j))],
        out_specs=pl.BlockSpec((n, tn), lambda j: (0, j)),
        out_shape=jax.ShapeDtypeStruct((n, m), F32),
        compiler_params=_params("arbitrary"),
        name="ada_mod",
    )(c, w, b.reshape(1, m))


def _mod_spec(mod3, tm, tiles_per_group):
    r = mod3.shape[1]
    w = mod3.shape[2]
    if r == 1:
        return pl.BlockSpec((None, 1, w), lambda i: (i // tiles_per_group, 0, 0))
    return pl.BlockSpec((None, tm, w), lambda i: (i, 0, 0))


def _row_spec(tm, w):
    return pl.BlockSpec((tm, w), lambda i: (i, 0))


def _gla_in_kernel(x_ref, mod_ref, g_ref, w_ref, wg2_ref, bg_ref,
                   q_ref, k_ref, v_ref, z_ref, lg_ref, *, ka, va, dk):
    x = x_ref[...]
    d = x.shape[-1]
    mod = mod_ref[...]
    h = _rms_rows(x) * g_ref[...] * (1.0 + mod[:, d:2 * d]) + mod[:, 0:d]
    p = jnp.dot(h.astype(BF16), w_ref[...], preferred_element_type=F32)
    q_ref[...] = p[:, 0:ka] * (dk ** -0.5)
    k_ref[...] = p[:, ka:2 * ka]
    v_ref[...] = p[:, 2 * ka:2 * ka + va]
    z_ref[...] = p[:, 2 * ka + va:2 * ka + 2 * va]
    glow = p[:, 2 * ka + 2 * va:]
    g2 = jnp.dot(glow.astype(BF16), wg2_ref[...], preferred_element_type=F32) + bg_ref[...]
    logsig = jnp.minimum(g2, 0.0) - jnp.log1p(jnp.exp(-jnp.abs(g2)))
    lg_ref[...] = logsig * (1.0 / GATE_NORMALIZER)


def _gla_in(x2, mod3, tiles_per_group, tm, g, w_pad, wg2_pad, bg, ka, va, dk):
    n, d = x2.shape
    wcols = w_pad.shape[1]
    kern = functools.partial(_gla_in_kernel, ka=ka, va=va, dk=dk)
    outs = [jax.ShapeDtypeStruct((n, ka), F32), jax.ShapeDtypeStruct((n, ka), F32),
            jax.ShapeDtypeStruct((n, va), F32), jax.ShapeDtypeStruct((n, va), F32),
            jax.ShapeDtypeStruct((n, ka), F32)]
    return pl.pallas_call(
        kern,
        grid=(n // tm,),
        in_specs=[_row_spec(tm, d), _mod_spec(mod3, tm, tiles_per_group),
                  _const_spec((1, d)), _const_spec((d, wcols)),
                  _const_spec(wg2_pad.shape), _const_spec((1, ka))],
        out_specs=[_row_spec(tm, ka), _row_spec(tm, ka), _row_spec(tm, va),
                   _row_spec(tm, va), _row_spec(tm, ka)],
        out_shape=outs,
        compiler_params=_params("arbitrary"),
        name="gla_in",
    )(x2, mod3, g.reshape(1, d), w_pad, wg2_pad, bg.reshape(1, ka))


def _cumsum_rows(tril_b, x):
    hi = x.astype(BF16)
    rest = x - hi.astype(F32)
    mid = rest.astype(BF16)
    lo = (rest - mid.astype(F32)).astype(BF16)
    dot = lambda part: jnp.dot(tril_b, part, preferred_element_type=F32)
    return dot(hi) + dot(mid) + dot(lo)


def _level_masks(row, col, max_block):
    levels = []
    n = 2
    while n <= max_block:
        sh = (n // 2).bit_length() - 1
        rb = row >> sh
        cb = col >> sh
        levels.append((n, jnp.where((cb & 1) == 0, rb - cb, 0) == 1))
        n *= 2
    return levels


def _midpoint_rows(cum, n, sub):
    c, dk = cum.shape
    if n == 2:
        return jnp.where((sub & 1) == 1, pltpu.roll(cum, 1, 0), cum)
    if n == 4:
        r = sub & 3
        return jnp.where(r == 0, pltpu.roll(cum, c - 1, 0),
                         jnp.where(r == 1, cum,
                                   jnp.where(r == 2, pltpu.roll(cum, 1, 0), pltpu.roll(cum, 2, 0))))
    half = n // 2
    return jnp.concatenate(
        [jnp.broadcast_to(cum[i * n + half - 1:i * n + half, :], (n, dk)) for i in range(c // n)], axis=0)


def _intra_operands(q, k, cum, levels, sub):
    ops = []
    for n, _ in levels:
        ref = _midpoint_rows(cum, n, sub)
        ops.append(((q * jnp.exp(cum - ref)).astype(BF16), (k * jnp.exp(ref - cum)).astype(BF16)))
    return ops, jnp.sum(q * k, axis=-1, keepdims=True)


def _intra_scores(ops, own, levels, eye):
    a = jnp.where(eye, own, 0.0)
    for (_, mask), (qs, ks) in zip(levels, ops):
        a = jnp.where(mask, lax.dot_general(qs, ks, NT_DIMS, preferred_element_type=F32), a)
    return a


def _gla_prompt_kernel(q_ref, k_ref, v_ref, lg_ref, o_ref, s_ref, st_scr, *, c, nc, nh, dk, dv):
    ci = pl.program_id(1)

    @pl.when(ci == 0)
    def _():
        st_scr[...] = jnp.zeros_like(st_scr)

    row = lax.broadcasted_iota(jnp.int32, (c, c), 0)
    col = lax.broadcasted_iota(jnp.int32, (c, c), 1)
    sub = lax.broadcasted_iota(jnp.int32, (c, dk), 0)
    tril_b = jnp.where(col <= row, 1.0, 0.0).astype(BF16)
    levels = _level_masks(row, col, c)
    pairs = [(j, h) for j in range(nc) for h in range(nh)]
    rows = [slice(j * c, (j + 1) * c) for j in range(nc)]
    sk = [slice(h * dk, (h + 1) * dk) for h in range(nh)]
    sv = [slice(h * dv, (h + 1) * dv) for h in range(nh)]
    cums = [_cumsum_rows(tril_b, lg_ref[rows[j], :]) for j in range(nc)]
    prep = {}
    for j, h in pairs:
        qh, kh, cum = q_ref[rows[j], sk[h]], k_ref[rows[j], sk[h]], cums[j][:, sk[h]]
        last = cum[c - 1:c, :]
        ops, own = _intra_operands(qh, kh, cum, levels, sub)
        prep[j, h] = (ops, own, (qh * jnp.exp(cum)).astype(BF16),
                      (kh * jnp.exp(last - cum)).astype(BF16), jnp.exp(last))
    local = {}
    for j, h in pairs:
        ops, own, _, kd, _ = prep[j, h]
        vb = v_ref[rows[j], sv[h]].astype(BF16)
        a = _intra_scores(ops, own, levels, row == col)
        local[j, h] = (jnp.dot(a.astype(BF16), vb, preferred_element_type=F32),
                       lax.dot_general(vb, kd, TN_DIMS, preferred_element_type=F32))
    for h in range(nh):
        st = st_scr[h]
        for j in range(nc):
            o_intra, update = local[j, h]
            _, _, qe, _, decay = prep[j, h]
            o_ref[rows[j], sv[h]] = o_intra + lax.dot_general(
                qe, st.astype(BF16), NT_DIMS, preferred_element_type=F32)
            st = decay * st + update
        st_scr[h] = st

    @pl.when(ci == pl.num_programs(1) - 1)
    def _():
        for h in range(nh):
            s_ref[h] = st_scr[h].T


def _gla_prompt(q, k, v, lg, b, t, nh, dk, dv, c=64, chunks_per_step=4):
    n = b * t
    rows = c * chunks_per_step
    assert t % rows == 0
    steps = t // rows
    kern = functools.partial(_gla_prompt_kernel, c=c, nc=chunks_per_step, nh=nh, dk=dk, dv=dv)
    row = lambda w: pl.BlockSpec((rows, w), lambda bi, ci: (bi * steps + ci, 0))
    return pl.pallas_call(
        kern,
        grid=(b, steps),
        in_specs=[row(nh * dk), row(nh * dk), row(nh * dv), row(nh * dk)],
        out_specs=[row(nh * dv),
                   pl.BlockSpec((None, nh, dk, dv), lambda bi, ci: (bi, 0, 0, 0))],
        out_shape=[jax.ShapeDtypeStruct((n, nh * dv), F32),
                   jax.ShapeDtypeStruct((b, nh, dk, dv), F32)],
        scratch_shapes=[pltpu.VMEM((nh, dv, dk), F32)],
        compiler_params=_params("arbitrary", "arbitrary"),
        name="gla_prompt",
    )(q, k, v, lg)


def _gla_sample_kernel(q_ref, k_ref, v_ref, lg_ref, s0_ref, o_ref, s_ref, *, g, nh, dk, dv):
    r = g * SUBLANES
    row = lax.broadcasted_iota(jnp.int32, (r, r), 0)
    col = lax.broadcasted_iota(jnp.int32, (r, r), 1)
    sub = lax.broadcasted_iota(jnp.int32, (r, dk), 0)
    same_seq = (row >> 3) == (col >> 3)
    cum_all = _cumsum_rows(jnp.where(same_seq & (col <= row), 1.0, 0.0).astype(BF16), lg_ref[...])
    levels = _level_masks(row, col, SUBLANES)
    for h in range(nh):
        sk = slice(h * dk, (h + 1) * dk)
        sv = slice(h * dv, (h + 1) * dv)
        qh = q_ref[:, sk]
        kh = k_ref[:, sk]
        vh = v_ref[:, sv]
        cum = cum_all[:, sk]
        ops, own = _intra_operands(qh, kh, cum, levels, sub)
        a = _intra_scores(ops, own, levels, row == col)
        o_intra = jnp.dot(a.astype(BF16), vh.astype(BF16), preferred_element_type=F32)
        qe = qh * jnp.exp(cum)
        first_row = lax.broadcasted_iota(jnp.int32, (SUBLANES, dk), 0) == 0
        ones = jnp.ones((SUBLANES, LANES), F32)
        for i in range(g):
            rs = slice(i * SUBLANES, (i + 1) * SUBLANES)
            st = s0_ref[i, h]
            o_ref[rs, sv] = o_intra[rs] + jnp.dot(
                qe[rs].astype(BF16), st.astype(BF16), preferred_element_type=F32)
            last = cum[(i + 1) * SUBLANES - 1:(i + 1) * SUBLANES, :]
            kd = (kh[rs] * jnp.exp(last - cum[rs])).astype(BF16)
            decay_row = jnp.where(first_row, jnp.broadcast_to(jnp.exp(last), (SUBLANES, dk)), 0.0)
            decay = lax.dot_general(decay_row, ones, TN_DIMS, precision=lax.Precision.HIGHEST,
                                    preferred_element_type=F32)
            s_ref[i, h] = jnp.concatenate([decay] * (dv // LANES), axis=1) * st + lax.dot_general(
                kd, vh[rs].astype(BF16), TN_DIMS, preferred_element_type=F32)


def _gla_sample(q, k, v, lg, s0, nh, dk, dv, g=8):
    nseq = s0.shape[0]
    r = g * SUBLANES
    kern = functools.partial(_gla_sample_kernel, g=g, nh=nh, dk=dk, dv=dv)
    st_spec = pl.BlockSpec((g, nh, dk, dv), lambda i: (i, 0, 0, 0))
    return pl.pallas_call(
        kern,
        grid=(nseq // g,),
        in_specs=[_row_spec(r, nh * dk), _row_spec(r, nh * dk), _row_spec(r, nh * dv),
                  _row_spec(r, nh * dk), st_spec],
        out_specs=[_row_spec(r, nh * dv), st_spec],
        out_shape=[jax.ShapeDtypeStruct((nseq * SUBLANES, nh * dv), F32),
                   jax.ShapeDtypeStruct(s0.shape, F32)],
        compiler_params=_params("arbitrary"),
        name="gla_sample",
    )(q, k, v, lg, s0)


def _rms_groups64(x, lane_lo):
    outs = []
    for j in range(x.shape[-1] // LANES):
        xj = x[:, j * LANES:(j + 1) * LANES]
        sq = xj * xj
        s0 = jnp.sum(jnp.where(lane_lo, sq, 0.0), axis=-1, keepdims=True)
        s1 = jnp.sum(jnp.where(lane_lo, 0.0, sq), axis=-1, keepdims=True)
        ms = jnp.where(lane_lo, s0, s1) * (2.0 / LANES)
        outs.append(xj * lax.rsqrt(ms + EPS))
    return jnp.concatenate(outs, axis=-1)


def _mid_kernel(o_ref, z_ref, x_ref, moda_ref, modkv_ref, modb_ref,
                onorm_ref, wout_ref, gkv_ref, wk_ref, wv_ref, gk_ref, gb_ref, winb_ref, gq_ref,
                y_ref, k_ref, v_ref, kb_ref, vb_ref, qb_ref, zb_ref, *, nh, dv, qk_scale, token_minor):
    d = x_ref.shape[-1]
    o = o_ref[...]
    on = jnp.concatenate([_rms_rows(o[:, h * dv:(h + 1) * dv]) for h in range(nh)], axis=-1)
    on = on * onorm_ref[...]
    u = (on * _silu(z_ref[...])).astype(BF16)
    out = jnp.dot(u, wout_ref[...], preferred_element_type=F32)
    y = x_ref[...] + moda_ref[:, 2 * d:3 * d] * out
    y_ref[...] = y
    yn = _rms_rows(y)
    lane_lo = lax.broadcasted_iota(jnp.int32, (1, LANES), 1) < (LANES // 2)

    modkv = modkv_ref[...]
    h2 = (yn * gkv_ref[...] * (1.0 + modkv[:, d:2 * d]) + modkv[:, 0:d]).astype(BF16)
    kk = _rms_groups64(jnp.dot(h2, wk_ref[...], preferred_element_type=F32), lane_lo) * gk_ref[...]
    k_ref[...] = kk.T if token_minor else kk
    kb_ref[...] = kk.astype(BF16)
    vv = jnp.dot(h2, wv_ref[...], preferred_element_type=F32)
    v_ref[...] = vv
    vb_ref[...] = (vv.T if token_minor else vv).astype(BF16)

    modb = modb_ref[...]
    h3 = (yn * gb_ref[...] * (1.0 + modb[:, d:2 * d]) + modb[:, 0:d]).astype(BF16)
    qz = jnp.dot(h3, winb_ref[...], preferred_element_type=F32)
    qb = qz.shape[-1] - zb_ref.shape[-1]
    qq = _rms_groups64(qz[:, 0:qb], lane_lo) * (gq_ref[...] * qk_scale)
    qb_ref[...] = (qq.T if token_minor else qq).astype(BF16)
    zb_ref[...] = qz[:, qb:]


def _mid(o, z, x2, moda3, modkv3, modb3, tiles_per_group, tm,
         onorm_t, wout, gkv, wk, wv, gk_t, gb, winb, gq_t, nh, dv, qk_scale, token_minor):
    n, d = x2.shape
    qb = wk.shape[1]
    zb = winb.shape[1] - qb
    kern = functools.partial(_mid_kernel, nh=nh, dv=dv, qk_scale=qk_scale, token_minor=token_minor)
    ms = lambda m: _mod_spec(m, tm, tiles_per_group)
    if token_minor:
        groups = n // (tm * tiles_per_group)
        t = tm * tiles_per_group
        feat = lambda w, dt: jax.ShapeDtypeStruct((groups, w, t), dt)
        feat_spec = lambda w: pl.BlockSpec(
            (None, w, tm), lambda i: (i // tiles_per_group, 0, i % tiles_per_group))
    else:
        feat = lambda w, dt: jax.ShapeDtypeStruct((n, w), dt)
        feat_spec = lambda w: _row_spec(tm, w)
    outs = [jax.ShapeDtypeStruct((n, d), F32), feat(qb, F32),
            jax.ShapeDtypeStruct((n, zb), F32), jax.ShapeDtypeStruct((n, qb), BF16),
            feat(zb, BF16), feat(qb, BF16),
            jax.ShapeDtypeStruct((n, zb), F32)]
    out_specs = [_row_spec(tm, d), feat_spec(qb), _row_spec(tm, zb), _row_spec(tm, qb),
                 feat_spec(zb), feat_spec(qb), _row_spec(tm, zb)]
    return pl.pallas_call(
        kern,
        grid=(n // tm,),
        in_specs=[_row_spec(tm, o.shape[1]), _row_spec(tm, z.shape[1]), _row_spec(tm, d),
                  ms(moda3), ms(modkv3), ms(modb3),
                  _const_spec(onorm_t.shape), _const_spec(wout.shape), _const_spec(gkv.shape),
                  _const_spec(wk.shape), _const_spec(wv.shape), _const_spec(gk_t.shape),
                  _const_spec(gb.shape), _const_spec(winb.shape), _const_spec(gq_t.shape)],
        out_specs=out_specs,
        out_shape=outs,
        compiler_params=_params("arbitrary"),
        name="mid_proj",
    )(o, z, x2, moda3, modkv3, modb3, onorm_t, wout, gkv, wk, wv, gk_t, gb, winb, gq_t)


def _lambda_full(l_ref, lam_init):
    lv = l_ref[...]
    s1 = jnp.sum(lv[0:1] * lv[1:2], axis=-1, keepdims=True)
    s2 = jnp.sum(lv[2:3] * lv[3:4], axis=-1, keepdims=True)
    return jnp.exp(s1) - jnp.exp(s2) + lam_init


MAX_STATIC_SHIFT = 60.0


def _attn_prompt_kernel(slope_ref, bound_ref, l_ref, qt_ref, k_ref, vt_ref, o_ref, *, tq, tk, hps, lam_init):
    hg = pl.program_id(1)
    qi = pl.program_id(2)
    half = LANES // 2
    heads = range(hps)
    cols = [slice(hh * LANES, (hh + 1) * LANES) for hh in heads]
    zeros = jnp.zeros((half, tq), BF16)
    krow = lax.broadcasted_iota(jnp.int32, (tk, LANES), 0).astype(F32)
    ones_rows = jnp.ones((2 * SUBLANES, tk), BF16)
    slopes, qqs = [], []
    for hh in heads:
        slopes.append(slope_ref[hg * hps + hh])
        qt = qt_ref[cols[hh], :]
        qqs.append(jnp.concatenate([jnp.concatenate([qt[0:half], zeros], axis=0),
                                    jnp.concatenate([zeros, qt[half:LANES]], axis=0)], axis=1))

    def scores(j, causal):
        start = pl.multiple_of(j * tk, tk)
        offset = (j * tk - qi * tq).astype(F32)
        out = []
        for hh in heads:
            key_bias = slopes[hh] * (krow + offset)
            s = jnp.dot(k_ref[pl.ds(start, tk), cols[hh]], qqs[hh], preferred_element_type=F32)
            out.append(s + jnp.concatenate([key_bias] * (2 * tq // LANES), axis=1))
        if causal:
            kk = lax.broadcasted_iota(jnp.int32, (tk, 2 * tq), 0)
            qpos = lax.broadcasted_iota(jnp.int32, (tk, 2 * tq), 1)
            keep = kk <= jnp.where(qpos >= tq, qpos - tq, qpos)
            out = [jnp.where(keep, s, -jnp.inf) for s in out]
        return start, out

    def weighted_values(start, probs):
        return [jnp.dot(jnp.concatenate([vt_ref[cols[hh], pl.ds(start, tk)], ones_rows], axis=0), probs[hh],
                        preferred_element_type=F32) for hh in heads]

    def tile_fixed(j, accs, causal, shifts):
        start, ss = scores(j, causal)
        pvs = weighted_values(start, [jnp.exp2(ss[hh] - shifts[hh]).astype(BF16) for hh in heads])
        return tuple(accs[hh] + pvs[hh] for hh in heads)

    def tile_online(j, carries, causal):
        start, ss = scores(j, causal)
        stats, probs = [], []
        for hh in heads:
            m = carries[hh][0]
            m_new = jnp.maximum(m, jnp.max(ss[hh], axis=0, keepdims=True))
            stats.append((m_new, jnp.exp2(m - m_new)))
            probs.append(jnp.exp2(ss[hh] - m_new).astype(BF16))
        pvs = weighted_values(start, probs)
        return tuple((stats[hh][0], stats[hh][1] * carries[hh][1] + pvs[hh]) for hh in heads)

    def write(accs):
        lam = _lambda_full(l_ref, lam_init)
        for hh in heads:
            r = accs[hh][0:LANES] * (1.0 / accs[hh][LANES:LANES + 1])
            o_ref[:, cols[hh]] = (r[:, 0:tq] - lam * r[:, tq:2 * tq]).T

    bound = bound_ref[0]
    zero_acc = tuple(jnp.zeros((LANES + 2 * SUBLANES, 2 * tq), F32) for _ in heads)

    @pl.when(bound <= MAX_STATIC_SHIFT)
    def _():
        qidx = lax.broadcasted_iota(jnp.int32, (1, 2 * tq), 1)
        qidx = jnp.where(qidx >= tq, qidx - tq, qidx).astype(F32)
        shifts = [bound + slopes[hh] * qidx for hh in heads]
        accs = lax.fori_loop(0, qi, lambda j, a: tile_fixed(j, a, False, shifts), zero_acc)
        write(tile_fixed(qi, accs, True, shifts))

    @pl.when(bound > MAX_STATIC_SHIFT)
    def _():
        init = tuple((jnp.full((1, 2 * tq), -jnp.inf, F32), zero_acc[hh]) for hh in heads)
        carries = lax.fori_loop(0, qi, lambda j, c: tile_online(j, c, False), init)
        write([c[1] for c in tile_online(qi, carries, True)])


def _attn_prompt(qt, kb, vt, slopes, bound, lvec, nh, lam_init, tq=512, tk=512, hps=2):
    b, t, w = kb.shape
    assert tq == tk, "the causal tile is peeled as the last key tile"
    assert nh % hps == 0
    gw = hps * LANES
    kern = functools.partial(_attn_prompt_kernel, tq=tq, tk=tk, hps=hps, lam_init=lam_init)
    return pl.pallas_call(
        kern,
        grid=(b, nh // hps, t // tq),
        in_specs=[pl.BlockSpec(memory_space=pltpu.SMEM), pl.BlockSpec(memory_space=pltpu.SMEM),
                  _const_spec(lvec.shape),
                  pl.BlockSpec((None, gw, tq), lambda bi, hi, qi: (bi, hi, qi)),
                  pl.BlockSpec((None, t, gw), lambda bi, hi, qi: (bi, 0, hi)),
                  pl.BlockSpec((None, gw, t), lambda bi, hi, qi: (bi, hi, 0))],
        out_specs=pl.BlockSpec((None, tq, gw), lambda bi, hi, qi: (bi, qi, hi)),
        out_shape=jax.ShapeDtypeStruct((b, t, w), F32),
        compiler_params=_params("arbitrary", "arbitrary", "arbitrary"),
        name="attn_prompt",
    )(slopes, bound, lvec, qt, kb, vt)


def _attn_sample_kernel(pt_ref, l_ref, slope_ref, qpos_ref, q_ref, *refs,
                        nh, nt, n_pages, page, lam_init):
    del pt_ref
    k_refs, v_refs = refs[0:n_pages], refs[n_pages:2 * n_pages]
    kn_ref, vn_ref, o_ref, spread_scr, own_scr = refs[2 * n_pages:]
    past = n_pages * page
    rows, w = own_scr.shape[0], q_ref.shape[-1]
    hr = 2 * nt
    dv = w // nh

    @pl.when(pl.program_id(0) == 0)
    def _():
        tok = lax.broadcasted_iota(jnp.int32, (page, page * nh), 0)
        slot = lax.broadcasted_iota(jnp.int32, (page, page * nh), 1)
        spread_scr[...] = jnp.where(slot // nh == tok, 1.0, 0.0).astype(BF16)
        slot_head = lax.broadcasted_iota(jnp.int32, (rows, page * nh), 1) % nh
        row_head = lax.broadcasted_iota(jnp.int32, (rows, page * nh), 0) // hr
        own_scr[...] = jnp.where(slot_head == row_head, 1.0, 0.0)

    q = q_ref[...].astype(F32)
    qt = jnp.concatenate([q] * (rows // SUBLANES), axis=0)
    lane_grp = lax.broadcasted_iota(jnp.int32, (rows, w), 1) // (LANES // 2)
    row_grp = lax.broadcasted_iota(jnp.int32, (rows, w), 0) // nt
    qbd = jnp.where(lane_grp == row_grp, qt, 0.0).astype(BF16)

    s_past = jnp.concatenate(
        [jnp.dot(qbd, k_refs[i][...].astype(BF16), preferred_element_type=F32) for i in range(n_pages)],
        axis=1)
    kpos = lax.broadcasted_iota(jnp.int32, (1, past), 1)
    s_past = s_past - slope_ref[...] * (qpos_ref[...] - kpos).astype(F32)
    idx = lax.broadcasted_iota(jnp.int32, (1, kn_ref.shape[0]), 1)
    dist = qpos_ref[...] - (past + idx)
    s_new = lax.dot_general(qbd, kn_ref[...], NT_DIMS, preferred_element_type=F32)
    s_new = jnp.where((idx < nt) & (dist >= 0), s_new - slope_ref[...] * dist.astype(F32), -jnp.inf)

    m = jnp.maximum(jnp.max(s_past, axis=-1, keepdims=True), jnp.max(s_new, axis=-1, keepdims=True))
    pr_past = jnp.exp2(s_past - m)
    pr_new = jnp.exp2(s_new - m)
    denom = jnp.sum(pr_past, axis=-1, keepdims=True) + jnp.sum(pr_new, axis=-1, keepdims=True)
    prb = pr_past.astype(BF16)
    pv = jnp.zeros((rows, dv), F32)
    for i in range(n_pages):
        wide = jnp.dot(prb[:, i * page:(i + 1) * page], spread_scr[...], preferred_element_type=F32)
        wide = (wide * own_scr[...]).astype(BF16)
        pv = pv + jnp.dot(wide, v_refs[i][...].astype(BF16), preferred_element_type=F32)
    pv_new = jnp.dot(pr_new.astype(BF16), vn_ref[...], preferred_element_type=F32)
    own_new = jnp.concatenate([pv_new[h * hr:(h + 1) * hr, h * dv:(h + 1) * dv] for h in range(nh)], axis=0)
    r = (pv + own_new) / denom
    diff = r - _lambda_full(l_ref, lam_init) * pltpu.roll(r, rows - nt, 0)
    for h in range(nh):
        o_ref[:, h * dv:(h + 1) * dv] = diff[h * hr:(h + 1) * hr]


def _attn_sample(q8, k_t, v_rows, page_table, kn, vn, lvec, slope_col, qpos_col, nh, nt, lam_init):
    nseq, _, w = q8.shape
    n_pool, _, page = k_t.shape
    dv = v_rows.shape[-1]
    n_pages = page_table.shape[1]
    nrows = kn.shape[1]
    rows = 2 * nh * nt
    assert 2 * nt == SUBLANES
    kern = functools.partial(_attn_sample_kernel, nh=nh, nt=nt, n_pages=n_pages, page=page, lam_init=lam_init)
    k_specs = [pl.BlockSpec((None, w, page), lambda b, pt, i=i: (pt[b, i], 0, 0)) for i in range(n_pages)]
    v_specs = [pl.BlockSpec((None, page * nh, dv), lambda b, pt, i=i: (pt[b, i], 0, 0)) for i in range(n_pages)]
    seq_spec = lambda r: pl.BlockSpec((None, r, w), lambda b, pt: (b, 0, 0))
    const = lambda shape: pl.BlockSpec(shape, lambda b, pt: (0,) * len(shape))
    grid_spec = pltpu.PrefetchScalarGridSpec(
        num_scalar_prefetch=1,
        grid=(nseq,),
        in_specs=[const(lvec.shape), const(slope_col.shape), const(qpos_col.shape), seq_spec(SUBLANES)]
        + k_specs + v_specs + [seq_spec(nrows), seq_spec(nrows)],
        out_specs=seq_spec(SUBLANES),
        scratch_shapes=[pltpu.VMEM((page, page * nh), BF16), pltpu.VMEM((rows, page * nh), F32)],
    )
    return pl.pallas_call(
        kern,
        grid_spec=grid_spec,
        out_shape=jax.ShapeDtypeStruct((nseq, SUBLANES, w), F32),
        compiler_params=_params("arbitrary"),
        name="attn_sample",
    )(page_table, lvec, slope_col, qpos_col, q8, *([k_t] * n_pages), *([v_rows] * n_pages), kn, vn)


def _final_kernel(o_ref, z_ref, y_ref, modb_ref, gsub_ref, wout_ref, out_ref, *, dvb, post_scale):
    d = y_ref.shape[-1]
    o = o_ref[...]
    on = jnp.concatenate([_rms_rows(o[:, h * dvb:(h + 1) * dvb]) for h in range(o.shape[-1] // dvb)], axis=-1)
    on = on * gsub_ref[...] * post_scale
    u = (on * _silu(z_ref[...])).astype(BF16)
    out = jnp.dot(u, wout_ref[...], preferred_element_type=F32)
    out_ref[...] = y_ref[...] + modb_ref[:, 2 * d:3 * d] * out


def _final(o, z, y, modb3, tiles_per_group, tm, gsub_t, wout, dvb, post_scale):
    n, d = y.shape
    kern = functools.partial(_final_kernel, dvb=dvb, post_scale=post_scale)
    return pl.pallas_call(
        kern,
        grid=(n // tm,),
        in_specs=[_row_spec(tm, o.shape[1]), _row_spec(tm, z.shape[1]), _row_spec(tm, d),
                  _mod_spec(modb3, tm, tiles_per_group), _const_spec(gsub_t.shape), _const_spec(wout.shape)],
        out_specs=_row_spec(tm, d),
        out_shape=jax.ShapeDtypeStruct((n, d), F32),
        compiler_params=_params("arbitrary"),
        name="final_proj",
    )(o, z, y, modb3, gsub_t, wout)


def _tile(vec, reps):
    return jnp.tile(vec.astype(F32), reps).reshape(1, -1)


def kernel(x_prompt, x_sample, c_prompt, c_sample, state_gla, cache_k, cache_v, page_table, norm_a, ada_w_a, ada_b_a, w_in_a, w_g2_a, b_g_a, onorm_a, w_out_a, norm_kv, ada_w_kv, ada_b_kv, w_k, w_v, g_k, norm_b, ada_w_b, ada_b_b, w_in_b, g_q, lambda_q1, lambda_k1, lambda_q2, lambda_k2, subln_b, w_out_b):
    b, t, d = x_prompt.shape
    nseq, nt, _ = x_sample.shape
    n_a, _, nh_a, dk_a, dv_a = state_gla.shape
    n_b = norm_b.shape[0]
    ka, va = nh_a * dk_a, nh_a * dv_a
    nh_b, dv_b = cache_v.shape[2], cache_v.shape[3]
    dqk_b = cache_k.shape[4]
    n_pool, page = cache_k.shape[0], cache_k.shape[1]
    tm = 256
    assert n_a == 1 and n_b == 1, "one GLA layer and one differential-attention layer"
    assert nt <= SUBLANES and t % tm == 0 and (nseq * nt) % tm == 0

    c_all = jnp.concatenate([c_prompt, jnp.repeat(c_sample, nt, axis=0)], axis=0)
    xp = x_prompt.reshape(b * t, d)
    xs = x_sample.reshape(nseq * nt, d)
    tiles_p = t // tm

    def split_mod(mod):
        mp = mod[:b].reshape(b, 1, -1)
        ms = mod[b:].reshape((nseq * nt) // tm, tm, -1)
        return mp, ms

    pad8 = lambda a: jnp.pad(a.reshape(nseq, nt, -1), ((0, 0), (0, SUBLANES - nt), (0, 0)))

    l = 0
    moda_p, moda_s = split_mod(_ada(c_all, ada_w_a[l], ada_b_a[l]))
    rank = w_g2_a.shape[1]
    w_in_pad = jnp.pad(w_in_a[l], ((0, 0), (0, LANES - rank))).astype(BF16)
    wg2_pad = jnp.pad(w_g2_a[l], ((0, LANES - rank), (0, 0))).astype(BF16)
    gla_args = (norm_a[l], w_in_pad, wg2_pad, b_g_a[l], ka, va, dk_a)
    qp, kp_, vp_, zp, lgp = _gla_in(xp, moda_p, tiles_p, tm, *gla_args)
    qs, ks_, vs_, zs, lgs = _gla_in(xs, moda_s, 1, tm, *gla_args)
    op, state_p = _gla_prompt(qp, kp_, vp_, lgp, b, t, nh_a, dk_a, dv_a)
    os8, state_s = _gla_sample(pad8(qs).reshape(-1, ka), pad8(ks_).reshape(-1, ka),
                               pad8(vs_).reshape(-1, va), pad8(lgs).reshape(-1, ka),
                               state_gla[l], nh_a, dk_a, dv_a)
    os_ = os8.reshape(nseq, SUBLANES, va)[:, :nt].reshape(nseq * nt, va)

    j = 0
    lam_init = 0.8 - 0.6 * math.exp(-0.3 * (n_a + j))
    modkv_p, modkv_s = split_mod(_ada(c_all, ada_w_kv, ada_b_kv))
    modb_p, modb_s = split_mod(_ada(c_all, ada_w_b[j], ada_b_b[j]))
    mid_w = (_tile(onorm_a[l], nh_a), w_out_a[l].astype(BF16), norm_kv.reshape(1, d),
             w_k.astype(BF16), w_v.astype(BF16), _tile(g_k, 2 * nh_b), norm_b[j].reshape(1, d),
             w_in_b[j].astype(BF16), _tile(g_q[j], 2 * nh_b), nh_a, dv_a, dqk_b ** -0.5 * LOG2E)
    y1p, ktp, vp, kbp, vtp, qtp, zbp = _mid(
        op, zp, xp, moda_p, modkv_p, modb_p, tiles_p, tm, *mid_w, token_minor=True)
    y1s, ks, vs, kbs, vbs, qbs, zbs = _mid(
        os_, zs, xs, moda_s, modkv_s, modb_s, 1, tm, *mid_w, token_minor=False)

    w = nh_b * dv_b
    lvec = jnp.stack([lambda_q1[j], lambda_k1[j], lambda_q2[j], lambda_k2[j]]).astype(F32)
    slopes = 2.0 ** (-8.0 * jnp.arange(1, nh_b + 1, dtype=F32) / nh_b) * LOG2E
    gain_bound = jnp.max(jnp.abs(g_q[j])) * jnp.max(jnp.abs(g_k))
    score_bound = (1.02 * dqk_b ** 0.5 * LOG2E * gain_bound + 1.0).astype(F32).reshape(1)
    ap = _attn_prompt(qtp, kbp.reshape(b, t, w), vtp, slopes, score_bound, lvec, nh_b, lam_init)
    past = page_table.shape[1] * page
    groups = 2 * nh_b
    slope_col = jnp.repeat(slopes, 2 * nt).reshape(nt * groups, 1)
    qpos_col = (past + jnp.tile(jnp.arange(nt, dtype=jnp.int32), groups)).reshape(nt * groups, 1)
    pad_rows = lambda a, r: jnp.pad(a.reshape(nseq, nt, w), ((0, 0), (0, r - nt), (0, 0)))
    q_rep = jnp.tile(qbs.reshape(nseq, nt, w), (1, SUBLANES // nt, 1))
    k_t = cache_k.transpose(0, 2, 3, 4, 1).reshape(n_pool, w, page)
    v_rows = cache_v.reshape(n_pool, page * nh_b, dv_b)
    as8 = _attn_sample(q_rep, k_t, v_rows, page_table,
                       pad_rows(kbs, 2 * SUBLANES), pad_rows(vbs, 2 * SUBLANES),
                       lvec, slope_col, qpos_col, nh_b, nt, lam_init)
    as_ = as8[:, :nt].reshape(nseq * nt, w)

    fin_w = (_tile(subln_b[j], nh_b), w_out_b[j].astype(BF16), dv_b, 1.0 - lam_init)
    yp = _final(ap.reshape(b * t, w), zbp, y1p, modb_p, tiles_p, tm, *fin_w)
    ys = _final(as_, zbs, y1s, modb_s, 1, tm, *fin_w)

    kp = ktp.reshape(b, nh_b, 2, dqk_b, t).transpose(0, 4, 1, 2, 3)
    return (yp.reshape(b, t, d), ys.reshape(nseq, nt, d), state_p[None], state_s[None],
            kp, vp.reshape(b, t, nh_b, dv_b),
            ks.reshape(nseq, nt, nh_b, 2, dqk_b), vs.reshape(nseq, nt, nh_b, dv_b))
```

```python
import functools
import math

import jax
import jax.numpy as jnp
from jax import lax
from jax.experimental import pallas as pl
from jax.experimental.pallas import tpu as pltpu

F32 = jnp.float32
BF16 = jnp.bfloat16
EPS = 1e-6
GATE_NORMALIZER = 16.0
LOG2E = math.log2(math.e)
LANES = 128
SUBLANES = 8
VMEM_LIMIT = 56 * 1024 * 1024

NT_DIMS = (((1,), (1,)), ((), ()))
TN_DIMS = (((0,), (0,)), ((), ()))


def _params(*sem):
    return pltpu.CompilerParams(dimension_semantics=sem, vmem_limit_bytes=VMEM_LIMIT)


def _silu(x):
    return x / (1.0 + jnp.exp(-x))


def _rms_rows(x):
    return x * lax.rsqrt(jnp.mean(x * x, axis=-1, keepdims=True) + EPS)


def _const_spec(shape):
    zeros = (0,) * len(shape)
    return pl.BlockSpec(shape, lambda *_: zeros, pipeline_mode=pl.Buffered(1))


def _ada_kernel(c_ref, w_ref, b_ref, o_ref):
    s = _silu(c_ref[...]).astype(BF16)
    o_ref[...] = jnp.dot(s, w_ref[...].astype(BF16), preferred_element_type=F32) + b_ref[...]


def _ada(c, w, b, tn=512):
    n, d = c.shape
    m = w.shape[1]
    return pl.pallas_call(
        _ada_kernel,
        grid=(m // tn,),
        in_specs=[pl.BlockSpec((n, d), lambda j: (0, 0)),
                  pl.BlockSpec((d, tn), lambda j: (0, j)),
                  pl.BlockSpec((1, tn), lambda j: (0, j))],
        out_specs=pl.BlockSpec((n, tn), lambda j: (0, j)),
        out_shape=jax.ShapeDtypeStruct((n, m), F32),
        compiler_params=_params("arbitrary"),
        name="ada_mod",
    )(c, w, b.reshape(1, m))


def _mod_spec(mod3, tm, tiles_per_group):
    r = mod3.shape[1]
    w = mod3.shape[2]
    if r == 1:
        return pl.BlockSpec((None, 1, w), lambda i: (i // tiles_per_group, 0, 0))
    return pl.BlockSpec((None, tm, w), lambda i: (i, 0, 0))


def _row_spec(tm, w):
    return pl.BlockSpec((tm, w), lambda i: (i, 0))


def _gla_in_kernel(x_ref, mod_ref, g_ref, w_ref, wg2_ref, bg_ref,
                   q_ref, k_ref, v_ref, z_ref, lg_ref, *, ka, va, dk):
    x = x_ref[...]
    d = x.shape[-1]
    mod = mod_ref[...]
    h = _rms_rows(x) * g_ref[...] * (1.0 + mod[:, d:2 * d]) + mod[:, 0:d]
    p = jnp.dot(h.astype(BF16), w_ref[...], preferred_element_type=F32)
    q_ref[...] = p[:, 0:ka] * (dk ** -0.5)
    k_ref[...] = p[:, ka:2 * ka]
    v_ref[...] = p[:, 2 * ka:2 * ka + va]
    z_ref[...] = p[:, 2 * ka + va:2 * ka + 2 * va]
    glow = p[:, 2 * ka + 2 * va:]
    g2 = jnp.dot(glow.astype(BF16), wg2_ref[...], preferred_element_type=F32) + bg_ref[...]
    logsig = jnp.minimum(g2, 0.0) - jnp.log1p(jnp.exp(-jnp.abs(g2)))
    lg_ref[...] = logsig * (1.0 / GATE_NORMALIZER)


def _gla_in(x2, mod3, tiles_per_group, tm, g, w_pad, wg2_pad, bg, ka, va, dk):
    n, d = x2.shape
    wcols = w_pad.shape[1]
    kern = functools.partial(_gla_in_kernel, ka=ka, va=va, dk=dk)
    outs = [jax.ShapeDtypeStruct((n, ka), F32), jax.ShapeDtypeStruct((n, ka), F32),
            jax.ShapeDtypeStruct((n, va), F32), jax.ShapeDtypeStruct((n, va), F32),
            jax.ShapeDtypeStruct((n, ka), F32)]
    return pl.pallas_call(
        kern,
        grid=(n // tm,),
        in_specs=[_row_spec(tm, d), _mod_spec(mod3, tm, tiles_per_group),
                  _const_spec((1, d)), _const_spec((d, wcols)),
                  _const_spec(wg2_pad.shape), _const_spec((1, ka))],
        out_specs=[_row_spec(tm, ka), _row_spec(tm, ka), _row_spec(tm, va),
                   _row_spec(tm, va), _row_spec(tm, ka)],
        out_shape=outs,
        compiler_params=_params("arbitrary"),
        name="gla_in",
    )(x2, mod3, g.reshape(1, d), w_pad, wg2_pad, bg.reshape(1, ka))


def _cumsum_rows(tril_b, x):
    hi = x.astype(BF16)
    rest = x - hi.astype(F32)
    mid = rest.astype(BF16)
    lo = (rest - mid.astype(F32)).astype(BF16)
    dot = lambda part: jnp.dot(tril_b, part, preferred_element_type=F32)
    return dot(hi) + dot(mid) + dot(lo)


def _level_masks(row, col, max_block):
    levels = []
    n = 2
    while n <= max_block:
        sh = (n // 2).bit_length() - 1
        rb = row >> sh
        cb = col >> sh
        levels.append((n, jnp.where((cb & 1) == 0, rb - cb, 0) == 1))
        n *= 2
    return levels


def _midpoint_rows(cum, n, sub):
    c, dk = cum.shape
    if n == 2:
        return jnp.where((sub & 1) == 1, pltpu.roll(cum, 1, 0), cum)
    if n == 4:
        r = sub & 3
        return jnp.where(r == 0, pltpu.roll(cum, c - 1, 0),
                         jnp.where(r == 1, cum,
                                   jnp.where(r == 2, pltpu.roll(cum, 1, 0), pltpu.roll(cum, 2, 0))))
    half = n // 2
    return jnp.concatenate(
        [jnp.broadcast_to(cum[i * n + half - 1:i * n + half, :], (n, dk)) for i in range(c // n)], axis=0)


def _intra_operands(q, k, cum, levels, sub):
    ops = []
    for n, _ in levels:
        ref = _midpoint_rows(cum, n, sub)
        ops.append(((q * jnp.exp(cum - ref)).astype(BF16), (k * jnp.exp(ref - cum)).astype(BF16)))
    return ops, jnp.sum(q * k, axis=-1, keepdims=True)


def _intra_scores(ops, own, levels, eye):
    a = jnp.where(eye, own, 0.0)
    for (_, mask), (qs, ks) in zip(levels, ops):
        a = jnp.where(mask, lax.dot_general(qs, ks, NT_DIMS, preferred_element_type=F32), a)
    return a


def _gla_prompt_kernel(q_ref, k_ref, v_ref, lg_ref, o_ref, s_ref, st_scr, *, c, nc, nh, dk, dv):
    ci = pl.program_id(1)

    @pl.when(ci == 0)
    def _():
        st_scr[...] = jnp.zeros_like(st_scr)

    row = lax.broadcasted_iota(jnp.int32, (c, c), 0)
    col = lax.broadcasted_iota(jnp.int32, (c, c), 1)
    sub = lax.broadcasted_iota(jnp.int32, (c, dk), 0)
    tril_b = jnp.where(col <= row, 1.0, 0.0).astype(BF16)
    levels = _level_masks(row, col, c)
    pairs = [(j, h) for j in range(nc) for h in range(nh)]
    rows = [slice(j * c, (j + 1) * c) for j in range(nc)]
    sk = [slice(h * dk, (h + 1) * dk) for h in range(nh)]
    sv = [slice(h * dv, (h + 1) * dv) for h in range(nh)]
    cums = [_cumsum_rows(tril_b, lg_ref[rows[j], :]) for j in range(nc)]
    prep = {}
    for j, h in pairs:
        qh, kh, cum = q_ref[rows[j], sk[h]], k_ref[rows[j], sk[h]], cums[j][:, sk[h]]
        last = cum[c - 1:c, :]
        ops, own = _intra_operands(qh, kh, cum, levels, sub)
        prep[j, h] = (ops, own, (qh * jnp.exp(cum)).astype(BF16),
                      (kh * jnp.exp(last - cum)).astype(BF16), jnp.exp(last))
    local = {}
    for j, h in pairs:
        ops, own, _, kd, _ = prep[j, h]
        vb = v_ref[rows[j], sv[h]].astype(BF16)
        a = _intra_scores(ops, own, levels, row == col)
        local[j, h] = (jnp.dot(a.astype(BF16), vb, preferred_element_type=F32),
                       lax.dot_general(vb, kd, TN_DIMS, preferred_element_type=F32))
    for h in range(nh):
        st = st_scr[h]
        for j in range(nc):
            o_intra, update = local[j, h]
            _, _, qe, _, decay = prep[j, h]
            o_ref[rows[j], sv[h]] = o_intra + lax.dot_general(
                qe, st.astype(BF16), NT_DIMS, preferred_element_type=F32)
            st = decay * st + update
        st_scr[h] = st

    @pl.when(ci == pl.num_programs(1) - 1)
    def _():
        for h in range(nh):
            s_ref[h] = st_scr[h].T


def _gla_prompt(q, k, v, lg, b, t, nh, dk, dv, c=64, chunks_per_step=4):
    n = b * t
    rows = c * chunks_per_step
    assert t % rows == 0
    steps = t // rows
    kern = functools.partial(_gla_prompt_kernel, c=c, nc=chunks_per_step, nh=nh, dk=dk, dv=dv)
    row = lambda w: pl.BlockSpec((rows, w), lambda bi, ci: (bi * steps + ci, 0))
    return pl.pallas_call(
        kern,
        grid=(b, steps),
        in_specs=[row(nh * dk), row(nh * dk), row(nh * dv), row(nh * dk)],
        out_specs=[row(nh * dv),
                   pl.BlockSpec((None, nh, dk, dv), lambda bi, ci: (bi, 0, 0, 0))],
        out_shape=[jax.ShapeDtypeStruct((n, nh * dv), F32),
                   jax.ShapeDtypeStruct((b, nh, dk, dv), F32)],
        scratch_shapes=[pltpu.VMEM((nh, dv, dk), F32)],
        compiler_params=_params("arbitrary", "arbitrary"),
        name="gla_prompt",
    )(q, k, v, lg)


def _gla_sample_kernel(q_ref, k_ref, v_ref, lg_ref, s0_ref, o_ref, s_ref, *, g, nh, dk, dv):
    r = g * SUBLANES
    row = lax.broadcasted_iota(jnp.int32, (r, r), 0)
    col = lax.broadcasted_iota(jnp.int32, (r, r), 1)
    sub = lax.broadcasted_iota(jnp.int32, (r, dk), 0)
    same_seq = (row >> 3) == (col >> 3)
    cum_all = _cumsum_rows(jnp.where(same_seq & (col <= row), 1.0, 0.0).astype(BF16), lg_ref[...])
    levels = _level_masks(row, col, SUBLANES)
    for h in range(nh):
        sk = slice(h * dk, (h + 1) * dk)
        sv = slice(h * dv, (h + 1) * dv)
        qh = q_ref[:, sk]
        kh = k_ref[:, sk]
        vh = v_ref[:, sv]
        cum = cum_all[:, sk]
        ops, own = _intra_operands(qh, kh, cum, levels, sub)
        a = _intra_scores(ops, own, levels, row == col)
        o_intra = jnp.dot(a.astype(BF16), vh.astype(BF16), preferred_element_type=F32)
        qe = qh * jnp.exp(cum)
        first_row = lax.broadcasted_iota(jnp.int32, (SUBLANES, dk), 0) == 0
        ones = jnp.ones((SUBLANES, LANES), F32)
        for i in range(g):
            rs = slice(i * SUBLANES, (i + 1) * SUBLANES)
            st = s0_ref[i, h]
            o_ref[rs, sv] = o_intra[rs] + jnp.dot(
                qe[rs].astype(BF16), st.astype(BF16), preferred_element_type=F32)
            last = cum[(i + 1) * SUBLANES - 1:(i + 1) * SUBLANES, :]
            kd = (kh[rs] * jnp.exp(last - cum[rs])).astype(BF16)
            decay_row = jnp.where(first_row, jnp.broadcast_to(jnp.exp(last), (SUBLANES, dk)), 0.0)
            decay = lax.dot_general(decay_row, ones, TN_DIMS, precision=lax.Precision.HIGHEST,
                                    preferred_element_type=F32)
            s_ref[i, h] = jnp.concatenate([decay] * (dv // LANES), axis=1) * st + lax.dot_general(
                kd, vh[rs].astype(BF16), TN_DIMS, preferred_element_type=F32)


def _gla_sample(q, k, v, lg, s0, nh, dk, dv, g=8):
    nseq = s0.shape[0]
    r = g * SUBLANES
    kern = functools.partial(_gla_sample_kernel, g=g, nh=nh, dk=dk, dv=dv)
    st_spec = pl.BlockSpec((g, nh, dk, dv), lambda i: (i, 0, 0, 0))
    return pl.pallas_call(
        kern,
        grid=(nseq // g,),
        in_specs=[_row_spec(r, nh * dk), _row_spec(r, nh * dk), _row_spec(r, nh * dv),
                  _row_spec(r, nh * dk), st_spec],
        out_specs=[_row_spec(r, nh * dv), st_spec],
        out_shape=[jax.ShapeDtypeStruct((nseq * SUBLANES, nh * dv), F32),
                   jax.ShapeDtypeStruct(s0.shape, F32)],
        compiler_params=_params("arbitrary"),
        name="gla_sample",
    )(q, k, v, lg, s0)


def _rms_groups64(x, lane_lo):
    outs = []
    for j in range(x.shape[-1] // LANES):
        xj = x[:, j * LANES:(j + 1) * LANES]
        sq = xj * xj
        s0 = jnp.sum(jnp.where(lane_lo, sq, 0.0), axis=-1, keepdims=True)
        s1 = jnp.sum(jnp.where(lane_lo, 0.0, sq), axis=-1, keepdims=True)
        ms = jnp.where(lane_lo, s0, s1) * (2.0 / LANES)
        outs.append(xj * lax.rsqrt(ms + EPS))
    return jnp.concatenate(outs, axis=-1)


def _mid_kernel(o_ref, z_ref, x_ref, moda_ref, modkv_ref, modb_ref,
                onorm_ref, wout_ref, gkv_ref, wk_ref, wv_ref, gk_ref, gb_ref, winb_ref, gq_ref,
                y_ref, k_ref, v_ref, kb_ref, vb_ref, qb_ref, zb_ref, *, nh, dv, qk_scale, token_minor):
    d = x_ref.shape[-1]
    o = o_ref[...]
    on = jnp.concatenate([_rms_rows(o[:, h * dv:(h + 1) * dv]) for h in range(nh)], axis=-1)
    on = on * onorm_ref[...]
    u = (on * _silu(z_ref[...])).astype(BF16)
    out = jnp.dot(u, wout_ref[...], preferred_element_type=F32)
    y = x_ref[...] + moda_ref[:, 2 * d:3 * d] * out
    y_ref[...] = y
    yn = _rms_rows(y)
    lane_lo = lax.broadcasted_iota(jnp.int32, (1, LANES), 1) < (LANES // 2)

    modkv = modkv_ref[...]
    h2 = (yn * gkv_ref[...] * (1.0 + modkv[:, d:2 * d]) + modkv[:, 0:d]).astype(BF16)
    kk = _rms_groups64(jnp.dot(h2, wk_ref[...], preferred_element_type=F32), lane_lo) * gk_ref[...]
    k_ref[...] = kk.T if token_minor else kk
    kb_ref[...] = kk.astype(BF16)
    vv = jnp.dot(h2, wv_ref[...], preferred_element_type=F32)
    v_ref[...] = vv
    vb_ref[...] = (vv.T if token_minor else vv).astype(BF16)

    modb = modb_ref[...]
    h3 = (yn * gb_ref[...] * (1.0 + modb[:, d:2 * d]) + modb[:, 0:d]).astype(BF16)
    qz = jnp.dot(h3, winb_ref[...], preferred_element_type=F32)
    qb = qz.shape[-1] - zb_ref.shape[-1]
    qq = _rms_groups64(qz[:, 0:qb], lane_lo) * (gq_ref[...] * qk_scale)
    qb_ref[...] = (qq.T if token_minor else qq).astype(BF16)
    zb_ref[...] = qz[:, qb:]


def _mid(o, z, x2, moda3, modkv3, modb3, tiles_per_group, tm,
         onorm_t, wout, gkv, wk, wv, gk_t, gb, winb, gq_t, nh, dv, qk_scale, token_minor):
    n, d = x2.shape
    qb = wk.shape[1]
    zb = winb.shape[1] - qb
    kern = functools.partial(_mid_kernel, nh=nh, dv=dv, qk_scale=qk_scale, token_minor=token_minor)
    ms = lambda m: _mod_spec(m, tm, tiles_per_group)
    if token_minor:
        groups = n // (tm * tiles_per_group)
        t = tm * tiles_per_group
        feat = lambda w, dt: jax.ShapeDtypeStruct((groups, w, t), dt)
        feat_spec = lambda w: pl.BlockSpec(
            (None, w, tm), lambda i: (i // tiles_per_group, 0, i % tiles_per_group))
    else:
        feat = lambda w, dt: jax.ShapeDtypeStruct((n, w), dt)
        feat_spec = lambda w: _row_spec(tm, w)
    outs = [jax.ShapeDtypeStruct((n, d), F32), feat(qb, F32),
            jax.ShapeDtypeStruct((n, zb), F32), jax.ShapeDtypeStruct((n, qb), BF16),
            feat(zb, BF16), feat(qb, BF16),
            jax.ShapeDtypeStruct((n, zb), F32)]
    out_specs = [_row_spec(tm, d), feat_spec(qb), _row_spec(tm, zb), _row_spec(tm, qb),
                 feat_spec(zb), feat_spec(qb), _row_spec(tm, zb)]
    return pl.pallas_call(
        kern,
        grid=(n // tm,),
        in_specs=[_row_spec(tm, o.shape[1]), _row_spec(tm, z.shape[1]), _row_spec(tm, d),
                  ms(moda3), ms(modkv3), ms(modb3),
                  _const_spec(onorm_t.shape), _const_spec(wout.shape), _const_spec(gkv.shape),
                  _const_spec(wk.shape), _const_spec(wv.shape), _const_spec(gk_t.shape),
                  _const_spec(gb.shape), _const_spec(winb.shape), _const_spec(gq_t.shape)],
        out_specs=out_specs,
        out_shape=outs,
        compiler_params=_params("arbitrary"),
        name="mid_proj",
    )(o, z, x2, moda3, modkv3, modb3, onorm_t, wout, gkv, wk, wv, gk_t, gb, winb, gq_t)


def _lambda_full(l_ref, lam_init):
    lv = l_ref[...]
    s1 = jnp.sum(lv[0:1] * lv[1:2], axis=-1, keepdims=True)
    s2 = jnp.sum(lv[2:3] * lv[3:4], axis=-1, keepdims=True)
    return jnp.exp(s1) - jnp.exp(s2) + lam_init


MAX_STATIC_SHIFT = 60.0


def _attn_prompt_body(hg, qi, slope_ref, bound_ref, l_ref, qt_ref, k_ref, vt_ref, o_ref, *, tq, tk, hps, lam_init):
    half = LANES // 2
    heads = range(hps)
    cols = [slice(hh * LANES, (hh + 1) * LANES) for hh in heads]
    zeros = jnp.zeros((half, tq), BF16)
    krow = lax.broadcasted_iota(jnp.int32, (tk, LANES), 0).astype(F32)
    ones_rows = jnp.ones((2 * SUBLANES, tk), BF16)
    slopes, qqs = [], []
    for hh in heads:
        slopes.append(slope_ref[hg * hps + hh])
        qt = qt_ref[cols[hh], :]
        qqs.append(jnp.concatenate([jnp.concatenate([qt[0:half], zeros], axis=0),
                                    jnp.concatenate([zeros, qt[half:LANES]], axis=0)], axis=1))

    def scores(j, causal):
        start = pl.multiple_of(j * tk, tk)
        offset = (j * tk - qi * tq).astype(F32)
        out = []
        for hh in heads:
            key_bias = slopes[hh] * (krow + offset)
            s = jnp.dot(k_ref[pl.ds(start, tk), cols[hh]], qqs[hh], preferred_element_type=F32)
            out.append(s + jnp.concatenate([key_bias] * (2 * tq // LANES), axis=1))
        if causal:
            kk = lax.broadcasted_iota(jnp.int32, (tk, 2 * tq), 0)
            qpos = lax.broadcasted_iota(jnp.int32, (tk, 2 * tq), 1)
            keep = kk <= jnp.where(qpos >= tq, qpos - tq, qpos)
            out = [jnp.where(keep, s, -jnp.inf) for s in out]
        return start, out

    def weighted_values(start, probs):
        return [jnp.dot(jnp.concatenate([vt_ref[cols[hh], pl.ds(start, tk)], ones_rows], axis=0), probs[hh],
                        preferred_element_type=F32) for hh in heads]

    def tile_fixed(j, accs, causal, shifts):
        start, ss = scores(j, causal)
        pvs = weighted_values(start, [jnp.exp2(ss[hh] - shifts[hh]).astype(BF16) for hh in heads])
        return tuple(accs[hh] + pvs[hh] for hh in heads)

    def tile_online(j, carries, causal):
        start, ss = scores(j, causal)
        stats, probs = [], []
        for hh in heads:
            m = carries[hh][0]
            m_new = jnp.maximum(m, jnp.max(ss[hh], axis=0, keepdims=True))
            stats.append((m_new, jnp.exp2(m - m_new)))
            probs.append(jnp.exp2(ss[hh] - m_new).astype(BF16))
        pvs = weighted_values(start, probs)
        return tuple((stats[hh][0], stats[hh][1] * carries[hh][1] + pvs[hh]) for hh in heads)

    def write(accs):
        lam = _lambda_full(l_ref, lam_init)
        for hh in heads:
            r = accs[hh][0:LANES] * (1.0 / accs[hh][LANES:LANES + 1])
            o_ref[:, cols[hh]] = (r[:, 0:tq] - lam * r[:, tq:2 * tq]).T

    bound = bound_ref[0]
    zero_acc = tuple(jnp.zeros((LANES + 2 * SUBLANES, 2 * tq), F32) for _ in heads)

    @pl.when(bound <= MAX_STATIC_SHIFT)
    def _():
        qidx = lax.broadcasted_iota(jnp.int32, (1, 2 * tq), 1)
        qidx = jnp.where(qidx >= tq, qidx - tq, qidx).astype(F32)
        shifts = [bound + slopes[hh] * qidx for hh in heads]
        accs = lax.fori_loop(0, qi, lambda j, a: tile_fixed(j, a, False, shifts), zero_acc)
        write(tile_fixed(qi, accs, True, shifts))

    @pl.when(bound > MAX_STATIC_SHIFT)
    def _():
        init = tuple((jnp.full((1, 2 * tq), -jnp.inf, F32), zero_acc[hh]) for hh in heads)
        carries = lax.fori_loop(0, qi, lambda j, c: tile_online(j, c, False), init)
        write([c[1] for c in tile_online(qi, carries, True)])


def _attn_sample_body(first_step, l_ref, slope_ref, qpos_ref, q_ref, k_refs, v_refs, kn_ref, vn_ref, o_ref,
                      spread_scr, own_scr, *, nh, nt, page, lam_init):
    n_pages = len(k_refs)
    past = n_pages * page
    rows, w = own_scr.shape[0], q_ref.shape[-1]
    hr = 2 * nt
    dv = w // nh

    @pl.when(first_step)
    def _():
        tok = lax.broadcasted_iota(jnp.int32, (page, page * nh), 0)
        slot = lax.broadcasted_iota(jnp.int32, (page, page * nh), 1)
        spread_scr[...] = jnp.where(slot // nh == tok, 1.0, 0.0).astype(BF16)
        slot_head = lax.broadcasted_iota(jnp.int32, (rows, page * nh), 1) % nh
        row_head = lax.broadcasted_iota(jnp.int32, (rows, page * nh), 0) // hr
        own_scr[...] = jnp.where(slot_head == row_head, 1.0, 0.0)

    q = q_ref[...].astype(F32)
    qt = jnp.concatenate([q] * (rows // SUBLANES), axis=0)
    lane_grp = lax.broadcasted_iota(jnp.int32, (rows, w), 1) // (LANES // 2)
    row_grp = lax.broadcasted_iota(jnp.int32, (rows, w), 0) // nt
    qbd = jnp.where(lane_grp == row_grp, qt, 0.0).astype(BF16)

    s_past = jnp.concatenate(
        [jnp.dot(qbd, k_refs[i][...].astype(BF16), preferred_element_type=F32) for i in range(n_pages)],
        axis=1)
    kpos = lax.broadcasted_iota(jnp.int32, (1, past), 1)
    s_past = s_past - slope_ref[...] * (qpos_ref[...] - kpos).astype(F32)
    idx = lax.broadcasted_iota(jnp.int32, (1, kn_ref.shape[0]), 1)
    dist = qpos_ref[...] - (past + idx)
    s_new = lax.dot_general(qbd, kn_ref[...], NT_DIMS, preferred_element_type=F32)
    s_new = jnp.where((idx < nt) & (dist >= 0), s_new - slope_ref[...] * dist.astype(F32), -jnp.inf)

    m = jnp.maximum(jnp.max(s_past, axis=-1, keepdims=True), jnp.max(s_new, axis=-1, keepdims=True))
    pr_past = jnp.exp2(s_past - m)
    pr_new = jnp.exp2(s_new - m)
    denom = jnp.sum(pr_past, axis=-1, keepdims=True) + jnp.sum(pr_new, axis=-1, keepdims=True)
    prb = pr_past.astype(BF16)
    pv = jnp.zeros((rows, dv), F32)
    for i in range(n_pages):
        wide = jnp.dot(prb[:, i * page:(i + 1) * page], spread_scr[...], preferred_element_type=F32)
        wide = (wide * own_scr[...]).astype(BF16)
        pv = pv + jnp.dot(wide, v_refs[i][...].astype(BF16), preferred_element_type=F32)
    pv_new = jnp.dot(pr_new.astype(BF16), vn_ref[...], preferred_element_type=F32)
    own_new = jnp.concatenate([pv_new[h * hr:(h + 1) * hr, h * dv:(h + 1) * dv] for h in range(nh)], axis=0)
    r = (pv + own_new) / denom
    diff = r - _lambda_full(l_ref, lam_init) * pltpu.roll(r, rows - nt, 0)
    for h in range(nh):
        o_ref[:, h * dv:(h + 1) * dv] = diff[h * hr:(h + 1) * hr]


def _attn_kernel(pt_ref, slope_ref, bound_ref, l_ref, qt_ref, k_ref, vt_ref,
                 slope_col_ref, qpos_ref, q_ref, *refs,
                 n_pages, prompt_steps, sample_steps, steps_per_batch, q_tiles, prompt_args, sample_args):
    del pt_ref
    k_refs, v_refs = refs[0:n_pages], refs[n_pages:2 * n_pages]
    kn_ref, vn_ref, op_ref, os_ref, spread_scr, own_scr = refs[2 * n_pages:]
    i = pl.program_id(0)

    def prompt_part():
        within = i % steps_per_batch
        _attn_prompt_body(within // q_tiles, within % q_tiles, slope_ref, bound_ref, l_ref,
                          qt_ref, k_ref, vt_ref, op_ref, **prompt_args)

    def sample_part():
        _attn_sample_body(i == 0, l_ref, slope_col_ref, qpos_ref, q_ref, k_refs, v_refs, kn_ref, vn_ref,
                          os_ref, spread_scr, own_scr, **sample_args)

    if prompt_steps < sample_steps:
        pl.when(i < prompt_steps)(prompt_part)
    else:
        prompt_part()
    if sample_steps < prompt_steps:
        pl.when(i < sample_steps)(sample_part)
    else:
        sample_part()


def _attn(qt, kb, vt, slopes, bound, lvec, q8, k_t, v_rows, page_table, kn, vn, slope_col, qpos_col,
          nh, nt, lam_init, tq=512, hps=2):
    b, t, w = kb.shape
    nseq = q8.shape[0]
    n_pool, _, page = k_t.shape
    dv = v_rows.shape[-1]
    n_pages = page_table.shape[1]
    nrows = kn.shape[1]
    rows = 2 * nh * nt
    assert 2 * nt == SUBLANES and nh % hps == 0 and t % tq == 0
    gw = hps * LANES
    q_tiles = t // tq
    steps_per_batch = (nh // hps) * q_tiles
    prompt_steps = b * steps_per_batch
    steps = max(prompt_steps, nseq)

    def prompt_index(i):
        i = jnp.minimum(i, prompt_steps - 1)
        within = i % steps_per_batch
        return i // steps_per_batch, within // q_tiles, within % q_tiles

    def seq_index(i):
        return jnp.minimum(i, nseq - 1)

    def prompt_spec(shape, pick):
        return pl.BlockSpec(shape, lambda i, pt: pick(*prompt_index(i)))

    k_specs = [pl.BlockSpec((None, w, page), lambda i, pt, p=p: (pt[seq_index(i), p], 0, 0))
               for p in range(n_pages)]
    v_specs = [pl.BlockSpec((None, page * nh, dv), lambda i, pt, p=p: (pt[seq_index(i), p], 0, 0))
               for p in range(n_pages)]
    seq_spec = lambda r: pl.BlockSpec((None, r, w), lambda i, pt: (seq_index(i), 0, 0))
    const = lambda shape: pl.BlockSpec(shape, lambda i, pt: (0,) * len(shape))
    smem = pl.BlockSpec(memory_space=pltpu.SMEM)
    kern = functools.partial(
        _attn_kernel, n_pages=n_pages, prompt_steps=prompt_steps, sample_steps=nseq,
        steps_per_batch=steps_per_batch, q_tiles=q_tiles,
        prompt_args=dict(tq=tq, tk=tq, hps=hps, lam_init=lam_init),
        sample_args=dict(nh=nh, nt=nt, page=page, lam_init=lam_init))
    grid_spec = pltpu.PrefetchScalarGridSpec(
        num_scalar_prefetch=1,
        grid=(steps,),
        in_specs=[smem, smem, const(lvec.shape),
                  prompt_spec((None, gw, tq), lambda bi, hi, qi: (bi, hi, qi)),
                  prompt_spec((None, t, gw), lambda bi, hi, qi: (bi, 0, hi)),
                  prompt_spec((None, gw, t), lambda bi, hi, qi: (bi, hi, 0)),
                  const(slope_col.shape), const(qpos_col.shape), seq_spec(SUBLANES)]
        + k_specs + v_specs + [seq_spec(nrows), seq_spec(nrows)],
        out_specs=[prompt_spec((None, tq, gw), lambda bi, hi, qi: (bi, qi, hi)), seq_spec(SUBLANES)],
        scratch_shapes=[pltpu.VMEM((page, page * nh), BF16), pltpu.VMEM((rows, page * nh), F32)],
    )
    return pl.pallas_call(
        kern,
        grid_spec=grid_spec,
        out_shape=[jax.ShapeDtypeStruct((b, t, w), F32), jax.ShapeDtypeStruct((nseq, SUBLANES, w), F32)],
        compiler_params=_params("arbitrary"),
        name="attn",
    )(page_table, slopes, bound, lvec, qt, kb, vt, slope_col, qpos_col, q8,
      *([k_t] * n_pages), *([v_rows] * n_pages), kn, vn)


def _final_kernel(o_ref, z_ref, y_ref, modb_ref, gsub_ref, wout_ref, out_ref, *, dvb, post_scale):
    d = y_ref.shape[-1]
    o = o_ref[...]
    on = jnp.concatenate([_rms_rows(o[:, h * dvb:(h + 1) * dvb]) for h in range(o.shape[-1] // dvb)], axis=-1)
    on = on * gsub_ref[...] * post_scale
    u = (on * _silu(z_ref[...])).astype(BF16)
    out = jnp.dot(u, wout_ref[...], preferred_element_type=F32)
    out_ref[...] = y_ref[...] + modb_ref[:, 2 * d:3 * d] * out


def _final(o, z, y, modb3, tiles_per_group, tm, gsub_t, wout, dvb, post_scale):
    n, d = y.shape
    kern = functools.partial(_final_kernel, dvb=dvb, post_scale=post_scale)
    return pl.pallas_call(
        kern,
        grid=(n // tm,),
        in_specs=[_row_spec(tm, o.shape[1]), _row_spec(tm, z.shape[1]), _row_spec(tm, d),
                  _mod_spec(modb3, tm, tiles_per_group), _const_spec(gsub_t.shape), _const_spec(wout.shape)],
        out_specs=_row_spec(tm, d),
        out_shape=jax.ShapeDtypeStruct((n, d), F32),
        compiler_params=_params("arbitrary"),
        name="final_proj",
    )(o, z, y, modb3, gsub_t, wout)


def _tile(vec, reps):
    return jnp.tile(vec.astype(F32), reps).reshape(1, -1)


def kernel(x_prompt, x_sample, c_prompt, c_sample, state_gla, cache_k, cache_v, page_table, norm_a, ada_w_a, ada_b_a, w_in_a, w_g2_a, b_g_a, onorm_a, w_out_a, norm_kv, ada_w_kv, ada_b_kv, w_k, w_v, g_k, norm_b, ada_w_b, ada_b_b, w_in_b, g_q, lambda_q1, lambda_k1, lambda_q2, lambda_k2, subln_b, w_out_b):
    b, t, d = x_prompt.shape
    nseq, nt, _ = x_sample.shape
    n_a, _, nh_a, dk_a, dv_a = state_gla.shape
    n_b = norm_b.shape[0]
    ka, va = nh_a * dk_a, nh_a * dv_a
    nh_b, dv_b = cache_v.shape[2], cache_v.shape[3]
    dqk_b = cache_k.shape[4]
    n_pool, page = cache_k.shape[0], cache_k.shape[1]
    tm = 256
    tm_wide = 512
    assert n_a == 1 and n_b == 1, "one GLA layer and one differential-attention layer"
    assert nt <= SUBLANES and t % tm_wide == 0 and (nseq * nt) % tm_wide == 0

    c_all = jnp.concatenate([c_prompt, jnp.repeat(c_sample, nt, axis=0)], axis=0)
    xp = x_prompt.reshape(b * t, d)
    xs = x_sample.reshape(nseq * nt, d)
    tiles_p = t // tm

    def split_mod(mod):
        mp = mod[:b].reshape(b, 1, -1)
        ms = mod[b:].reshape((nseq * nt) // tm, tm, -1)
        return mp, ms

    wide = lambda ms: ms.reshape(-1, tm_wide, ms.shape[-1])

    pad8 = lambda a: jnp.pad(a.reshape(nseq, nt, -1), ((0, 0), (0, SUBLANES - nt), (0, 0)))

    l = 0
    moda_p, moda_s = split_mod(_ada(c_all, ada_w_a[l], ada_b_a[l]))
    rank = w_g2_a.shape[1]
    w_in_pad = jnp.pad(w_in_a[l], ((0, 0), (0, LANES - rank))).astype(BF16)
    wg2_pad = jnp.pad(w_g2_a[l], ((0, LANES - rank), (0, 0))).astype(BF16)
    gla_args = (norm_a[l], w_in_pad, wg2_pad, b_g_a[l], ka, va, dk_a)
    qp, kp_, vp_, zp, lgp = _gla_in(xp, moda_p, t // tm_wide, tm_wide, *gla_args)
    qs, ks_, vs_, zs, lgs = _gla_in(xs, wide(moda_s), 1, tm_wide, *gla_args)
    op, state_p = _gla_prompt(qp, kp_, vp_, lgp, b, t, nh_a, dk_a, dv_a)
    os8, state_s = _gla_sample(pad8(qs).reshape(-1, ka), pad8(ks_).reshape(-1, ka),
                               pad8(vs_).reshape(-1, va), pad8(lgs).reshape(-1, ka),
                               state_gla[l], nh_a, dk_a, dv_a)
    os_ = os8.reshape(nseq, SUBLANES, va)[:, :nt].reshape(nseq * nt, va)

    j = 0
    lam_init = 0.8 - 0.6 * math.exp(-0.3 * (n_a + j))
    modkv_p, modkv_s = split_mod(_ada(c_all, ada_w_kv, ada_b_kv))
    modb_p, modb_s = split_mod(_ada(c_all, ada_w_b[j], ada_b_b[j]))
    mid_w = (_tile(onorm_a[l], nh_a), w_out_a[l].astype(BF16), norm_kv.reshape(1, d),
             w_k.astype(BF16), w_v.astype(BF16), _tile(g_k, 2 * nh_b), norm_b[j].reshape(1, d),
             w_in_b[j].astype(BF16), _tile(g_q[j], 2 * nh_b), nh_a, dv_a, dqk_b ** -0.5 * LOG2E)
    y1p, ktp, vp, kbp, vtp, qtp, zbp = _mid(
        op, zp, xp, moda_p, modkv_p, modb_p, tiles_p, tm, *mid_w, token_minor=True)
    y1s, ks, vs, kbs, vbs, qbs, zbs = _mid(
        os_, zs, xs, moda_s, modkv_s, modb_s, 1, tm, *mid_w, token_minor=False)

    w = nh_b * dv_b
    lvec = jnp.stack([lambda_q1[j], lambda_k1[j], lambda_q2[j], lambda_k2[j]]).astype(F32)
    slopes = 2.0 ** (-8.0 * jnp.arange(1, nh_b + 1, dtype=F32) / nh_b) * LOG2E
    gain_bound = jnp.max(jnp.abs(g_q[j])) * jnp.max(jnp.abs(g_k))
    score_bound = (1.02 * dqk_b ** 0.5 * LOG2E * gain_bound + 1.0).astype(F32).reshape(1)
    past = page_table.shape[1] * page
    groups = 2 * nh_b
    slope_col = jnp.repeat(slopes, 2 * nt).reshape(nt * groups, 1)
    qpos_col = (past + jnp.tile(jnp.arange(nt, dtype=jnp.int32), groups)).reshape(nt * groups, 1)
    pad_rows = lambda a, r: jnp.pad(a.reshape(nseq, nt, w), ((0, 0), (0, r - nt), (0, 0)))
    q_rep = jnp.tile(qbs.reshape(nseq, nt, w), (1, SUBLANES // nt, 1))
    k_t = cache_k.transpose(0, 2, 3, 4, 1).reshape(n_pool, w, page)
    v_rows = cache_v.reshape(n_pool, page * nh_b, dv_b)
    ap, as8 = _attn(qtp, kbp.reshape(b, t, w), vtp, slopes, score_bound, lvec,
                    q_rep, k_t, v_rows, page_table, pad_rows(kbs, 2 * SUBLANES), pad_rows(vbs, 2 * SUBLANES),
                    slope_col, qpos_col, nh_b, nt, lam_init)
    as_ = as8[:, :nt].reshape(nseq * nt, w)

    fin_w = (_tile(subln_b[j], nh_b), w_out_b[j].astype(BF16), dv_b, 1.0 - lam_init)
    yp = _final(ap.reshape(b * t, w), zbp, y1p, modb_p, t // tm_wide, tm_wide, *fin_w)
    ys = _final(as_, zbs, y1s, wide(modb_s), 1, tm_wide, *fin_w)

    kp = ktp.reshape(b, nh_b, 2, dqk_b, t).transpose(0, 4, 1, 2, 3)
    return (yp.reshape(b, t, d), ys.reshape(nseq, nt, d), state_p[None], state_s[None],
            kp, vp.reshape(b, t, nh_b, dv_b),
            ks.reshape(nseq, nt, nh_b, 2, dqk_b), vs.reshape(nseq, nt, nh_b, dv_b))
```

```python
import functools
import math

import jax
import jax.numpy as jnp
from jax import lax
from jax.experimental import pallas as pl
from jax.experimental.pallas import tpu as pltpu

F32 = jnp.float32
BF16 = jnp.bfloat16
EPS = 1e-6
GATE_NORMALIZER = 16.0
LOG2E = math.log2(math.e)
LANES = 128
SUBLANES = 8
VMEM_LIMIT = 56 * 1024 * 1024

NT_DIMS = (((1,), (1,)), ((), ()))
TN_DIMS = (((0,), (0,)), ((), ()))


def _params(*sem):
    return pltpu.CompilerParams(dimension_semantics=sem, vmem_limit_bytes=VMEM_LIMIT)


def _silu(x):
    return x / (1.0 + jnp.exp(-x))


def _rms_rows(x):
    return x * lax.rsqrt(jnp.mean(x * x, axis=-1, keepdims=True) + EPS)


def _const_spec(shape):
    zeros = (0,) * len(shape)
    return pl.BlockSpec(shape, lambda *_: zeros, pipeline_mode=pl.Buffered(1))


def _ada_kernel(c_ref, w_ref, b_ref, head_ref, rest_ref):
    s = _silu(c_ref[...]).astype(BF16)
    mod = jnp.dot(s, w_ref[...].astype(BF16), preferred_element_type=F32) + b_ref[...]
    head = head_ref.shape[0]
    head_ref[...] = mod[0:head]
    rest_ref[...] = mod[head:]


def _ada(c, w, b, head, tn=512):
    n, d = c.shape
    m = w.shape[1]
    return pl.pallas_call(
        _ada_kernel,
        grid=(m // tn,),
        in_specs=[pl.BlockSpec((n, d), lambda j: (0, 0)),
                  pl.BlockSpec((d, tn), lambda j: (0, j)),
                  pl.BlockSpec((1, tn), lambda j: (0, j))],
        out_specs=[pl.BlockSpec((head, tn), lambda j: (0, j)), pl.BlockSpec((n - head, tn), lambda j: (0, j))],
        out_shape=[jax.ShapeDtypeStruct((head, m), F32), jax.ShapeDtypeStruct((n - head, m), F32)],
        compiler_params=_params("arbitrary"),
        name="ada_mod",
    )(c, w, b.reshape(1, m))


def _mod_spec(mod3, tm, tiles_per_group):
    r = mod3.shape[1]
    w = mod3.shape[2]
    if r == 1:
        return pl.BlockSpec((None, 1, w), lambda i: (i // tiles_per_group, 0, 0))
    return pl.BlockSpec((None, tm, w), lambda i: (i, 0, 0))


def _row_spec(tm, w):
    return pl.BlockSpec((tm, w), lambda i: (i, 0))


def _gla_in_kernel(x_ref, mod_ref, g_ref, w_ref, wg2_ref, bg_ref,
                   q_ref, k_ref, v_ref, z_ref, lg_ref, *, ka, va, dk):
    x = x_ref[...]
    d = x.shape[-1]
    mod = mod_ref[...]
    h = _rms_rows(x) * g_ref[...] * (1.0 + mod[:, d:2 * d]) + mod[:, 0:d]
    p = jnp.dot(h.astype(BF16), w_ref[...], preferred_element_type=F32)
    q_ref[...] = p[:, 0:ka] * (dk ** -0.5)
    k_ref[...] = p[:, ka:2 * ka]
    v_ref[...] = p[:, 2 * ka:2 * ka + va]
    z_ref[...] = p[:, 2 * ka + va:2 * ka + 2 * va]
    glow = p[:, 2 * ka + 2 * va:]
    g2 = jnp.dot(glow.astype(BF16), wg2_ref[...], preferred_element_type=F32) + bg_ref[...]
    logsig = jnp.minimum(g2, 0.0) - jnp.log1p(jnp.exp(-jnp.abs(g2)))
    lg_ref[...] = logsig * (1.0 / GATE_NORMALIZER)


def _gla_in(x2, mod3, tiles_per_group, tm, g, w_pad, wg2_pad, bg, ka, va, dk):
    n, d = x2.shape
    wcols = w_pad.shape[1]
    kern = functools.partial(_gla_in_kernel, ka=ka, va=va, dk=dk)
    outs = [jax.ShapeDtypeStruct((n, ka), F32), jax.ShapeDtypeStruct((n, ka), F32),
            jax.ShapeDtypeStruct((n, va), F32), jax.ShapeDtypeStruct((n, va), F32),
            jax.ShapeDtypeStruct((n, ka), F32)]
    return pl.pallas_call(
        kern,
        grid=(n // tm,),
        in_specs=[_row_spec(tm, d), _mod_spec(mod3, tm, tiles_per_group),
                  _const_spec((1, d)), _const_spec((d, wcols)),
                  _const_spec(wg2_pad.shape), _const_spec((1, ka))],
        out_specs=[_row_spec(tm, ka), _row_spec(tm, ka), _row_spec(tm, va),
                   _row_spec(tm, va), _row_spec(tm, ka)],
        out_shape=outs,
        compiler_params=_params("arbitrary"),
        name="gla_in",
    )(x2, mod3, g.reshape(1, d), w_pad, wg2_pad, bg.reshape(1, ka))


def _cumsum_rows(tril_b, x):
    hi = x.astype(BF16)
    rest = x - hi.astype(F32)
    mid = rest.astype(BF16)
    lo = (rest - mid.astype(F32)).astype(BF16)
    dot = lambda part: jnp.dot(tril_b, part, preferred_element_type=F32)
    return dot(hi) + dot(mid) + dot(lo)


def _level_masks(row, col, max_block):
    levels = []
    n = 2
    while n <= max_block:
        sh = (n // 2).bit_length() - 1
        rb = row >> sh
        cb = col >> sh
        levels.append((n, jnp.where((cb & 1) == 0, rb - cb, 0) == 1))
        n *= 2
    return levels


def _midpoint_rows(cum, n, sub):
    c, dk = cum.shape
    if n == 2:
        return jnp.where((sub & 1) == 1, pltpu.roll(cum, 1, 0), cum)
    if n == 4:
        r = sub & 3
        return jnp.where(r == 0, pltpu.roll(cum, c - 1, 0),
                         jnp.where(r == 1, cum,
                                   jnp.where(r == 2, pltpu.roll(cum, 1, 0), pltpu.roll(cum, 2, 0))))
    half = n // 2
    return jnp.concatenate(
        [jnp.broadcast_to(cum[i * n + half - 1:i * n + half, :], (n, dk)) for i in range(c // n)], axis=0)


def _intra_operands(q, k, cum, levels, sub):
    ops = []
    for n, _ in levels:
        ref = _midpoint_rows(cum, n, sub)
        ops.append(((q * jnp.exp(cum - ref)).astype(BF16), (k * jnp.exp(ref - cum)).astype(BF16)))
    return ops, jnp.sum(q * k, axis=-1, keepdims=True)


def _intra_scores(ops, own, levels, eye):
    a = jnp.where(eye, own, 0.0)
    for (_, mask), (qs, ks) in zip(levels, ops):
        a = jnp.where(mask, lax.dot_general(qs, ks, NT_DIMS, preferred_element_type=F32), a)
    return a


def _gla_prompt_kernel(q_ref, k_ref, v_ref, lg_ref, o_ref, s_ref, st_scr, *, c, nc, nh, dk, dv):
    ci = pl.program_id(1)

    @pl.when(ci == 0)
    def _():
        st_scr[...] = jnp.zeros_like(st_scr)

    row = lax.broadcasted_iota(jnp.int32, (c, c), 0)
    col = lax.broadcasted_iota(jnp.int32, (c, c), 1)
    sub = lax.broadcasted_iota(jnp.int32, (c, dk), 0)
    tril_b = jnp.where(col <= row, 1.0, 0.0).astype(BF16)
    levels = _level_masks(row, col, c)
    pairs = [(j, h) for j in range(nc) for h in range(nh)]
    rows = [slice(j * c, (j + 1) * c) for j in range(nc)]
    sk = [slice(h * dk, (h + 1) * dk) for h in range(nh)]
    sv = [slice(h * dv, (h + 1) * dv) for h in range(nh)]
    cums = [_cumsum_rows(tril_b, lg_ref[rows[j], :]) for j in range(nc)]
    prep = {}
    for j, h in pairs:
        qh, kh, cum = q_ref[rows[j], sk[h]], k_ref[rows[j], sk[h]], cums[j][:, sk[h]]
        last = cum[c - 1:c, :]
        ops, own = _intra_operands(qh, kh, cum, levels, sub)
        prep[j, h] = (ops, own, (qh * jnp.exp(cum)).astype(BF16),
                      (kh * jnp.exp(last - cum)).astype(BF16), jnp.exp(last))
    local = {}
    for j, h in pairs:
        ops, own, _, kd, _ = prep[j, h]
        vb = v_ref[rows[j], sv[h]].astype(BF16)
        a = _intra_scores(ops, own, levels, row == col)
        local[j, h] = (jnp.dot(a.astype(BF16), vb, preferred_element_type=F32),
                       lax.dot_general(vb, kd, TN_DIMS, preferred_element_type=F32))
    for h in range(nh):
        st = st_scr[h]
        for j in range(nc):
            o_intra, update = local[j, h]
            _, _, qe, _, decay = prep[j, h]
            o_ref[rows[j], sv[h]] = o_intra + lax.dot_general(
                qe, st.astype(BF16), NT_DIMS, preferred_element_type=F32)
            st = decay * st + update
        st_scr[h] = st

    @pl.when(ci == pl.num_programs(1) - 1)
    def _():
        for h in range(nh):
            s_ref[h] = st_scr[h].T


def _gla_prompt(q, k, v, lg, b, t, nh, dk, dv, c=64, chunks_per_step=4):
    n = b * t
    rows = c * chunks_per_step
    assert t % rows == 0
    steps = t // rows
    kern = functools.partial(_gla_prompt_kernel, c=c, nc=chunks_per_step, nh=nh, dk=dk, dv=dv)
    row = lambda w: pl.BlockSpec((rows, w), lambda bi, ci: (bi * steps + ci, 0))
    return pl.pallas_call(
        kern,
        grid=(b, steps),
        in_specs=[row(nh * dk), row(nh * dk), row(nh * dv), row(nh * dk)],
        out_specs=[row(nh * dv),
                   pl.BlockSpec((None, nh, dk, dv), lambda bi, ci: (bi, 0, 0, 0))],
        out_shape=[jax.ShapeDtypeStruct((n, nh * dv), F32),
                   jax.ShapeDtypeStruct((b, nh, dk, dv), F32)],
        scratch_shapes=[pltpu.VMEM((nh, dv, dk), F32)],
        compiler_params=_params("arbitrary", "arbitrary"),
        name="gla_prompt",
    )(q, k, v, lg)


def _gla_sample_kernel(q_ref, k_ref, v_ref, lg_ref, s0_ref, o_ref, s_ref, *, g, nh, dk, dv):
    r = g * SUBLANES
    row = lax.broadcasted_iota(jnp.int32, (r, r), 0)
    col = lax.broadcasted_iota(jnp.int32, (r, r), 1)
    sub = lax.broadcasted_iota(jnp.int32, (r, dk), 0)
    same_seq = (row >> 3) == (col >> 3)
    cum_all = _cumsum_rows(jnp.where(same_seq & (col <= row), 1.0, 0.0).astype(BF16), lg_ref[...])
    levels = _level_masks(row, col, SUBLANES)
    for h in range(nh):
        sk = slice(h * dk, (h + 1) * dk)
        sv = slice(h * dv, (h + 1) * dv)
        qh = q_ref[:, sk]
        kh = k_ref[:, sk]
        vh = v_ref[:, sv]
        cum = cum_all[:, sk]
        ops, own = _intra_operands(qh, kh, cum, levels, sub)
        a = _intra_scores(ops, own, levels, row == col)
        o_intra = jnp.dot(a.astype(BF16), vh.astype(BF16), preferred_element_type=F32)
        qe = qh * jnp.exp(cum)
        first_row = lax.broadcasted_iota(jnp.int32, (SUBLANES, dk), 0) == 0
        ones = jnp.ones((SUBLANES, LANES), F32)
        for i in range(g):
            rs = slice(i * SUBLANES, (i + 1) * SUBLANES)
            st = s0_ref[i, h]
            o_ref[rs, sv] = o_intra[rs] + jnp.dot(
                qe[rs].astype(BF16), st.astype(BF16), preferred_element_type=F32)
            last = cum[(i + 1) * SUBLANES - 1:(i + 1) * SUBLANES, :]
            kd = (kh[rs] * jnp.exp(last - cum[rs])).astype(BF16)
            decay_row = jnp.where(first_row, jnp.broadcast_to(jnp.exp(last), (SUBLANES, dk)), 0.0)
            decay = lax.dot_general(decay_row, ones, TN_DIMS, precision=lax.Precision.HIGHEST,
                                    preferred_element_type=F32)
            s_ref[i, h] = jnp.concatenate([decay] * (dv // LANES), axis=1) * st + lax.dot_general(
                kd, vh[rs].astype(BF16), TN_DIMS, preferred_element_type=F32)


def _gla_sample(q, k, v, lg, s0, nh, dk, dv, g=8):
    nseq = s0.shape[0]
    r = g * SUBLANES
    kern = functools.partial(_gla_sample_kernel, g=g, nh=nh, dk=dk, dv=dv)
    st_spec = pl.BlockSpec((g, nh, dk, dv), lambda i: (i, 0, 0, 0))
    return pl.pallas_call(
        kern,
        grid=(nseq // g,),
        in_specs=[_row_spec(r, nh * dk), _row_spec(r, nh * dk), _row_spec(r, nh * dv),
                  _row_spec(r, nh * dk), st_spec],
        out_specs=[_row_spec(r, nh * dv), st_spec],
        out_shape=[jax.ShapeDtypeStruct((nseq * SUBLANES, nh * dv), F32),
                   jax.ShapeDtypeStruct(s0.shape, F32)],
        compiler_params=_params("arbitrary"),
        name="gla_sample",
    )(q, k, v, lg, s0)


def _rms_groups64(x, lane_lo):
    outs = []
    for j in range(x.shape[-1] // LANES):
        xj = x[:, j * LANES:(j + 1) * LANES]
        sq = xj * xj
        s0 = jnp.sum(jnp.where(lane_lo, sq, 0.0), axis=-1, keepdims=True)
        s1 = jnp.sum(jnp.where(lane_lo, 0.0, sq), axis=-1, keepdims=True)
        ms = jnp.where(lane_lo, s0, s1) * (2.0 / LANES)
        outs.append(xj * lax.rsqrt(ms + EPS))
    return jnp.concatenate(outs, axis=-1)


def _mid_kernel(o_ref, z_ref, x_ref, moda_ref, modkv_ref, modb_ref,
                onorm_ref, wout_ref, gkv_ref, wk_ref, wv_ref, gk_ref, gb_ref, winb_ref, gq_ref,
                y_ref, k_ref, v_ref, kb_ref, vb_ref, qb_ref, zb_ref, *, nh, dv, qk_scale, token_minor):
    d = x_ref.shape[-1]
    o = o_ref[...]
    on = jnp.concatenate([_rms_rows(o[:, h * dv:(h + 1) * dv]) for h in range(nh)], axis=-1)
    on = on * onorm_ref[...]
    u = (on * _silu(z_ref[...])).astype(BF16)
    out = jnp.dot(u, wout_ref[...], preferred_element_type=F32)
    y = x_ref[...] + moda_ref[:, 2 * d:3 * d] * out
    y_ref[...] = y
    yn = _rms_rows(y)
    lane_lo = lax.broadcasted_iota(jnp.int32, (1, LANES), 1) < (LANES // 2)

    modkv = modkv_ref[...]
    h2 = (yn * gkv_ref[...] * (1.0 + modkv[:, d:2 * d]) + modkv[:, 0:d]).astype(BF16)
    kk = _rms_groups64(jnp.dot(h2, wk_ref[...], preferred_element_type=F32), lane_lo) * gk_ref[...]
    k_ref[...] = kk.T if token_minor else kk
    kb_ref[...] = kk.astype(BF16)
    vv = jnp.dot(h2, wv_ref[...], preferred_element_type=F32)
    v_ref[...] = vv
    vb_ref[...] = (vv.T if token_minor else vv).astype(BF16)

    modb = modb_ref[...]
    h3 = (yn * gb_ref[...] * (1.0 + modb[:, d:2 * d]) + modb[:, 0:d]).astype(BF16)
    qz = jnp.dot(h3, winb_ref[...], preferred_element_type=F32)
    qb = qz.shape[-1] - zb_ref.shape[-1]
    qq = _rms_groups64(qz[:, 0:qb], lane_lo) * (gq_ref[...] * qk_scale)
    qb_ref[...] = (qq.T if token_minor else qq).astype(BF16)
    zb_ref[...] = qz[:, qb:].astype(zb_ref.dtype)


def _mid(o, z, x2, moda3, modkv3, modb3, tiles_per_group, tm,
         onorm_t, wout, gkv, wk, wv, gk_t, gb, winb, gq_t, nh, dv, qk_scale, token_minor):
    n, d = x2.shape
    qb = wk.shape[1]
    zb = winb.shape[1] - qb
    kern = functools.partial(_mid_kernel, nh=nh, dv=dv, qk_scale=qk_scale, token_minor=token_minor)
    ms = lambda m: _mod_spec(m, tm, tiles_per_group)
    if token_minor:
        groups = n // (tm * tiles_per_group)
        t = tm * tiles_per_group
        feat = lambda w, dt: jax.ShapeDtypeStruct((groups, w, t), dt)
        feat_spec = lambda w: pl.BlockSpec(
            (None, w, tm), lambda i: (i // tiles_per_group, 0, i % tiles_per_group))
    else:
        feat = lambda w, dt: jax.ShapeDtypeStruct((n, w), dt)
        feat_spec = lambda w: _row_spec(tm, w)
    outs = [jax.ShapeDtypeStruct((n, d), F32), feat(qb, F32),
            jax.ShapeDtypeStruct((n, zb), F32), jax.ShapeDtypeStruct((n, qb), BF16),
            feat(zb, BF16), feat(qb, BF16),
            jax.ShapeDtypeStruct((n, zb), BF16)]
    out_specs = [_row_spec(tm, d), feat_spec(qb), _row_spec(tm, zb), _row_spec(tm, qb),
                 feat_spec(zb), feat_spec(qb), _row_spec(tm, zb)]
    return pl.pallas_call(
        kern,
        grid=(n // tm,),
        in_specs=[_row_spec(tm, o.shape[1]), _row_spec(tm, z.shape[1]), _row_spec(tm, d),
                  ms(moda3), ms(modkv3), ms(modb3),
                  _const_spec(onorm_t.shape), _const_spec(wout.shape), _const_spec(gkv.shape),
                  _const_spec(wk.shape), _const_spec(wv.shape), _const_spec(gk_t.shape),
                  _const_spec(gb.shape), _const_spec(winb.shape), _const_spec(gq_t.shape)],
        out_specs=out_specs,
        out_shape=outs,
        compiler_params=_params("arbitrary"),
        name="mid_proj",
    )(o, z, x2, moda3, modkv3, modb3, onorm_t, wout, gkv, wk, wv, gk_t, gb, winb, gq_t)


def _lambda_full(l_ref, lam_init):
    lv = l_ref[...]
    s1 = jnp.sum(lv[0:1] * lv[1:2], axis=-1, keepdims=True)
    s2 = jnp.sum(lv[2:3] * lv[3:4], axis=-1, keepdims=True)
    return jnp.exp(s1) - jnp.exp(s2) + lam_init


MAX_STATIC_SHIFT = 60.0


def _attn_prompt_body(hg, qi, slope_ref, bound_ref, l_ref, qt_ref, k_ref, vt_ref, o_ref, *, tq, tk, hps, lam_init):
    half = LANES // 2
    heads = range(hps)
    cols = [slice(hh * LANES, (hh + 1) * LANES) for hh in heads]
    zeros = jnp.zeros((half, tq), BF16)
    krow = lax.broadcasted_iota(jnp.int32, (tk, LANES), 0).astype(F32)
    ones_rows = jnp.ones((2 * SUBLANES, tk), BF16)
    slopes, qqs = [], []
    for hh in heads:
        slopes.append(slope_ref[hg * hps + hh])
        qt = qt_ref[cols[hh], :]
        qqs.append(jnp.concatenate([jnp.concatenate([qt[0:half], zeros], axis=0),
                                    jnp.concatenate([zeros, qt[half:LANES]], axis=0)], axis=1))

    def scores(j, causal):
        start = pl.multiple_of(j * tk, tk)
        offset = (j * tk - qi * tq).astype(F32)
        out = []
        for hh in heads:
            key_bias = slopes[hh] * (krow + offset)
            s = jnp.dot(k_ref[pl.ds(start, tk), cols[hh]], qqs[hh], preferred_element_type=F32)
            out.append(s + jnp.concatenate([key_bias] * (2 * tq // LANES), axis=1))
        if causal:
            kk = lax.broadcasted_iota(jnp.int32, (tk, 2 * tq), 0)
            qpos = lax.broadcasted_iota(jnp.int32, (tk, 2 * tq), 1)
            keep = kk <= jnp.where(qpos >= tq, qpos - tq, qpos)
            out = [jnp.where(keep, s, -jnp.inf) for s in out]
        return start, out

    def weighted_values(start, probs):
        return [jnp.dot(jnp.concatenate([vt_ref[cols[hh], pl.ds(start, tk)], ones_rows], axis=0), probs[hh],
                        preferred_element_type=F32) for hh in heads]

    def tile_fixed(j, accs, causal, shifts):
        start, ss = scores(j, causal)
        pvs = weighted_values(start, [jnp.exp2(ss[hh] - shifts[hh]).astype(BF16) for hh in heads])
        return tuple(accs[hh] + pvs[hh] for hh in heads)

    def tile_online(j, carries, causal):
        start, ss = scores(j, causal)
        stats, probs = [], []
        for hh in heads:
            m = carries[hh][0]
            m_new = jnp.maximum(m, jnp.max(ss[hh], axis=0, keepdims=True))
            stats.append((m_new, jnp.exp2(m - m_new)))
            probs.append(jnp.exp2(ss[hh] - m_new).astype(BF16))
        pvs = weighted_values(start, probs)
        return tuple((stats[hh][0], stats[hh][1] * carries[hh][1] + pvs[hh]) for hh in heads)

    def write(accs):
        lam = _lambda_full(l_ref, lam_init)
        for hh in heads:
            r = accs[hh][0:LANES] * (1.0 / accs[hh][LANES:LANES + 1])
            o_ref[:, cols[hh]] = (r[:, 0:tq] - lam * r[:, tq:2 * tq]).T.astype(o_ref.dtype)

    bound = bound_ref[0]
    zero_acc = tuple(jnp.zeros((LANES + 2 * SUBLANES, 2 * tq), F32) for _ in heads)

    @pl.when(bound <= MAX_STATIC_SHIFT)
    def _():
        qidx = lax.broadcasted_iota(jnp.int32, (1, 2 * tq), 1)
        qidx = jnp.where(qidx >= tq, qidx - tq, qidx).astype(F32)
        shifts = [bound + slopes[hh] * qidx for hh in heads]
        accs = lax.fori_loop(0, qi, lambda j, a: tile_fixed(j, a, False, shifts), zero_acc)
        write(tile_fixed(qi, accs, True, shifts))

    @pl.when(bound > MAX_STATIC_SHIFT)
    def _():
        init = tuple((jnp.full((1, 2 * tq), -jnp.inf, F32), zero_acc[hh]) for hh in heads)
        carries = lax.fori_loop(0, qi, lambda j, c: tile_online(j, c, False), init)
        write([c[1] for c in tile_online(qi, carries, True)])


def _attn_sample_body(l_ref, slope_ref, qpos_ref, q_ref, k_refs, v_refs, kn_ref, vn_ref, o_ref,
                      *, nh, nt, page, lam_init):
    n_pages = len(k_refs)
    past = n_pages * page
    w = q_ref.shape[-1]
    rows = 2 * nh * nt
    hr = 2 * nt
    dv = w // nh

    q = q_ref[...].astype(F32)
    qt = jnp.concatenate([q] * (rows // SUBLANES), axis=0)
    lane_grp = lax.broadcasted_iota(jnp.int32, (rows, w), 1) // (LANES // 2)
    row_grp = lax.broadcasted_iota(jnp.int32, (rows, w), 0) // nt
    qbd = jnp.where(lane_grp == row_grp, qt, 0.0).astype(BF16)

    s_past = jnp.concatenate(
        [jnp.dot(qbd, k_refs[i][...].astype(BF16), preferred_element_type=F32) for i in range(n_pages)],
        axis=1)
    kpos = lax.broadcasted_iota(jnp.int32, (1, past), 1)
    s_past = s_past - slope_ref[...] * (qpos_ref[...] - kpos).astype(F32)
    idx = lax.broadcasted_iota(jnp.int32, (1, kn_ref.shape[0]), 1)
    dist = qpos_ref[...] - (past + idx)
    s_new = lax.dot_general(qbd, kn_ref[...], NT_DIMS, preferred_element_type=F32)
    s_new = jnp.where((idx < nt) & (dist >= 0), s_new - slope_ref[...] * dist.astype(F32), -jnp.inf)

    m = jnp.maximum(jnp.max(s_past, axis=-1, keepdims=True), jnp.max(s_new, axis=-1, keepdims=True))
    pr_past = jnp.exp2(s_past - m)
    pr_new = jnp.exp2(s_new - m)
    denom = jnp.sum(pr_past, axis=-1, keepdims=True) + jnp.sum(pr_new, axis=-1, keepdims=True)
    heads_pv = []
    for h in range(nh):
        vh = jnp.concatenate([v_refs[i][pl.ds(h, page, stride=nh), :] for i in range(n_pages)], axis=0)
        heads_pv.append(jnp.dot(pr_past[h * hr:(h + 1) * hr].astype(BF16), vh.astype(BF16),
                                preferred_element_type=F32))
    pv = jnp.concatenate(heads_pv, axis=0)
    pv_new = jnp.dot(pr_new.astype(BF16), vn_ref[...], preferred_element_type=F32)
    own_new = jnp.concatenate([pv_new[h * hr:(h + 1) * hr, h * dv:(h + 1) * dv] for h in range(nh)], axis=0)
    r = (pv + own_new) / denom
    diff = r - _lambda_full(l_ref, lam_init) * pltpu.roll(r, rows - nt, 0)
    for h in range(nh):
        o_ref[:, h * dv:(h + 1) * dv] = diff[h * hr:(h + 1) * hr]


def _attn_kernel(pt_ref, slope_ref, bound_ref, l_ref, qt_ref, k_ref, vt_ref,
                 slope_col_ref, qpos_ref, q_ref, *refs,
                 n_pages, prompt_steps, sample_steps, steps_per_batch, q_tiles, prompt_args, sample_args):
    del pt_ref
    k_refs, v_refs = refs[0:n_pages], refs[n_pages:2 * n_pages]
    kn_ref, vn_ref, op_ref, os_ref = refs[2 * n_pages:]
    i = pl.program_id(0)

    def prompt_part():
        within = i % steps_per_batch
        _attn_prompt_body(within // q_tiles, within % q_tiles, slope_ref, bound_ref, l_ref,
                          qt_ref, k_ref, vt_ref, op_ref, **prompt_args)

    def sample_part():
        _attn_sample_body(l_ref, slope_col_ref, qpos_ref, q_ref, k_refs, v_refs, kn_ref, vn_ref,
                          os_ref, **sample_args)

    if prompt_steps < sample_steps:
        pl.when(i < prompt_steps)(prompt_part)
    else:
        prompt_part()
    if sample_steps < prompt_steps:
        pl.when(i < sample_steps)(sample_part)
    else:
        sample_part()


def _attn(qt, kb, vt, slopes, bound, lvec, q8, k_t, v_rows, page_table, kn, vn, slope_col, qpos_col,
          nh, nt, lam_init, tq=512, hps=2):
    b, t, w = kb.shape
    nseq = q8.shape[0]
    n_pool, _, page = k_t.shape
    dv = v_rows.shape[-1]
    n_pages = page_table.shape[1]
    nrows = kn.shape[1]
    rows = 2 * nh * nt
    assert 2 * nt == SUBLANES and nh % hps == 0 and t % tq == 0
    gw = hps * LANES
    q_tiles = t // tq
    steps_per_batch = (nh // hps) * q_tiles
    prompt_steps = b * steps_per_batch
    steps = max(prompt_steps, nseq)

    def prompt_index(i):
        i = jnp.minimum(i, prompt_steps - 1)
        within = i % steps_per_batch
        return i // steps_per_batch, within // q_tiles, within % q_tiles

    def seq_index(i):
        return jnp.minimum(i, nseq - 1)

    def prompt_spec(shape, pick):
        return pl.BlockSpec(shape, lambda i, pt: pick(*prompt_index(i)))

    k_specs = [pl.BlockSpec((None, w, page), lambda i, pt, p=p: (pt[seq_index(i), p], 0, 0))
               for p in range(n_pages)]
    v_specs = [pl.BlockSpec((None, page * nh, dv), lambda i, pt, p=p: (pt[seq_index(i), p], 0, 0))
               for p in range(n_pages)]
    seq_spec = lambda r: pl.BlockSpec((None, r, w), lambda i, pt: (seq_index(i), 0, 0))
    const = lambda shape: pl.BlockSpec(shape, lambda i, pt: (0,) * len(shape))
    smem = pl.BlockSpec(memory_space=pltpu.SMEM)
    kern = functools.partial(
        _attn_kernel, n_pages=n_pages, prompt_steps=prompt_steps, sample_steps=nseq,
        steps_per_batch=steps_per_batch, q_tiles=q_tiles,
        prompt_args=dict(tq=tq, tk=tq, hps=hps, lam_init=lam_init),
        sample_args=dict(nh=nh, nt=nt, page=page, lam_init=lam_init))
    grid_spec = pltpu.PrefetchScalarGridSpec(
        num_scalar_prefetch=1,
        grid=(steps,),
        in_specs=[smem, smem, const(lvec.shape),
                  prompt_spec((None, gw, tq), lambda bi, hi, qi: (bi, hi, qi)),
                  prompt_spec((None, t, gw), lambda bi, hi, qi: (bi, 0, hi)),
                  prompt_spec((None, gw, t), lambda bi, hi, qi: (bi, hi, 0)),
                  const(slope_col.shape), const(qpos_col.shape), seq_spec(SUBLANES)]
        + k_specs + v_specs + [seq_spec(nrows), seq_spec(nrows)],
        out_specs=[prompt_spec((None, tq, gw), lambda bi, hi, qi: (bi, qi, hi)), seq_spec(SUBLANES)],
    )
    return pl.pallas_call(
        kern,
        grid_spec=grid_spec,
        out_shape=[jax.ShapeDtypeStruct((b, t, w), BF16), jax.ShapeDtypeStruct((nseq, SUBLANES, w), F32)],
        compiler_params=_params("arbitrary"),
        name="attn",
    )(page_table, slopes, bound, lvec, qt, kb, vt, slope_col, qpos_col, q8,
      *([k_t] * n_pages), *([v_rows] * n_pages), kn, vn)


def _final_kernel(o_ref, z_ref, y_ref, modb_ref, gsub_ref, wout_ref, out_ref, *, dvb, post_scale):
    d = y_ref.shape[-1]
    o = o_ref[...].astype(F32)
    on = jnp.concatenate([_rms_rows(o[:, h * dvb:(h + 1) * dvb]) for h in range(o.shape[-1] // dvb)], axis=-1)
    on = on * gsub_ref[...] * post_scale
    u = (on * _silu(z_ref[...].astype(F32))).astype(BF16)
    out = jnp.dot(u, wout_ref[...], preferred_element_type=F32)
    out_ref[...] = y_ref[...] + modb_ref[:, 2 * d:3 * d] * out


def _final(o, z, y, modb3, tiles_per_group, tm, gsub_t, wout, dvb, post_scale):
    n, d = y.shape
    kern = functools.partial(_final_kernel, dvb=dvb, post_scale=post_scale)
    return pl.pallas_call(
        kern,
        grid=(n // tm,),
        in_specs=[_row_spec(tm, o.shape[1]), _row_spec(tm, z.shape[1]), _row_spec(tm, d),
                  _mod_spec(modb3, tm, tiles_per_group), _const_spec(gsub_t.shape), _const_spec(wout.shape)],
        out_specs=_row_spec(tm, d),
        out_shape=jax.ShapeDtypeStruct((n, d), F32),
        compiler_params=_params("arbitrary"),
        name="final_proj",
    )(o, z, y, modb3, gsub_t, wout)


def _tile(vec, reps):
    return jnp.tile(vec.astype(F32), reps).reshape(1, -1)


def kernel(x_prompt, x_sample, c_prompt, c_sample, state_gla, cache_k, cache_v, page_table, norm_a, ada_w_a, ada_b_a, w_in_a, w_g2_a, b_g_a, onorm_a, w_out_a, norm_kv, ada_w_kv, ada_b_kv, w_k, w_v, g_k, norm_b, ada_w_b, ada_b_b, w_in_b, g_q, lambda_q1, lambda_k1, lambda_q2, lambda_k2, subln_b, w_out_b):
    b, t, d = x_prompt.shape
    nseq, nt, _ = x_sample.shape
    n_a, _, nh_a, dk_a, dv_a = state_gla.shape
    n_b = norm_b.shape[0]
    ka, va = nh_a * dk_a, nh_a * dv_a
    nh_b, dv_b = cache_v.shape[2], cache_v.shape[3]
    dqk_b = cache_k.shape[4]
    n_pool, page = cache_k.shape[0], cache_k.shape[1]
    tm = 256
    tm_wide = 512
    assert n_a == 1 and n_b == 1, "one GLA layer and one differential-attention layer"
    assert nt <= SUBLANES and t % tm_wide == 0 and (nseq * nt) % tm_wide == 0

    head = -(-b // SUBLANES) * SUBLANES
    c_all = jnp.concatenate([jnp.pad(c_prompt, ((0, head - b), (0, 0))), jnp.repeat(c_sample, nt, axis=0)], axis=0)
    xp = x_prompt.reshape(b * t, d)
    xs = x_sample.reshape(nseq * nt, d)
    tiles_p = t // tm

    def split_mod(mods):
        mod_head, mod_rest = mods
        return mod_head[:b].reshape(b, 1, -1), mod_rest.reshape((nseq * nt) // tm, tm, -1)

    wide = lambda ms: ms.reshape(-1, tm_wide, ms.shape[-1])

    pad8 = lambda a: jnp.pad(a.reshape(nseq, nt, -1), ((0, 0), (0, SUBLANES - nt), (0, 0)))

    l = 0
    moda_p, moda_s = split_mod(_ada(c_all, ada_w_a[l], ada_b_a[l], head))
    rank = w_g2_a.shape[1]
    w_in_pad = jnp.pad(w_in_a[l], ((0, 0), (0, LANES - rank))).astype(BF16)
    wg2_pad = jnp.pad(w_g2_a[l], ((0, LANES - rank), (0, 0))).astype(BF16)
    gla_args = (norm_a[l], w_in_pad, wg2_pad, b_g_a[l], ka, va, dk_a)
    qp, kp_, vp_, zp, lgp = _gla_in(xp, moda_p, t // tm_wide, tm_wide, *gla_args)
    qs, ks_, vs_, zs, lgs = _gla_in(xs, wide(moda_s), 1, tm_wide, *gla_args)
    op, state_p = _gla_prompt(qp, kp_, vp_, lgp, b, t, nh_a, dk_a, dv_a)
    os8, state_s = _gla_sample(pad8(qs).reshape(-1, ka), pad8(ks_).reshape(-1, ka),
                               pad8(vs_).reshape(-1, va), pad8(lgs).reshape(-1, ka),
                               state_gla[l], nh_a, dk_a, dv_a)
    os_ = os8.reshape(nseq, SUBLANES, va)[:, :nt].reshape(nseq * nt, va)

    j = 0
    lam_init = 0.8 - 0.6 * math.exp(-0.3 * (n_a + j))
    modkv_p, modkv_s = split_mod(_ada(c_all, ada_w_kv, ada_b_kv, head))
    modb_p, modb_s = split_mod(_ada(c_all, ada_w_b[j], ada_b_b[j], head))
    mid_w = (_tile(onorm_a[l], nh_a), w_out_a[l].astype(BF16), norm_kv.reshape(1, d),
             w_k.astype(BF16), w_v.astype(BF16), _tile(g_k, 2 * nh_b), norm_b[j].reshape(1, d),
             w_in_b[j].astype(BF16), _tile(g_q[j], 2 * nh_b), nh_a, dv_a, dqk_b ** -0.5 * LOG2E)
    y1p, ktp, vp, kbp, vtp, qtp, zbp = _mid(
        op, zp, xp, moda_p, modkv_p, modb_p, tiles_p, tm, *mid_w, token_minor=True)
    y1s, ks, vs, kbs, vbs, qbs, zbs = _mid(
        os_, zs, xs, moda_s, modkv_s, modb_s, 1, tm, *mid_w, token_minor=False)

    w = nh_b * dv_b
    lvec = jnp.stack([lambda_q1[j], lambda_k1[j], lambda_q2[j], lambda_k2[j]]).astype(F32)
    slopes = 2.0 ** (-8.0 * jnp.arange(1, nh_b + 1, dtype=F32) / nh_b) * LOG2E
    gain_bound = jnp.max(jnp.abs(g_q[j])) * jnp.max(jnp.abs(g_k))
    score_bound = (1.02 * dqk_b ** 0.5 * LOG2E * gain_bound + 1.0).astype(F32).reshape(1)
    past = page_table.shape[1] * page
    groups = 2 * nh_b
    slope_col = jnp.repeat(slopes, 2 * nt).reshape(nt * groups, 1)
    qpos_col = (past + jnp.tile(jnp.arange(nt, dtype=jnp.int32), groups)).reshape(nt * groups, 1)
    pad_rows = lambda a, r: jnp.pad(a.reshape(nseq, nt, w), ((0, 0), (0, r - nt), (0, 0)))
    q_rep = jnp.tile(qbs.reshape(nseq, nt, w), (1, SUBLANES // nt, 1))
    k_t = cache_k.transpose(0, 2, 3, 4, 1).reshape(n_pool, w, page)
    v_rows = cache_v.reshape(n_pool, page * nh_b, dv_b)
    ap, as8 = _attn(qtp, kbp.reshape(b, t, w), vtp, slopes, score_bound, lvec,
                    q_rep, k_t, v_rows, page_table, pad_rows(kbs, 2 * SUBLANES), pad_rows(vbs, 2 * SUBLANES),
                    slope_col, qpos_col, nh_b, nt, lam_init)
    as_ = as8[:, :nt].reshape(nseq * nt, w)

    fin_w = (_tile(subln_b[j], nh_b), w_out_b[j].astype(BF16), dv_b, 1.0 - lam_init)
    yp = _final(ap.reshape(b * t, w), zbp, y1p, modb_p, t // tm_wide, tm_wide, *fin_w)
    ys = _final(as_, zbs, y1s, wide(modb_s), 1, tm_wide, *fin_w)

    kp = ktp.reshape(b, nh_b, 2, dqk_b, t).transpose(0, 4, 1, 2, 3)
    return (yp.reshape(b, t, d), ys.reshape(nseq, nt, d), state_p[None], state_s[None],
            kp, vp.reshape(b, t, nh_b, dv_b),
            ks.reshape(nseq, nt, nh_b, 2, dqk_b), vs.reshape(nseq, nt, nh_b, dv_b))
```

```python
import functools
import math

import jax
import jax.numpy as jnp
from jax import lax
from jax.experimental import pallas as pl
from jax.experimental.pallas import tpu as pltpu

F32 = jnp.float32
BF16 = jnp.bfloat16
EPS = 1e-6
GATE_NORMALIZER = 16.0
LOG2E = math.log2(math.e)
LANES = 128
SUBLANES = 8
VMEM_LIMIT = 56 * 1024 * 1024

NT_DIMS = (((1,), (1,)), ((), ()))
TN_DIMS = (((0,), (0,)), ((), ()))


def _params(*sem):
    return pltpu.CompilerParams(dimension_semantics=sem, vmem_limit_bytes=VMEM_LIMIT)


def _silu(x):
    return x / (1.0 + jnp.exp(-x))


def _rms_rows(x):
    return x * lax.rsqrt(jnp.mean(x * x, axis=-1, keepdims=True) + EPS)


def _const_spec(shape):
    zeros = (0,) * len(shape)
    return pl.BlockSpec(shape, lambda *_: zeros, pipeline_mode=pl.Buffered(1))


def _ada_kernel(c_ref, w_ref, b_ref, head_ref, rest_ref):
    s = _silu(c_ref[...]).astype(BF16)
    mod = jnp.dot(s, w_ref[...].astype(BF16), preferred_element_type=F32) + b_ref[...]
    head = head_ref.shape[0]
    head_ref[...] = mod[0:head]
    rest_ref[...] = mod[head:]


def _ada(c, w, b, head, tn=512):
    n, d = c.shape
    m = w.shape[1]
    return pl.pallas_call(
        _ada_kernel,
        grid=(m // tn,),
        in_specs=[pl.BlockSpec((n, d), lambda j: (0, 0)),
                  pl.BlockSpec((d, tn), lambda j: (0, j)),
                  pl.BlockSpec((1, tn), lambda j: (0, j))],
        out_specs=[pl.BlockSpec((head, tn), lambda j: (0, j)), pl.BlockSpec((n - head, tn), lambda j: (0, j))],
        out_shape=[jax.ShapeDtypeStruct((head, m), F32), jax.ShapeDtypeStruct((n - head, m), F32)],
        compiler_params=_params("arbitrary"),
        name="ada_mod",
    )(c, w, b.reshape(1, m))


def _mod_spec(mod3, tm, tiles_per_group):
    r = mod3.shape[1]
    w = mod3.shape[2]
    if r == 1:
        return pl.BlockSpec((None, 1, w), lambda i: (i // tiles_per_group, 0, 0))
    return pl.BlockSpec((None, tm, w), lambda i: (i, 0, 0))


def _row_spec(tm, w):
    return pl.BlockSpec((tm, w), lambda i: (i, 0))


def _gla_project(x_ref, mod_ref, g_ref, w_ref, wg2_ref, bg_ref, *, ka, va, dk):
    x = x_ref[...]
    d = x.shape[-1]
    mod = mod_ref[...]
    h = _rms_rows(x) * g_ref[...] * (1.0 + mod[:, d:2 * d]) + mod[:, 0:d]
    p = jnp.dot(h.astype(BF16), w_ref[...], preferred_element_type=F32)
    glow = p[:, 2 * ka + 2 * va:]
    g2 = jnp.dot(glow.astype(BF16), wg2_ref[...], preferred_element_type=F32) + bg_ref[...]
    logsig = jnp.minimum(g2, 0.0) - jnp.log1p(jnp.exp(-jnp.abs(g2)))
    return (p[:, 0:ka] * (dk ** -0.5), p[:, ka:2 * ka], p[:, 2 * ka:2 * ka + va],
            p[:, 2 * ka + va:2 * ka + 2 * va], logsig * (1.0 / GATE_NORMALIZER))


def _gla_in_kernel(x_ref, mod_ref, g_ref, w_ref, wg2_ref, bg_ref,
                   q_ref, k_ref, v_ref, z_ref, lg_ref, *, ka, va, dk):
    q, k, v, z, lg = _gla_project(x_ref, mod_ref, g_ref, w_ref, wg2_ref, bg_ref, ka=ka, va=va, dk=dk)
    q_ref[...] = q
    k_ref[...] = k
    v_ref[...] = v
    z_ref[...] = z
    lg_ref[...] = lg


def _gla_in(x2, mod3, tiles_per_group, tm, g, w_pad, wg2_pad, bg, ka, va, dk):
    n, d = x2.shape
    wcols = w_pad.shape[1]
    kern = functools.partial(_gla_in_kernel, ka=ka, va=va, dk=dk)
    outs = [jax.ShapeDtypeStruct((n, ka), F32), jax.ShapeDtypeStruct((n, ka), F32),
            jax.ShapeDtypeStruct((n, va), F32), jax.ShapeDtypeStruct((n, va), F32),
            jax.ShapeDtypeStruct((n, ka), F32)]
    return pl.pallas_call(
        kern,
        grid=(n // tm,),
        in_specs=[_row_spec(tm, d), _mod_spec(mod3, tm, tiles_per_group),
                  _const_spec((1, d)), _const_spec((d, wcols)),
                  _const_spec(wg2_pad.shape), _const_spec((1, ka))],
        out_specs=[_row_spec(tm, ka), _row_spec(tm, ka), _row_spec(tm, va),
                   _row_spec(tm, va), _row_spec(tm, ka)],
        out_shape=outs,
        compiler_params=_params("arbitrary"),
        name="gla_in",
    )(x2, mod3, g.reshape(1, d), w_pad, wg2_pad, bg.reshape(1, ka))


def _cumsum_rows(tril_b, x):
    hi = x.astype(BF16)
    rest = x - hi.astype(F32)
    mid = rest.astype(BF16)
    lo = (rest - mid.astype(F32)).astype(BF16)
    dot = lambda part: jnp.dot(tril_b, part, preferred_element_type=F32)
    return dot(hi) + dot(mid) + dot(lo)


def _level_masks(row, col, max_block):
    levels = []
    n = 2
    while n <= max_block:
        sh = (n // 2).bit_length() - 1
        rb = row >> sh
        cb = col >> sh
        levels.append((n, jnp.where((cb & 1) == 0, rb - cb, 0) == 1))
        n *= 2
    return levels


def _midpoint_rows(cum, n, sub):
    c, dk = cum.shape
    if n == 2:
        return jnp.where((sub & 1) == 1, pltpu.roll(cum, 1, 0), cum)
    if n == 4:
        r = sub & 3
        return jnp.where(r == 0, pltpu.roll(cum, c - 1, 0),
                         jnp.where(r == 1, cum,
                                   jnp.where(r == 2, pltpu.roll(cum, 1, 0), pltpu.roll(cum, 2, 0))))
    half = n // 2
    return jnp.concatenate(
        [jnp.broadcast_to(cum[i * n + half - 1:i * n + half, :], (n, dk)) for i in range(c // n)], axis=0)


def _intra_operands(q, k, cum, levels, sub):
    ops = []
    for n, _ in levels:
        ref = _midpoint_rows(cum, n, sub)
        ops.append(((q * jnp.exp(cum - ref)).astype(BF16), (k * jnp.exp(ref - cum)).astype(BF16)))
    return ops, jnp.sum(q * k, axis=-1, keepdims=True)


def _intra_scores(ops, own, levels, eye):
    a = jnp.where(eye, own, 0.0)
    for (_, mask), (qs, ks) in zip(levels, ops):
        a = jnp.where(mask, lax.dot_general(qs, ks, NT_DIMS, preferred_element_type=F32), a)
    return a


def _gla_prompt_kernel(x_ref, mod_ref, g_ref, w_ref, wg2_ref, bg_ref, o_ref, z_ref, s_ref, st_scr,
                       *, c, nc, nh, dk, dv):
    ci = pl.program_id(1)

    @pl.when(ci == 0)
    def _():
        st_scr[...] = jnp.zeros_like(st_scr)

    q, k, v, z, lg = _gla_project(
        x_ref, mod_ref, g_ref, w_ref, wg2_ref, bg_ref, ka=nh * dk, va=nh * dv, dk=dk)
    z_ref[...] = z.astype(z_ref.dtype)
    row = lax.broadcasted_iota(jnp.int32, (c, c), 0)
    col = lax.broadcasted_iota(jnp.int32, (c, c), 1)
    sub = lax.broadcasted_iota(jnp.int32, (c, dk), 0)
    tril_b = jnp.where(col <= row, 1.0, 0.0).astype(BF16)
    levels = _level_masks(row, col, c)
    pairs = [(j, h) for j in range(nc) for h in range(nh)]
    rows = [slice(j * c, (j + 1) * c) for j in range(nc)]
    sk = [slice(h * dk, (h + 1) * dk) for h in range(nh)]
    sv = [slice(h * dv, (h + 1) * dv) for h in range(nh)]
    cums = [_cumsum_rows(tril_b, lg[rows[j], :]) for j in range(nc)]
    prep = {}
    for j, h in pairs:
        qh, kh, cum = q[rows[j], sk[h]], k[rows[j], sk[h]], cums[j][:, sk[h]]
        last = cum[c - 1:c, :]
        ops, own = _intra_operands(qh, kh, cum, levels, sub)
        prep[j, h] = (ops, own, (qh * jnp.exp(cum)).astype(BF16),
                      (kh * jnp.exp(last - cum)).astype(BF16), jnp.exp(last))
    local = {}
    for j, h in pairs:
        ops, own, _, kd, _ = prep[j, h]
        vb = v[rows[j], sv[h]].astype(BF16)
        a = _intra_scores(ops, own, levels, row == col)
        local[j, h] = (jnp.dot(a.astype(BF16), vb, preferred_element_type=F32),
                       lax.dot_general(vb, kd, TN_DIMS, preferred_element_type=F32))
    for h in range(nh):
        st = st_scr[h]
        for j in range(nc):
            o_intra, update = local[j, h]
            _, _, qe, _, decay = prep[j, h]
            o_ref[rows[j], sv[h]] = (o_intra + lax.dot_general(
                qe, st.astype(BF16), NT_DIMS, preferred_element_type=F32)).astype(o_ref.dtype)
            st = decay * st + update
        st_scr[h] = st

    @pl.when(ci == pl.num_programs(1) - 1)
    def _():
        for h in range(nh):
            s_ref[h] = st_scr[h].T


def _gla_prompt(x2, mod3, g, w_pad, wg2_pad, bg, b, t, nh, dk, dv, c=64, chunks_per_step=4):
    n, d = x2.shape
    rows = c * chunks_per_step
    assert t % rows == 0 and mod3.shape[1] == 1
    steps = t // rows
    ka, va = nh * dk, nh * dv
    kern = functools.partial(_gla_prompt_kernel, c=c, nc=chunks_per_step, nh=nh, dk=dk, dv=dv)
    row = lambda w: pl.BlockSpec((rows, w), lambda bi, ci: (bi * steps + ci, 0))
    return pl.pallas_call(
        kern,
        grid=(b, steps),
        in_specs=[row(d), pl.BlockSpec((None, 1, mod3.shape[2]), lambda bi, ci: (bi, 0, 0)),
                  _const_spec((1, d)), _const_spec(w_pad.shape), _const_spec(wg2_pad.shape),
                  _const_spec((1, ka))],
        out_specs=[row(va), row(va),
                   pl.BlockSpec((None, nh, dk, dv), lambda bi, ci: (bi, 0, 0, 0))],
        out_shape=[jax.ShapeDtypeStruct((n, va), BF16), jax.ShapeDtypeStruct((n, va), BF16),
                   jax.ShapeDtypeStruct((b, nh, dk, dv), F32)],
        scratch_shapes=[pltpu.VMEM((nh, dv, dk), F32)],
        compiler_params=_params("arbitrary", "arbitrary"),
        name="gla_prompt",
    )(x2, mod3, g.reshape(1, d), w_pad, wg2_pad, bg.reshape(1, ka))


def _gla_sample_kernel(q_ref, k_ref, v_ref, lg_ref, s0_ref, o_ref, s_ref, *, g, nh, dk, dv):
    r = g * SUBLANES
    row = lax.broadcasted_iota(jnp.int32, (r, r), 0)
    col = lax.broadcasted_iota(jnp.int32, (r, r), 1)
    sub = lax.broadcasted_iota(jnp.int32, (r, dk), 0)
    same_seq = (row >> 3) == (col >> 3)
    cum_all = _cumsum_rows(jnp.where(same_seq & (col <= row), 1.0, 0.0).astype(BF16), lg_ref[...])
    levels = _level_masks(row, col, SUBLANES)
    for h in range(nh):
        sk = slice(h * dk, (h + 1) * dk)
        sv = slice(h * dv, (h + 1) * dv)
        qh = q_ref[:, sk]
        kh = k_ref[:, sk]
        vh = v_ref[:, sv]
        cum = cum_all[:, sk]
        ops, own = _intra_operands(qh, kh, cum, levels, sub)
        a = _intra_scores(ops, own, levels, row == col)
        o_intra = jnp.dot(a.astype(BF16), vh.astype(BF16), preferred_element_type=F32)
        qe = qh * jnp.exp(cum)
        first_row = lax.broadcasted_iota(jnp.int32, (SUBLANES, dk), 0) == 0
        ones = jnp.ones((SUBLANES, LANES), F32)
        for i in range(g):
            rs = slice(i * SUBLANES, (i + 1) * SUBLANES)
            st = s0_ref[i, h]
            o_ref[rs, sv] = o_intra[rs] + jnp.dot(
                qe[rs].astype(BF16), st.astype(BF16), preferred_element_type=F32)
            last = cum[(i + 1) * SUBLANES - 1:(i + 1) * SUBLANES, :]
            kd = (kh[rs] * jnp.exp(last - cum[rs])).astype(BF16)
            decay_row = jnp.where(first_row, jnp.broadcast_to(jnp.exp(last), (SUBLANES, dk)), 0.0)
            decay = lax.dot_general(decay_row, ones, TN_DIMS, precision=lax.Precision.HIGHEST,
                                    preferred_element_type=F32)
            s_ref[i, h] = jnp.concatenate([decay] * (dv // LANES), axis=1) * st + lax.dot_general(
                kd, vh[rs].astype(BF16), TN_DIMS, preferred_element_type=F32)


def _gla_sample(q, k, v, lg, s0, nh, dk, dv, g=8):
    nseq = s0.shape[0]
    r = g * SUBLANES
    kern = functools.partial(_gla_sample_kernel, g=g, nh=nh, dk=dk, dv=dv)
    st_spec = pl.BlockSpec((g, nh, dk, dv), lambda i: (i, 0, 0, 0))
    return pl.pallas_call(
        kern,
        grid=(nseq // g,),
        in_specs=[_row_spec(r, nh * dk), _row_spec(r, nh * dk), _row_spec(r, nh * dv),
                  _row_spec(r, nh * dk), st_spec],
        out_specs=[_row_spec(r, nh * dv), st_spec],
        out_shape=[jax.ShapeDtypeStruct((nseq * SUBLANES, nh * dv), F32),
                   jax.ShapeDtypeStruct(s0.shape, F32)],
        compiler_params=_params("arbitrary"),
        name="gla_sample",
    )(q, k, v, lg, s0)


def _rms_groups64(x, lane_lo):
    outs = []
    for j in range(x.shape[-1] // LANES):
        xj = x[:, j * LANES:(j + 1) * LANES]
        sq = xj * xj
        s0 = jnp.sum(jnp.where(lane_lo, sq, 0.0), axis=-1, keepdims=True)
        s1 = jnp.sum(jnp.where(lane_lo, 0.0, sq), axis=-1, keepdims=True)
        ms = jnp.where(lane_lo, s0, s1) * (2.0 / LANES)
        outs.append(xj * lax.rsqrt(ms + EPS))
    return jnp.concatenate(outs, axis=-1)


def _mid_kernel(o_ref, z_ref, x_ref, moda_ref, modkv_ref, modb_ref,
                onorm_ref, wout_ref, gkv_ref, wk_ref, wv_ref, gk_ref, gb_ref, winb_ref, gq_ref,
                y_ref, k_ref, v_ref, kb_ref, vb_ref, qb_ref, zb_ref, *, nh, dv, qk_scale, token_minor):
    d = x_ref.shape[-1]
    o = o_ref[...].astype(F32)
    on = jnp.concatenate([_rms_rows(o[:, h * dv:(h + 1) * dv]) for h in range(nh)], axis=-1)
    on = on * onorm_ref[...]
    u = (on * _silu(z_ref[...].astype(F32))).astype(BF16)
    out = jnp.dot(u, wout_ref[...], preferred_element_type=F32)
    y = x_ref[...] + moda_ref[:, 2 * d:3 * d] * out
    y_ref[...] = y
    yn = _rms_rows(y)
    lane_lo = lax.broadcasted_iota(jnp.int32, (1, LANES), 1) < (LANES // 2)

    modkv = modkv_ref[...]
    h2 = (yn * gkv_ref[...] * (1.0 + modkv[:, d:2 * d]) + modkv[:, 0:d]).astype(BF16)
    kk = _rms_groups64(jnp.dot(h2, wk_ref[...], preferred_element_type=F32), lane_lo) * gk_ref[...]
    k_ref[...] = kk.T if token_minor else kk
    kb_ref[...] = kk.astype(BF16)
    vv = jnp.dot(h2, wv_ref[...], preferred_element_type=F32)
    v_ref[...] = vv
    vb_ref[...] = (vv.T if token_minor else vv).astype(BF16)

    modb = modb_ref[...]
    h3 = (yn * gb_ref[...] * (1.0 + modb[:, d:2 * d]) + modb[:, 0:d]).astype(BF16)
    qz = jnp.dot(h3, winb_ref[...], preferred_element_type=F32)
    qb = qz.shape[-1] - zb_ref.shape[-1]
    qq = _rms_groups64(qz[:, 0:qb], lane_lo) * (gq_ref[...] * qk_scale)
    qb_ref[...] = (qq.T if token_minor else qq).astype(BF16)
    zb_ref[...] = qz[:, qb:].astype(zb_ref.dtype)


def _mid(o, z, x2, moda3, modkv3, modb3, tiles_per_group, tm,
         onorm_t, wout, gkv, wk, wv, gk_t, gb, winb, gq_t, nh, dv, qk_scale, token_minor):
    n, d = x2.shape
    qb = wk.shape[1]
    zb = winb.shape[1] - qb
    kern = functools.partial(_mid_kernel, nh=nh, dv=dv, qk_scale=qk_scale, token_minor=token_minor)
    ms = lambda m: _mod_spec(m, tm, tiles_per_group)
    if token_minor:
        groups = n // (tm * tiles_per_group)
        t = tm * tiles_per_group
        feat = lambda w, dt: jax.ShapeDtypeStruct((groups, w, t), dt)
        feat_spec = lambda w: pl.BlockSpec(
            (None, w, tm), lambda i: (i // tiles_per_group, 0, i % tiles_per_group))
    else:
        feat = lambda w, dt: jax.ShapeDtypeStruct((n, w), dt)
        feat_spec = lambda w: _row_spec(tm, w)
    outs = [jax.ShapeDtypeStruct((n, d), F32), feat(qb, F32),
            jax.ShapeDtypeStruct((n, zb), F32), jax.ShapeDtypeStruct((n, qb), BF16),
            feat(zb, BF16), feat(qb, BF16),
            jax.ShapeDtypeStruct((n, zb), BF16)]
    out_specs = [_row_spec(tm, d), feat_spec(qb), _row_spec(tm, zb), _row_spec(tm, qb),
                 feat_spec(zb), feat_spec(qb), _row_spec(tm, zb)]
    return pl.pallas_call(
        kern,
        grid=(n // tm,),
        in_specs=[_row_spec(tm, o.shape[1]), _row_spec(tm, z.shape[1]), _row_spec(tm, d),
                  ms(moda3), ms(modkv3), ms(modb3),
                  _const_spec(onorm_t.shape), _const_spec(wout.shape), _const_spec(gkv.shape),
                  _const_spec(wk.shape), _const_spec(wv.shape), _const_spec(gk_t.shape),
                  _const_spec(gb.shape), _const_spec(winb.shape), _const_spec(gq_t.shape)],
        out_specs=out_specs,
        out_shape=outs,
        compiler_params=_params("arbitrary"),
        name="mid_proj",
    )(o, z, x2, moda3, modkv3, modb3, onorm_t, wout, gkv, wk, wv, gk_t, gb, winb, gq_t)


def _lambda_full(l_ref, lam_init):
    lv = l_ref[...]
    s1 = jnp.sum(lv[0:1] * lv[1:2], axis=-1, keepdims=True)
    s2 = jnp.sum(lv[2:3] * lv[3:4], axis=-1, keepdims=True)
    return jnp.exp(s1) - jnp.exp(s2) + lam_init


MAX_STATIC_SHIFT = 60.0


def _attn_prompt_body(hg, qi, slope_ref, bound_ref, l_ref, qt_ref, k_ref, vt_ref, o_ref, *, tq, tk, hps, lam_init):
    half = LANES // 2
    heads = range(hps)
    cols = [slice(hh * LANES, (hh + 1) * LANES) for hh in heads]
    zeros = jnp.zeros((half, tq), BF16)
    krow = lax.broadcasted_iota(jnp.int32, (tk, LANES), 0).astype(F32)
    ones_rows = jnp.ones((2 * SUBLANES, tk), BF16)
    slopes, qqs = [], []
    for hh in heads:
        slopes.append(slope_ref[hg * hps + hh])
        qt = qt_ref[cols[hh], :]
        qqs.append(jnp.concatenate([jnp.concatenate([qt[0:half], zeros], axis=0),
                                    jnp.concatenate([zeros, qt[half:LANES]], axis=0)], axis=1))

    def scores(j, causal):
        start = pl.multiple_of(j * tk, tk)
        offset = (j * tk - qi * tq).astype(F32)
        out = []
        for hh in heads:
            key_bias = slopes[hh] * (krow + offset)
            s = jnp.dot(k_ref[pl.ds(start, tk), cols[hh]], qqs[hh], preferred_element_type=F32)
            out.append(s + jnp.concatenate([key_bias] * (2 * tq // LANES), axis=1))
        if causal:
            kk = lax.broadcasted_iota(jnp.int32, (tk, 2 * tq), 0)
            qpos = lax.broadcasted_iota(jnp.int32, (tk, 2 * tq), 1)
            keep = kk <= jnp.where(qpos >= tq, qpos - tq, qpos)
            out = [jnp.where(keep, s, -jnp.inf) for s in out]
        return start, out

    def weighted_values(start, probs):
        return [jnp.dot(jnp.concatenate([vt_ref[cols[hh], pl.ds(start, tk)], ones_rows], axis=0), probs[hh],
                        preferred_element_type=F32) for hh in heads]

    def tile_fixed(j, accs, causal, shifts):
        start, ss = scores(j, causal)
        pvs = weighted_values(start, [jnp.exp2(ss[hh] - shifts[hh]).astype(BF16) for hh in heads])
        return tuple(accs[hh] + pvs[hh] for hh in heads)

    def tile_online(j, carries, causal):
        start, ss = scores(j, causal)
        stats, probs = [], []
        for hh in heads:
            m = carries[hh][0]
            m_new = jnp.maximum(m, jnp.max(ss[hh], axis=0, keepdims=True))
            stats.append((m_new, jnp.exp2(m - m_new)))
            probs.append(jnp.exp2(ss[hh] - m_new).astype(BF16))
        pvs = weighted_values(start, probs)
        return tuple((stats[hh][0], stats[hh][1] * carries[hh][1] + pvs[hh]) for hh in heads)

    def write(accs):
        lam = _lambda_full(l_ref, lam_init)
        for hh in heads:
            r = accs[hh][0:LANES] * (1.0 / accs[hh][LANES:LANES + 1])
            o_ref[:, cols[hh]] = (r[:, 0:tq] - lam * r[:, tq:2 * tq]).T.astype(o_ref.dtype)

    bound = bound_ref[0]
    zero_acc = tuple(jnp.zeros((LANES + 2 * SUBLANES, 2 * tq), F32) for _ in heads)

    @pl.when(bound <= MAX_STATIC_SHIFT)
    def _():
        qidx = lax.broadcasted_iota(jnp.int32, (1, 2 * tq), 1)
        qidx = jnp.where(qidx >= tq, qidx - tq, qidx).astype(F32)
        shifts = [bound + slopes[hh] * qidx for hh in heads]
        accs = lax.fori_loop(0, qi, lambda j, a: tile_fixed(j, a, False, shifts), zero_acc)
        write(tile_fixed(qi, accs, True, shifts))

    @pl.when(bound > MAX_STATIC_SHIFT)
    def _():
        init = tuple((jnp.full((1, 2 * tq), -jnp.inf, F32), zero_acc[hh]) for hh in heads)
        carries = lax.fori_loop(0, qi, lambda j, c: tile_online(j, c, False), init)
        write([c[1] for c in tile_online(qi, carries, True)])


def _attn_sample_body(l_ref, slope_ref, qpos_ref, q_ref, k_refs, v_refs, kn_ref, vn_ref, o_ref,
                      *, nh, nt, page, lam_init):
    n_pages = len(k_refs)
    past = n_pages * page
    w = q_ref.shape[-1]
    rows = 2 * nh * nt
    hr = 2 * nt
    dv = w // nh

    q = q_ref[...].astype(F32)
    qt = jnp.concatenate([q] * (rows // SUBLANES), axis=0)
    lane_grp = lax.broadcasted_iota(jnp.int32, (rows, w), 1) // (LANES // 2)
    row_grp = lax.broadcasted_iota(jnp.int32, (rows, w), 0) // nt
    qbd = jnp.where(lane_grp == row_grp, qt, 0.0).astype(BF16)

    s_past = jnp.concatenate(
        [jnp.dot(qbd, k_refs[i][...].astype(BF16), preferred_element_type=F32) for i in range(n_pages)],
        axis=1)
    kpos = lax.broadcasted_iota(jnp.int32, (1, past), 1)
    s_past = s_past - slope_ref[...] * (qpos_ref[...] - kpos).astype(F32)
    idx = lax.broadcasted_iota(jnp.int32, (1, kn_ref.shape[0]), 1)
    dist = qpos_ref[...] - (past + idx)
    s_new = lax.dot_general(qbd, kn_ref[...], NT_DIMS, preferred_element_type=F32)
    s_new = jnp.where((idx < nt) & (dist >= 0), s_new - slope_ref[...] * dist.astype(F32), -jnp.inf)

    m = jnp.maximum(jnp.max(s_past, axis=-1, keepdims=True), jnp.max(s_new, axis=-1, keepdims=True))
    pr_past = jnp.exp2(s_past - m)
    pr_new = jnp.exp2(s_new - m)
    denom = jnp.sum(pr_past, axis=-1, keepdims=True) + jnp.sum(pr_new, axis=-1, keepdims=True)
    head_values = [
        jnp.concatenate([v_refs[i][pl.ds(h, page, stride=nh), :].astype(BF16) for i in range(n_pages)], axis=0)
        for h in range(nh)]
    pv = jnp.concatenate(
        [jnp.dot(pr_past[h * hr:(h + 1) * hr].astype(BF16), head_values[h], preferred_element_type=F32)
         for h in range(nh)], axis=0)
    pv_new = jnp.dot(pr_new.astype(BF16), vn_ref[...], preferred_element_type=F32)
    own_new = jnp.concatenate([pv_new[h * hr:(h + 1) * hr, h * dv:(h + 1) * dv] for h in range(nh)], axis=0)
    r = (pv + own_new) / denom
    diff = r - _lambda_full(l_ref, lam_init) * pltpu.roll(r, rows - nt, 0)
    for h in range(nh):
        o_ref[:, h * dv:(h + 1) * dv] = diff[h * hr:(h + 1) * hr]


def _attn_kernel(pt_ref, slope_ref, bound_ref, l_ref, qt_ref, k_ref, vt_ref,
                 slope_col_ref, qpos_ref, q_ref, *refs,
                 n_pages, prompt_steps, sample_steps, steps_per_batch, q_tiles, prompt_args, sample_args):
    del pt_ref
    k_refs, v_refs = refs[0:n_pages], refs[n_pages:2 * n_pages]
    kn_ref, vn_ref, op_ref, os_ref = refs[2 * n_pages:]
    i = pl.program_id(0)

    def prompt_part():
        within = i % steps_per_batch
        _attn_prompt_body(within // q_tiles, within % q_tiles, slope_ref, bound_ref, l_ref,
                          qt_ref, k_ref, vt_ref, op_ref, **prompt_args)

    def sample_part():
        _attn_sample_body(l_ref, slope_col_ref, qpos_ref, q_ref, k_refs, v_refs, kn_ref, vn_ref,
                          os_ref, **sample_args)

    if prompt_steps < sample_steps:
        pl.when(i < prompt_steps)(prompt_part)
    else:
        prompt_part()
    if sample_steps < prompt_steps:
        pl.when(i < sample_steps)(sample_part)
    else:
        sample_part()


def _attn(qt, kb, vt, slopes, bound, lvec, q8, k_t, v_rows, page_table, kn, vn, slope_col, qpos_col,
          nh, nt, lam_init, tq=512, hps=2):
    b, t, w = kb.shape
    nseq = q8.shape[0]
    n_pool, _, page = k_t.shape
    dv = v_rows.shape[-1]
    n_pages = page_table.shape[1]
    nrows = kn.shape[1]
    rows = 2 * nh * nt
    assert 2 * nt == SUBLANES and nh % hps == 0 and t % tq == 0
    gw = hps * LANES
    q_tiles = t // tq
    steps_per_batch = (nh // hps) * q_tiles
    prompt_steps = b * steps_per_batch
    steps = max(prompt_steps, nseq)

    def prompt_index(i):
        i = jnp.minimum(i, prompt_steps - 1)
        within = i % steps_per_batch
        return i // steps_per_batch, within // q_tiles, within % q_tiles

    def seq_index(i):
        return jnp.minimum(i, nseq - 1)

    def prompt_spec(shape, pick):
        return pl.BlockSpec(shape, lambda i, pt: pick(*prompt_index(i)))

    k_specs = [pl.BlockSpec((None, w, page), lambda i, pt, p=p: (pt[seq_index(i), p], 0, 0))
               for p in range(n_pages)]
    v_specs = [pl.BlockSpec((None, page * nh, dv), lambda i, pt, p=p: (pt[seq_index(i), p], 0, 0))
               for p in range(n_pages)]
    seq_spec = lambda r: pl.BlockSpec((None, r, w), lambda i, pt: (seq_index(i), 0, 0))
    const = lambda shape: pl.BlockSpec(shape, lambda i, pt: (0,) * len(shape))
    smem = pl.BlockSpec(memory_space=pltpu.SMEM)
    kern = functools.partial(
        _attn_kernel, n_pages=n_pages, prompt_steps=prompt_steps, sample_steps=nseq,
        steps_per_batch=steps_per_batch, q_tiles=q_tiles,
        prompt_args=dict(tq=tq, tk=tq, hps=hps, lam_init=lam_init),
        sample_args=dict(nh=nh, nt=nt, page=page, lam_init=lam_init))
    grid_spec = pltpu.PrefetchScalarGridSpec(
        num_scalar_prefetch=1,
        grid=(steps,),
        in_specs=[smem, smem, const(lvec.shape),
                  prompt_spec((None, gw, tq), lambda bi, hi, qi: (bi, hi, qi)),
                  prompt_spec((None, t, gw), lambda bi, hi, qi: (bi, 0, hi)),
                  prompt_spec((None, gw, t), lambda bi, hi, qi: (bi, hi, 0)),
                  const(slope_col.shape), const(qpos_col.shape), seq_spec(SUBLANES)]
        + k_specs + v_specs + [seq_spec(nrows), seq_spec(nrows)],
        out_specs=[prompt_spec((None, tq, gw), lambda bi, hi, qi: (bi, qi, hi)), seq_spec(SUBLANES)],
    )
    return pl.pallas_call(
        kern,
        grid_spec=grid_spec,
        out_shape=[jax.ShapeDtypeStruct((b, t, w), BF16), jax.ShapeDtypeStruct((nseq, SUBLANES, w), F32)],
        compiler_params=_params("arbitrary"),
        name="attn",
    )(page_table, slopes, bound, lvec, qt, kb, vt, slope_col, qpos_col, q8,
      *([k_t] * n_pages), *([v_rows] * n_pages), kn, vn)


def _final_kernel(o_ref, z_ref, y_ref, modb_ref, gsub_ref, wout_ref, out_ref, *, dvb, post_scale):
    d = y_ref.shape[-1]
    o = o_ref[...].astype(F32)
    on = jnp.concatenate([_rms_rows(o[:, h * dvb:(h + 1) * dvb]) for h in range(o.shape[-1] // dvb)], axis=-1)
    on = on * gsub_ref[...] * post_scale
    u = (on * _silu(z_ref[...].astype(F32))).astype(BF16)
    out = jnp.dot(u, wout_ref[...], preferred_element_type=F32)
    out_ref[...] = y_ref[...] + modb_ref[:, 2 * d:3 * d] * out


def _final(o, z, y, modb3, tiles_per_group, tm, gsub_t, wout, dvb, post_scale):
    n, d = y.shape
    kern = functools.partial(_final_kernel, dvb=dvb, post_scale=post_scale)
    return pl.pallas_call(
        kern,
        grid=(n // tm,),
        in_specs=[_row_spec(tm, o.shape[1]), _row_spec(tm, z.shape[1]), _row_spec(tm, d),
                  _mod_spec(modb3, tm, tiles_per_group), _const_spec(gsub_t.shape), _const_spec(wout.shape)],
        out_specs=_row_spec(tm, d),
        out_shape=jax.ShapeDtypeStruct((n, d), F32),
        compiler_params=_params("arbitrary"),
        name="final_proj",
    )(o, z, y, modb3, gsub_t, wout)


def _tile(vec, reps):
    return jnp.tile(vec.astype(F32), reps).reshape(1, -1)


def kernel(x_prompt, x_sample, c_prompt, c_sample, state_gla, cache_k, cache_v, page_table, norm_a, ada_w_a, ada_b_a, w_in_a, w_g2_a, b_g_a, onorm_a, w_out_a, norm_kv, ada_w_kv, ada_b_kv, w_k, w_v, g_k, norm_b, ada_w_b, ada_b_b, w_in_b, g_q, lambda_q1, lambda_k1, lambda_q2, lambda_k2, subln_b, w_out_b):
    b, t, d = x_prompt.shape
    nseq, nt, _ = x_sample.shape
    n_a, _, nh_a, dk_a, dv_a = state_gla.shape
    n_b = norm_b.shape[0]
    ka, va = nh_a * dk_a, nh_a * dv_a
    nh_b, dv_b = cache_v.shape[2], cache_v.shape[3]
    dqk_b = cache_k.shape[4]
    n_pool, page = cache_k.shape[0], cache_k.shape[1]
    tm = 256
    tm_wide = 512
    assert n_a == 1 and n_b == 1, "one GLA layer and one differential-attention layer"
    assert nt <= SUBLANES and t % tm_wide == 0 and (nseq * nt) % tm_wide == 0

    head = -(-b // SUBLANES) * SUBLANES
    c_all = jnp.concatenate([jnp.pad(c_prompt, ((0, head - b), (0, 0))), jnp.repeat(c_sample, nt, axis=0)], axis=0)
    xp = x_prompt.reshape(b * t, d)
    xs = x_sample.reshape(nseq * nt, d)
    tiles_p = t // tm

    def split_mod(mods):
        mod_head, mod_rest = mods
        return mod_head[:b].reshape(b, 1, -1), mod_rest.reshape((nseq * nt) // tm, tm, -1)

    wide = lambda ms: ms.reshape(-1, tm_wide, ms.shape[-1])

    pad8 = lambda a: jnp.pad(a.reshape(nseq, nt, -1), ((0, 0), (0, SUBLANES - nt), (0, 0)))

    l = 0
    moda_p, moda_s = split_mod(_ada(c_all, ada_w_a[l], ada_b_a[l], head))
    rank = w_g2_a.shape[1]
    w_in_pad = jnp.pad(w_in_a[l], ((0, 0), (0, LANES - rank))).astype(BF16)
    wg2_pad = jnp.pad(w_g2_a[l], ((0, LANES - rank), (0, 0))).astype(BF16)
    gla_args = (norm_a[l], w_in_pad, wg2_pad, b_g_a[l], ka, va, dk_a)
    op, zp, state_p = _gla_prompt(xp, moda_p, norm_a[l], w_in_pad, wg2_pad, b_g_a[l], b, t, nh_a, dk_a, dv_a)
    qs, ks_, vs_, zs, lgs = _gla_in(xs, wide(moda_s), 1, tm_wide, *gla_args)
    os8, state_s = _gla_sample(pad8(qs).reshape(-1, ka), pad8(ks_).reshape(-1, ka),
                               pad8(vs_).reshape(-1, va), pad8(lgs).reshape(-1, ka),
                               state_gla[l], nh_a, dk_a, dv_a)
    os_ = os8.reshape(nseq, SUBLANES, va)[:, :nt].reshape(nseq * nt, va)

    j = 0
    lam_init = 0.8 - 0.6 * math.exp(-0.3 * (n_a + j))
    modkv_p, modkv_s = split_mod(_ada(c_all, ada_w_kv, ada_b_kv, head))
    modb_p, modb_s = split_mod(_ada(c_all, ada_w_b[j], ada_b_b[j], head))
    mid_w = (_tile(onorm_a[l], nh_a), w_out_a[l].astype(BF16), norm_kv.reshape(1, d),
             w_k.astype(BF16), w_v.astype(BF16), _tile(g_k, 2 * nh_b), norm_b[j].reshape(1, d),
             w_in_b[j].astype(BF16), _tile(g_q[j], 2 * nh_b), nh_a, dv_a, dqk_b ** -0.5 * LOG2E)
    y1p, ktp, vp, kbp, vtp, qtp, zbp = _mid(
        op, zp, xp, moda_p, modkv_p, modb_p, tiles_p, tm, *mid_w, token_minor=True)
    y1s, ks, vs, kbs, vbs, qbs, zbs = _mid(
        os_, zs, xs, moda_s, modkv_s, modb_s, 1, tm, *mid_w, token_minor=False)

    w = nh_b * dv_b
    lvec = jnp.stack([lambda_q1[j], lambda_k1[j], lambda_q2[j], lambda_k2[j]]).astype(F32)
    slopes = 2.0 ** (-8.0 * jnp.arange(1, nh_b + 1, dtype=F32) / nh_b) * LOG2E
    gain_bound = jnp.max(jnp.abs(g_q[j])) * jnp.max(jnp.abs(g_k))
    score_bound = (1.02 * dqk_b ** 0.5 * LOG2E * gain_bound + 1.0).astype(F32).reshape(1)
    past = page_table.shape[1] * page
    groups = 2 * nh_b
    slope_col = jnp.repeat(slopes, 2 * nt).reshape(nt * groups, 1)
    qpos_col = (past + jnp.tile(jnp.arange(nt, dtype=jnp.int32), groups)).reshape(nt * groups, 1)
    pad_rows = lambda a, r: jnp.pad(a.reshape(nseq, nt, w), ((0, 0), (0, r - nt), (0, 0)))
    q_rep = jnp.tile(qbs.reshape(nseq, nt, w), (1, SUBLANES // nt, 1))
    k_t = cache_k.transpose(0, 2, 3, 4, 1).reshape(n_pool, w, page)
    v_rows = cache_v.reshape(n_pool, page * nh_b, dv_b)
    ap, as8 = _attn(qtp, kbp.reshape(b, t, w), vtp, slopes, score_bound, lvec,
                    q_rep, k_t, v_rows, page_table, pad_rows(kbs, 2 * SUBLANES), pad_rows(vbs, 2 * SUBLANES),
                    slope_col, qpos_col, nh_b, nt, lam_init)
    as_ = as8[:, :nt].reshape(nseq * nt, w)

    fin_w = (_tile(subln_b[j], nh_b), w_out_b[j].astype(BF16), dv_b, 1.0 - lam_init)
    yp = _final(ap.reshape(b * t, w), zbp, y1p, modb_p, t // tm_wide, tm_wide, *fin_w)
    ys = _final(as_, zbs, y1s, wide(modb_s), 1, tm_wide, *fin_w)

    kp = ktp.reshape(b, nh_b, 2, dqk_b, t).transpose(0, 4, 1, 2, 3)
    return (yp.reshape(b, t, d), ys.reshape(nseq, nt, d), state_p[None], state_s[None],
            kp, vp.reshape(b, t, nh_b, dv_b),
            ks.reshape(nseq, nt, nh_b, 2, dqk_b), vs.reshape(nseq, nt, nh_b, dv_b))
```

```python
import functools
import math

import jax
import jax.numpy as jnp
from jax import lax
from jax.experimental import pallas as pl
from jax.experimental.pallas import tpu as pltpu

F32 = jnp.float32
BF16 = jnp.bfloat16
EPS = 1e-6
GATE_NORMALIZER = 16.0
LOG2E = math.log2(math.e)
LANES = 128
SUBLANES = 8
VMEM_LIMIT = 56 * 1024 * 1024

NT_DIMS = (((1,), (1,)), ((), ()))
TN_DIMS = (((0,), (0,)), ((), ()))


def _params(*sem):
    return pltpu.CompilerParams(dimension_semantics=sem, vmem_limit_bytes=VMEM_LIMIT)


def _silu(x):
    return x / (1.0 + jnp.exp(-x))


def _rms_rows(x):
    return x * lax.rsqrt(jnp.mean(x * x, axis=-1, keepdims=True) + EPS)


def _const_spec(shape):
    zeros = (0,) * len(shape)
    return pl.BlockSpec(shape, lambda *_: zeros, pipeline_mode=pl.Buffered(1))


def _ada_kernel(c_ref, w_ref, b_ref, head_ref, rest_ref):
    s = _silu(c_ref[...]).astype(BF16)
    mod = jnp.dot(s, w_ref[...].astype(BF16), preferred_element_type=F32) + b_ref[...]
    head = head_ref.shape[0]
    head_ref[...] = mod[0:head]
    rest_ref[...] = mod[head:]


def _ada(c, w, b, head, tn=1024):
    n, d = c.shape
    m = w.shape[1]
    return pl.pallas_call(
        _ada_kernel,
        grid=(m // tn,),
        in_specs=[pl.BlockSpec((n, d), lambda j: (0, 0)),
                  pl.BlockSpec((d, tn), lambda j: (0, j)),
                  pl.BlockSpec((1, tn), lambda j: (0, j))],
        out_specs=[pl.BlockSpec((head, tn), lambda j: (0, j)), pl.BlockSpec((n - head, tn), lambda j: (0, j))],
        out_shape=[jax.ShapeDtypeStruct((head, m), F32), jax.ShapeDtypeStruct((n - head, m), F32)],
        compiler_params=_params("arbitrary"),
        name="ada_mod",
    )(c, w, b.reshape(1, m))


def _mod_spec(mod3, tm, tiles_per_group):
    r = mod3.shape[1]
    w = mod3.shape[2]
    if r == 1:
        return pl.BlockSpec((None, 1, w), lambda i: (i // tiles_per_group, 0, 0))
    return pl.BlockSpec((None, tm, w), lambda i: (i, 0, 0))


def _row_spec(tm, w):
    return pl.BlockSpec((tm, w), lambda i: (i, 0))


def _gla_project(x_ref, mod_ref, g_ref, w_ref, wg2_ref, bg_ref, *, ka, va, dk):
    x = x_ref[...]
    d = x.shape[-1]
    mod = mod_ref[...]
    h = _rms_rows(x) * g_ref[...] * (1.0 + mod[:, d:2 * d]) + mod[:, 0:d]
    p = jnp.dot(h.astype(BF16), w_ref[...], preferred_element_type=F32)
    glow = p[:, 2 * ka + 2 * va:]
    g2 = jnp.dot(glow.astype(BF16), wg2_ref[...], preferred_element_type=F32) + bg_ref[...]
    logsig = jnp.minimum(g2, 0.0) - jnp.log1p(jnp.exp(-jnp.abs(g2)))
    return (p[:, 0:ka] * (dk ** -0.5), p[:, ka:2 * ka], p[:, 2 * ka:2 * ka + va],
            p[:, 2 * ka + va:2 * ka + 2 * va], logsig * (1.0 / GATE_NORMALIZER))


def _gla_in_kernel(x_ref, mod_ref, g_ref, w_ref, wg2_ref, bg_ref,
                   q_ref, k_ref, v_ref, z_ref, lg_ref, *, ka, va, dk):
    q, k, v, z, lg = _gla_project(x_ref, mod_ref, g_ref, w_ref, wg2_ref, bg_ref, ka=ka, va=va, dk=dk)
    q_ref[...] = q
    k_ref[...] = k
    v_ref[...] = v
    z_ref[...] = z
    lg_ref[...] = lg


def _gla_in(x2, mod3, tiles_per_group, tm, g, w_pad, wg2_pad, bg, ka, va, dk):
    n, d = x2.shape
    wcols = w_pad.shape[1]
    kern = functools.partial(_gla_in_kernel, ka=ka, va=va, dk=dk)
    outs = [jax.ShapeDtypeStruct((n, ka), F32), jax.ShapeDtypeStruct((n, ka), F32),
            jax.ShapeDtypeStruct((n, va), F32), jax.ShapeDtypeStruct((n, va), F32),
            jax.ShapeDtypeStruct((n, ka), F32)]
    return pl.pallas_call(
        kern,
        grid=(n // tm,),
        in_specs=[_row_spec(tm, d), _mod_spec(mod3, tm, tiles_per_group),
                  _const_spec((1, d)), _const_spec((d, wcols)),
                  _const_spec(wg2_pad.shape), _const_spec((1, ka))],
        out_specs=[_row_spec(tm, ka), _row_spec(tm, ka), _row_spec(tm, va),
                   _row_spec(tm, va), _row_spec(tm, ka)],
        out_shape=outs,
        compiler_params=_params("arbitrary"),
        name="gla_in",
    )(x2, mod3, g.reshape(1, d), w_pad, wg2_pad, bg.reshape(1, ka))


def _cumsum_rows(tril_b, x):
    hi = x.astype(BF16)
    rest = x - hi.astype(F32)
    mid = rest.astype(BF16)
    lo = (rest - mid.astype(F32)).astype(BF16)
    dot = lambda part: jnp.dot(tril_b, part, preferred_element_type=F32)
    return dot(hi) + dot(mid) + dot(lo)


def _level_masks(row, col, max_block):
    levels = []
    n = 2
    while n <= max_block:
        sh = (n // 2).bit_length() - 1
        rb = row >> sh
        cb = col >> sh
        levels.append((n, jnp.where((cb & 1) == 0, rb - cb, 0) == 1))
        n *= 2
    return levels


def _midpoint_rows(cum, n, sub):
    c, dk = cum.shape
    if n == 2:
        return jnp.where((sub & 1) == 1, pltpu.roll(cum, 1, 0), cum)
    if n == 4:
        r = sub & 3
        return jnp.where(r == 0, pltpu.roll(cum, c - 1, 0),
                         jnp.where(r == 1, cum,
                                   jnp.where(r == 2, pltpu.roll(cum, 1, 0), pltpu.roll(cum, 2, 0))))
    half = n // 2
    return jnp.concatenate(
        [jnp.broadcast_to(cum[i * n + half - 1:i * n + half, :], (n, dk)) for i in range(c // n)], axis=0)


def _intra_operands(q, k, cum, levels, sub):
    ops = []
    for n, _ in levels:
        ref = _midpoint_rows(cum, n, sub)
        ops.append(((q * jnp.exp(cum - ref)).astype(BF16), (k * jnp.exp(ref - cum)).astype(BF16)))
    return ops, jnp.sum(q * k, axis=-1, keepdims=True)


def _intra_scores(ops, own, levels, eye):
    a = jnp.where(eye, own, 0.0)
    for (_, mask), (qs, ks) in zip(levels, ops):
        a = jnp.where(mask, lax.dot_general(qs, ks, NT_DIMS, preferred_element_type=F32), a)
    return a


def _gla_prompt_kernel(x_ref, mod_ref, g_ref, w_ref, wg2_ref, bg_ref, o_ref, z_ref, s_ref, st_scr,
                       *, c, nc, nh, dk, dv):
    ci = pl.program_id(1)

    @pl.when(ci == 0)
    def _():
        st_scr[...] = jnp.zeros_like(st_scr)

    q, k, v, z, lg = _gla_project(
        x_ref, mod_ref, g_ref, w_ref, wg2_ref, bg_ref, ka=nh * dk, va=nh * dv, dk=dk)
    z_ref[...] = z.astype(z_ref.dtype)
    row = lax.broadcasted_iota(jnp.int32, (c, c), 0)
    col = lax.broadcasted_iota(jnp.int32, (c, c), 1)
    sub = lax.broadcasted_iota(jnp.int32, (c, dk), 0)
    tril_b = jnp.where(col <= row, 1.0, 0.0).astype(BF16)
    levels = _level_masks(row, col, c)
    pairs = [(j, h) for j in range(nc) for h in range(nh)]
    rows = [slice(j * c, (j + 1) * c) for j in range(nc)]
    sk = [slice(h * dk, (h + 1) * dk) for h in range(nh)]
    sv = [slice(h * dv, (h + 1) * dv) for h in range(nh)]
    cums = [_cumsum_rows(tril_b, lg[rows[j], :]) for j in range(nc)]
    prep = {}
    for j, h in pairs:
        qh, kh, cum = q[rows[j], sk[h]], k[rows[j], sk[h]], cums[j][:, sk[h]]
        last = cum[c - 1:c, :]
        ops, own = _intra_operands(qh, kh, cum, levels, sub)
        prep[j, h] = (ops, own, (qh * jnp.exp(cum)).astype(BF16),
                      (kh * jnp.exp(last - cum)).astype(BF16), jnp.exp(last))
    local = {}
    for j, h in pairs:
        ops, own, _, kd, _ = prep[j, h]
        vb = v[rows[j], sv[h]].astype(BF16)
        a = _intra_scores(ops, own, levels, row == col)
        local[j, h] = (jnp.dot(a.astype(BF16), vb, preferred_element_type=F32),
                       lax.dot_general(vb, kd, TN_DIMS, preferred_element_type=F32))
    for h in range(nh):
        st = st_scr[h]
        for j in range(nc):
            o_intra, update = local[j, h]
            _, _, qe, _, decay = prep[j, h]
            o_ref[rows[j], sv[h]] = (o_intra + lax.dot_general(
                qe, st.astype(BF16), NT_DIMS, preferred_element_type=F32)).astype(o_ref.dtype)
            st = decay * st + update
        st_scr[h] = st

    @pl.when(ci == pl.num_programs(1) - 1)
    def _():
        for h in range(nh):
            s_ref[h] = st_scr[h].T


def _gla_prompt(x2, mod3, g, w_pad, wg2_pad, bg, b, t, nh, dk, dv, c=64, chunks_per_step=4):
    n, d = x2.shape
    rows = c * chunks_per_step
    assert t % rows == 0 and mod3.shape[1] == 1
    steps = t // rows
    ka, va = nh * dk, nh * dv
    kern = functools.partial(_gla_prompt_kernel, c=c, nc=chunks_per_step, nh=nh, dk=dk, dv=dv)
    row = lambda w: pl.BlockSpec((rows, w), lambda bi, ci: (bi * steps + ci, 0))
    return pl.pallas_call(
        kern,
        grid=(b, steps),
        in_specs=[row(d), pl.BlockSpec((None, 1, mod3.shape[2]), lambda bi, ci: (bi, 0, 0)),
                  _const_spec((1, d)), _const_spec(w_pad.shape), _const_spec(wg2_pad.shape),
                  _const_spec((1, ka))],
        out_specs=[row(va), row(va),
                   pl.BlockSpec((None, nh, dk, dv), lambda bi, ci: (bi, 0, 0, 0))],
        out_shape=[jax.ShapeDtypeStruct((n, va), BF16), jax.ShapeDtypeStruct((n, va), BF16),
                   jax.ShapeDtypeStruct((b, nh, dk, dv), F32)],
        scratch_shapes=[pltpu.VMEM((nh, dv, dk), F32)],
        compiler_params=_params("arbitrary", "arbitrary"),
        name="gla_prompt",
    )(x2, mod3, g.reshape(1, d), w_pad, wg2_pad, bg.reshape(1, ka))


def _gla_sample_kernel(q_ref, k_ref, v_ref, lg_ref, s0_ref, o_ref, s_ref, *, g, nh, dk, dv):
    r = g * SUBLANES
    row = lax.broadcasted_iota(jnp.int32, (r, r), 0)
    col = lax.broadcasted_iota(jnp.int32, (r, r), 1)
    sub = lax.broadcasted_iota(jnp.int32, (r, dk), 0)
    same_seq = (row >> 3) == (col >> 3)
    cum_all = _cumsum_rows(jnp.where(same_seq & (col <= row), 1.0, 0.0).astype(BF16), lg_ref[...])
    levels = _level_masks(row, col, SUBLANES)
    for h in range(nh):
        sk = slice(h * dk, (h + 1) * dk)
        sv = slice(h * dv, (h + 1) * dv)
        qh = q_ref[:, sk]
        kh = k_ref[:, sk]
        vh = v_ref[:, sv]
        cum = cum_all[:, sk]
        ops, own = _intra_operands(qh, kh, cum, levels, sub)
        a = _intra_scores(ops, own, levels, row == col)
        o_intra = jnp.dot(a.astype(BF16), vh.astype(BF16), preferred_element_type=F32)
        qe = qh * jnp.exp(cum)
        first_row = lax.broadcasted_iota(jnp.int32, (SUBLANES, dk), 0) == 0
        ones = jnp.ones((SUBLANES, LANES), F32)
        for i in range(g):
            rs = slice(i * SUBLANES, (i + 1) * SUBLANES)
            st = s0_ref[i, h]
            o_ref[rs, sv] = o_intra[rs] + jnp.dot(
                qe[rs].astype(BF16), st.astype(BF16), preferred_element_type=F32)
            last = cum[(i + 1) * SUBLANES - 1:(i + 1) * SUBLANES, :]
            kd = (kh[rs] * jnp.exp(last - cum[rs])).astype(BF16)
            decay_row = jnp.where(first_row, jnp.broadcast_to(jnp.exp(last), (SUBLANES, dk)), 0.0)
            decay = lax.dot_general(decay_row, ones, TN_DIMS, precision=lax.Precision.HIGHEST,
                                    preferred_element_type=F32)
            s_ref[i, h] = jnp.concatenate([decay] * (dv // LANES), axis=1) * st + lax.dot_general(
                kd, vh[rs].astype(BF16), TN_DIMS, preferred_element_type=F32)


def _gla_sample(q, k, v, lg, s0, nh, dk, dv, g=8):
    nseq = s0.shape[0]
    r = g * SUBLANES
    kern = functools.partial(_gla_sample_kernel, g=g, nh=nh, dk=dk, dv=dv)
    st_spec = pl.BlockSpec((g, nh, dk, dv), lambda i: (i, 0, 0, 0))
    return pl.pallas_call(
        kern,
        grid=(nseq // g,),
        in_specs=[_row_spec(r, nh * dk), _row_spec(r, nh * dk), _row_spec(r, nh * dv),
                  _row_spec(r, nh * dk), st_spec],
        out_specs=[_row_spec(r, nh * dv), st_spec],
        out_shape=[jax.ShapeDtypeStruct((nseq * SUBLANES, nh * dv), F32),
                   jax.ShapeDtypeStruct(s0.shape, F32)],
        compiler_params=_params("arbitrary"),
        name="gla_sample",
    )(q, k, v, lg, s0)


def _rms_groups64(x, lane_lo):
    outs = []
    for j in range(x.shape[-1] // LANES):
        xj = x[:, j * LANES:(j + 1) * LANES]
        sq = xj * xj
        s0 = jnp.sum(jnp.where(lane_lo, sq, 0.0), axis=-1, keepdims=True)
        s1 = jnp.sum(jnp.where(lane_lo, 0.0, sq), axis=-1, keepdims=True)
        ms = jnp.where(lane_lo, s0, s1) * (2.0 / LANES)
        outs.append(xj * lax.rsqrt(ms + EPS))
    return jnp.concatenate(outs, axis=-1)


def _mid_kernel(o_ref, z_ref, x_ref, moda_ref, modkv_ref, modb_ref,
                onorm_ref, wout_ref, gkv_ref, wk_ref, wv_ref, gk_ref, gb_ref, winb_ref, gq_ref,
                y_ref, k_ref, v_ref, kb_ref, vb_ref, qb_ref, zb_ref, *, nh, dv, qk_scale, token_minor):
    d = x_ref.shape[-1]
    o = o_ref[...].astype(F32)
    on = jnp.concatenate([_rms_rows(o[:, h * dv:(h + 1) * dv]) for h in range(nh)], axis=-1)
    on = on * onorm_ref[...]
    u = (on * _silu(z_ref[...].astype(F32))).astype(BF16)
    out = jnp.dot(u, wout_ref[...], preferred_element_type=F32)
    y = x_ref[...] + moda_ref[:, 2 * d:3 * d] * out
    y_ref[...] = y
    yn = _rms_rows(y)
    lane_lo = lax.broadcasted_iota(jnp.int32, (1, LANES), 1) < (LANES // 2)

    modkv = modkv_ref[...]
    h2 = (yn * gkv_ref[...] * (1.0 + modkv[:, d:2 * d]) + modkv[:, 0:d]).astype(BF16)
    kk = _rms_groups64(jnp.dot(h2, wk_ref[...], preferred_element_type=F32), lane_lo) * gk_ref[...]
    k_ref[...] = kk.T if token_minor else kk
    kb_ref[...] = kk.astype(BF16)
    vv = jnp.dot(h2, wv_ref[...], preferred_element_type=F32)
    v_ref[...] = vv
    vb_ref[...] = (vv.T if token_minor else vv).astype(BF16)

    modb = modb_ref[...]
    h3 = (yn * gb_ref[...] * (1.0 + modb[:, d:2 * d]) + modb[:, 0:d]).astype(BF16)
    qz = jnp.dot(h3, winb_ref[...], preferred_element_type=F32)
    qb = qz.shape[-1] - zb_ref.shape[-1]
    qq = _rms_groups64(qz[:, 0:qb], lane_lo) * (gq_ref[...] * qk_scale)
    qb_ref[...] = (qq.T if token_minor else qq).astype(BF16)
    zb_ref[...] = qz[:, qb:].astype(zb_ref.dtype)


def _mid(o, z, x2, moda3, modkv3, modb3, tiles_per_group, tm,
         onorm_t, wout, gkv, wk, wv, gk_t, gb, winb, gq_t, nh, dv, qk_scale, token_minor):
    n, d = x2.shape
    qb = wk.shape[1]
    zb = winb.shape[1] - qb
    kern = functools.partial(_mid_kernel, nh=nh, dv=dv, qk_scale=qk_scale, token_minor=token_minor)
    ms = lambda m: _mod_spec(m, tm, tiles_per_group)
    if token_minor:
        groups = n // (tm * tiles_per_group)
        t = tm * tiles_per_group
        feat = lambda w, dt: jax.ShapeDtypeStruct((groups, w, t), dt)
        feat_spec = lambda w: pl.BlockSpec(
            (None, w, tm), lambda i: (i // tiles_per_group, 0, i % tiles_per_group))
    else:
        feat = lambda w, dt: jax.ShapeDtypeStruct((n, w), dt)
        feat_spec = lambda w: _row_spec(tm, w)
    outs = [jax.ShapeDtypeStruct((n, d), F32), feat(qb, F32),
            jax.ShapeDtypeStruct((n, zb), F32), jax.ShapeDtypeStruct((n, qb), BF16),
            feat(zb, BF16), feat(qb, BF16),
            jax.ShapeDtypeStruct((n, zb), BF16)]
    out_specs = [_row_spec(tm, d), feat_spec(qb), _row_spec(tm, zb), _row_spec(tm, qb),
                 feat_spec(zb), feat_spec(qb), _row_spec(tm, zb)]
    return pl.pallas_call(
        kern,
        grid=(n // tm,),
        in_specs=[_row_spec(tm, o.shape[1]), _row_spec(tm, z.shape[1]), _row_spec(tm, d),
                  ms(moda3), ms(modkv3), ms(modb3),
                  _const_spec(onorm_t.shape), _const_spec(wout.shape), _const_spec(gkv.shape),
                  _const_spec(wk.shape), _const_spec(wv.shape), _const_spec(gk_t.shape),
                  _const_spec(gb.shape), _const_spec(winb.shape), _const_spec(gq_t.shape)],
        out_specs=out_specs,
        out_shape=outs,
        compiler_params=_params("arbitrary"),
        name="mid_proj",
    )(o, z, x2, moda3, modkv3, modb3, onorm_t, wout, gkv, wk, wv, gk_t, gb, winb, gq_t)


def _lambda_full(l_ref, lam_init):
    lv = l_ref[...]
    s1 = jnp.sum(lv[0:1] * lv[1:2], axis=-1, keepdims=True)
    s2 = jnp.sum(lv[2:3] * lv[3:4], axis=-1, keepdims=True)
    return jnp.exp(s1) - jnp.exp(s2) + lam_init


MAX_STATIC_SHIFT = 60.0


def _attn_prompt_body(hg, qi, slope_ref, bound_ref, l_ref, qt_ref, k_ref, vt_ref, o_ref, *, tq, tk, hps, lam_init):
    half = LANES // 2
    heads = range(hps)
    cols = [slice(hh * LANES, (hh + 1) * LANES) for hh in heads]
    zeros = jnp.zeros((half, tq), BF16)
    krow = lax.broadcasted_iota(jnp.int32, (tk, LANES), 0).astype(F32)
    ones_rows = jnp.ones((2 * SUBLANES, tk), BF16)
    slopes, qqs = [], []
    for hh in heads:
        slopes.append(slope_ref[hg * hps + hh])
        qt = qt_ref[cols[hh], :]
        qqs.append(jnp.concatenate([jnp.concatenate([qt[0:half], zeros], axis=0),
                                    jnp.concatenate([zeros, qt[half:LANES]], axis=0)], axis=1))

    def scores(j, causal):
        start = pl.multiple_of(j * tk, tk)
        offset = (j * tk - qi * tq).astype(F32)
        out = []
        for hh in heads:
            key_bias = slopes[hh] * (krow + offset)
            s = jnp.dot(k_ref[pl.ds(start, tk), cols[hh]], qqs[hh], preferred_element_type=F32)
            out.append(s + jnp.concatenate([key_bias] * (2 * tq // LANES), axis=1))
        if causal:
            kk = lax.broadcasted_iota(jnp.int32, (tk, 2 * tq), 0)
            qpos = lax.broadcasted_iota(jnp.int32, (tk, 2 * tq), 1)
            keep = kk <= jnp.where(qpos >= tq, qpos - tq, qpos)
            out = [jnp.where(keep, s, -jnp.inf) for s in out]
        return start, out

    def weighted_values(start, probs):
        return [jnp.dot(jnp.concatenate([vt_ref[cols[hh], pl.ds(start, tk)], ones_rows], axis=0), probs[hh],
                        preferred_element_type=F32) for hh in heads]

    def tile_fixed(j, accs, causal, shifts):
        start, ss = scores(j, causal)
        pvs = weighted_values(start, [jnp.exp2(ss[hh] - shifts[hh]).astype(BF16) for hh in heads])
        return tuple(accs[hh] + pvs[hh] for hh in heads)

    def tile_online(j, carries, causal):
        start, ss = scores(j, causal)
        stats, probs = [], []
        for hh in heads:
            m = carries[hh][0]
            m_new = jnp.maximum(m, jnp.max(ss[hh], axis=0, keepdims=True))
            stats.append((m_new, jnp.exp2(m - m_new)))
            probs.append(jnp.exp2(ss[hh] - m_new).astype(BF16))
        pvs = weighted_values(start, probs)
        return tuple((stats[hh][0], stats[hh][1] * carries[hh][1] + pvs[hh]) for hh in heads)

    def write(accs):
        lam = _lambda_full(l_ref, lam_init)
        for hh in heads:
            r = accs[hh][0:LANES] * (1.0 / accs[hh][LANES:LANES + 1])
            o_ref[:, cols[hh]] = (r[:, 0:tq] - lam * r[:, tq:2 * tq]).T.astype(o_ref.dtype)

    bound = bound_ref[0]
    zero_acc = tuple(jnp.zeros((LANES + 2 * SUBLANES, 2 * tq), F32) for _ in heads)

    @pl.when(bound <= MAX_STATIC_SHIFT)
    def _():
        qidx = lax.broadcasted_iota(jnp.int32, (1, 2 * tq), 1)
        qidx = jnp.where(qidx >= tq, qidx - tq, qidx).astype(F32)
        shifts = [bound + slopes[hh] * qidx for hh in heads]
        accs = lax.fori_loop(0, qi, lambda j, a: tile_fixed(j, a, False, shifts), zero_acc)
        write(tile_fixed(qi, accs, True, shifts))

    @pl.when(jnp.logical_not(bound <= MAX_STATIC_SHIFT))
    def _():
        init = tuple((jnp.full((1, 2 * tq), -jnp.inf, F32), zero_acc[hh]) for hh in heads)
        carries = lax.fori_loop(0, qi, lambda j, c: tile_online(j, c, False), init)
        write([c[1] for c in tile_online(qi, carries, True)])


def _attn_sample_body(l_ref, slope_ref, qpos_ref, q_ref, k_refs, v_refs, kn_ref, vn_ref, o_ref,
                      *, nh, nt, page, lam_init):
    n_pages = len(k_refs)
    past = n_pages * page
    w = q_ref.shape[-1]
    rows = 2 * nh * nt
    hr = 2 * nt
    dv = w // nh

    q = q_ref[...].astype(F32)
    qt = jnp.concatenate([q] * (rows // SUBLANES), axis=0)
    lane_grp = lax.broadcasted_iota(jnp.int32, (rows, w), 1) // (LANES // 2)
    row_grp = lax.broadcasted_iota(jnp.int32, (rows, w), 0) // nt
    qbd = jnp.where(lane_grp == row_grp, qt, 0.0).astype(BF16)

    s_past = jnp.concatenate(
        [jnp.dot(qbd, k_refs[i][...].astype(BF16), preferred_element_type=F32) for i in range(n_pages)],
        axis=1)
    kpos = lax.broadcasted_iota(jnp.int32, (1, past), 1)
    s_past = s_past - slope_ref[...] * (qpos_ref[...] - kpos).astype(F32)
    idx = lax.broadcasted_iota(jnp.int32, (1, kn_ref.shape[0]), 1)
    dist = qpos_ref[...] - (past + idx)
    s_new = lax.dot_general(qbd, kn_ref[...], NT_DIMS, preferred_element_type=F32)
    s_new = jnp.where((idx < nt) & (dist >= 0), s_new - slope_ref[...] * dist.astype(F32), -jnp.inf)

    m = jnp.maximum(jnp.max(s_past, axis=-1, keepdims=True), jnp.max(s_new, axis=-1, keepdims=True))
    pr_past = jnp.exp2(s_past - m)
    pr_new = jnp.exp2(s_new - m)
    denom = jnp.sum(pr_past, axis=-1, keepdims=True) + jnp.sum(pr_new, axis=-1, keepdims=True)
    head_values = [
        jnp.concatenate([v_refs[i][pl.ds(h, page, stride=nh), :].astype(BF16) for i in range(n_pages)], axis=0)
        for h in range(nh)]
    pv = jnp.concatenate(
        [jnp.dot(pr_past[h * hr:(h + 1) * hr].astype(BF16), head_values[h], preferred_element_type=F32)
         for h in range(nh)], axis=0)
    pv_new = jnp.dot(pr_new.astype(BF16), vn_ref[...], preferred_element_type=F32)
    own_new = jnp.concatenate([pv_new[h * hr:(h + 1) * hr, h * dv:(h + 1) * dv] for h in range(nh)], axis=0)
    r = (pv + own_new) / denom
    diff = r - _lambda_full(l_ref, lam_init) * pltpu.roll(r, rows - nt, 0)
    for h in range(nh):
        o_ref[:, h * dv:(h + 1) * dv] = diff[h * hr:(h + 1) * hr]


def _attn_kernel(pt_ref, slope_ref, bound_ref, l_ref, qt_ref, k_ref, vt_ref,
                 slope_col_ref, qpos_ref, q_ref, *refs,
                 n_pages, prompt_steps, sample_steps, steps_per_batch, q_tiles, prompt_args, sample_args):
    del pt_ref
    k_refs, v_refs = refs[0:n_pages], refs[n_pages:2 * n_pages]
    kn_ref, vn_ref, op_ref, os_ref = refs[2 * n_pages:]
    i = pl.program_id(0)

    def prompt_part():
        within = i % steps_per_batch
        _attn_prompt_body(within // q_tiles, within % q_tiles, slope_ref, bound_ref, l_ref,
                          qt_ref, k_ref, vt_ref, op_ref, **prompt_args)

    def sample_part():
        _attn_sample_body(l_ref, slope_col_ref, qpos_ref, q_ref, k_refs, v_refs, kn_ref, vn_ref,
                          os_ref, **sample_args)

    if prompt_steps < sample_steps:
        pl.when(i < prompt_steps)(prompt_part)
    else:
        prompt_part()
    if sample_steps < prompt_steps:
        pl.when(i < sample_steps)(sample_part)
    else:
        sample_part()


def _attn(qt, kb, vt, slopes, bound, lvec, q8, k_t, v_rows, page_table, kn, vn, slope_col, qpos_col,
          nh, nt, lam_init, tq=512, hps=2):
    b, t, w = kb.shape
    nseq = q8.shape[0]
    n_pool, _, page = k_t.shape
    dv = v_rows.shape[-1]
    n_pages = page_table.shape[1]
    nrows = kn.shape[1]
    rows = 2 * nh * nt
    assert 2 * nt == SUBLANES and nh % hps == 0 and t % tq == 0
    gw = hps * LANES
    q_tiles = t // tq
    steps_per_batch = (nh // hps) * q_tiles
    prompt_steps = b * steps_per_batch
    steps = max(prompt_steps, nseq)

    def prompt_index(i):
        i = jnp.minimum(i, prompt_steps - 1)
        within = i % steps_per_batch
        return i // steps_per_batch, within // q_tiles, within % q_tiles

    def seq_index(i):
        return jnp.minimum(i, nseq - 1)

    def prompt_spec(shape, pick):
        return pl.BlockSpec(shape, lambda i, pt: pick(*prompt_index(i)))

    k_specs = [pl.BlockSpec((None, w, page), lambda i, pt, p=p: (pt[seq_index(i), p], 0, 0))
               for p in range(n_pages)]
    v_specs = [pl.BlockSpec((None, page * nh, dv), lambda i, pt, p=p: (pt[seq_index(i), p], 0, 0))
               for p in range(n_pages)]
    seq_spec = lambda r: pl.BlockSpec((None, r, w), lambda i, pt: (seq_index(i), 0, 0))
    const = lambda shape: pl.BlockSpec(shape, lambda i, pt: (0,) * len(shape))
    smem = pl.BlockSpec(memory_space=pltpu.SMEM)
    kern = functools.partial(
        _attn_kernel, n_pages=n_pages, prompt_steps=prompt_steps, sample_steps=nseq,
        steps_per_batch=steps_per_batch, q_tiles=q_tiles,
        prompt_args=dict(tq=tq, tk=tq, hps=hps, lam_init=lam_init),
        sample_args=dict(nh=nh, nt=nt, page=page, lam_init=lam_init))
    grid_spec = pltpu.PrefetchScalarGridSpec(
        num_scalar_prefetch=1,
        grid=(steps,),
        in_specs=[smem, smem, const(lvec.shape),
                  prompt_spec((None, gw, tq), lambda bi, hi, qi: (bi, hi, qi)),
                  prompt_spec((None, t, gw), lambda bi, hi, qi: (bi, 0, hi)),
                  prompt_spec((None, gw, t), lambda bi, hi, qi: (bi, hi, 0)),
                  const(slope_col.shape), const(qpos_col.shape), seq_spec(SUBLANES)]
        + k_specs + v_specs + [seq_spec(nrows), seq_spec(nrows)],
        out_specs=[prompt_spec((None, tq, gw), lambda bi, hi, qi: (bi, qi, hi)), seq_spec(SUBLANES)],
    )
    return pl.pallas_call(
        kern,
        grid_spec=grid_spec,
        out_shape=[jax.ShapeDtypeStruct((b, t, w), BF16), jax.ShapeDtypeStruct((nseq, SUBLANES, w), F32)],
        compiler_params=_params("arbitrary"),
        name="attn",
    )(page_table, slopes, bound, lvec, qt, kb, vt, slope_col, qpos_col, q8,
      *([k_t] * n_pages), *([v_rows] * n_pages), kn, vn)


def _final_kernel(o_ref, z_ref, y_ref, modb_ref, gsub_ref, wout_ref, out_ref, *, dvb, post_scale):
    d = y_ref.shape[-1]
    o = o_ref[...].astype(F32)
    on = jnp.concatenate([_rms_rows(o[:, h * dvb:(h + 1) * dvb]) for h in range(o.shape[-1] // dvb)], axis=-1)
    on = on * gsub_ref[...] * post_scale
    u = (on * _silu(z_ref[...].astype(F32))).astype(BF16)
    out = jnp.dot(u, wout_ref[...], preferred_element_type=F32)
    out_ref[...] = y_ref[...] + modb_ref[:, 2 * d:3 * d] * out


def _final(o, z, y, modb3, tiles_per_group, tm, gsub_t, wout, dvb, post_scale):
    n, d = y.shape
    kern = functools.partial(_final_kernel, dvb=dvb, post_scale=post_scale)
    return pl.pallas_call(
        kern,
        grid=(n // tm,),
        in_specs=[_row_spec(tm, o.shape[1]), _row_spec(tm, z.shape[1]), _row_spec(tm, d),
                  _mod_spec(modb3, tm, tiles_per_group), _const_spec(gsub_t.shape), _const_spec(wout.shape)],
        out_specs=_row_spec(tm, d),
        out_shape=jax.ShapeDtypeStruct((n, d), F32),
        compiler_params=_params("arbitrary"),
        name="final_proj",
    )(o, z, y, modb3, gsub_t, wout)


def _tile(vec, reps):
    return jnp.tile(vec.astype(F32), reps).reshape(1, -1)


def kernel(x_prompt, x_sample, c_prompt, c_sample, state_gla, cache_k, cache_v, page_table, norm_a, ada_w_a, ada_b_a, w_in_a, w_g2_a, b_g_a, onorm_a, w_out_a, norm_kv, ada_w_kv, ada_b_kv, w_k, w_v, g_k, norm_b, ada_w_b, ada_b_b, w_in_b, g_q, lambda_q1, lambda_k1, lambda_q2, lambda_k2, subln_b, w_out_b):
    b, t, d = x_prompt.shape
    nseq, nt, _ = x_sample.shape
    n_a, _, nh_a, dk_a, dv_a = state_gla.shape
    n_b = norm_b.shape[0]
    ka, va = nh_a * dk_a, nh_a * dv_a
    nh_b, dv_b = cache_v.shape[2], cache_v.shape[3]
    dqk_b = cache_k.shape[4]
    n_pool, page = cache_k.shape[0], cache_k.shape[1]
    tm = 256
    tm_wide = 512
    assert n_a == 1 and n_b == 1, "one GLA layer and one differential-attention layer"
    assert nt <= SUBLANES and t % tm_wide == 0 and (nseq * nt) % tm_wide == 0

    head = -(-b // SUBLANES) * SUBLANES
    c_all = jnp.concatenate([jnp.pad(c_prompt, ((0, head - b), (0, 0))), jnp.repeat(c_sample, nt, axis=0)], axis=0)
    xp = x_prompt.reshape(b * t, d)
    xs = x_sample.reshape(nseq * nt, d)
    tiles_p = t // tm

    def split_mod(mods):
        mod_head, mod_rest = mods
        return mod_head[:b].reshape(b, 1, -1), mod_rest.reshape((nseq * nt) // tm, tm, -1)

    wide = lambda ms: ms.reshape(-1, tm_wide, ms.shape[-1])

    pad8 = lambda a: jnp.pad(a.reshape(nseq, nt, -1), ((0, 0), (0, SUBLANES - nt), (0, 0)))

    l = 0
    moda_p, moda_s = split_mod(_ada(c_all, ada_w_a[l], ada_b_a[l], head))
    rank = w_g2_a.shape[1]
    w_in_pad = jnp.pad(w_in_a[l], ((0, 0), (0, LANES - rank))).astype(BF16)
    wg2_pad = jnp.pad(w_g2_a[l], ((0, LANES - rank), (0, 0))).astype(BF16)
    gla_args = (norm_a[l], w_in_pad, wg2_pad, b_g_a[l], ka, va, dk_a)
    op, zp, state_p = _gla_prompt(xp, moda_p, norm_a[l], w_in_pad, wg2_pad, b_g_a[l], b, t, nh_a, dk_a, dv_a)
    qs, ks_, vs_, zs, lgs = _gla_in(xs, wide(moda_s), 1, tm_wide, *gla_args)
    os8, state_s = _gla_sample(pad8(qs).reshape(-1, ka), pad8(ks_).reshape(-1, ka),
                               pad8(vs_).reshape(-1, va), pad8(lgs).reshape(-1, ka),
                               state_gla[l], nh_a, dk_a, dv_a)
    os_ = os8.reshape(nseq, SUBLANES, va)[:, :nt].reshape(nseq * nt, va)

    j = 0
    lam_init = 0.8 - 0.6 * math.exp(-0.3 * (n_a + j))
    modkv_p, modkv_s = split_mod(_ada(c_all, ada_w_kv, ada_b_kv, head))
    modb_p, modb_s = split_mod(_ada(c_all, ada_w_b[j], ada_b_b[j], head))
    mid_w = (_tile(onorm_a[l], nh_a), w_out_a[l].astype(BF16), norm_kv.reshape(1, d),
             w_k.astype(BF16), w_v.astype(BF16), _tile(g_k, 2 * nh_b), norm_b[j].reshape(1, d),
             w_in_b[j].astype(BF16), _tile(g_q[j], 2 * nh_b), nh_a, dv_a, dqk_b ** -0.5 * LOG2E)
    y1p, ktp, vp, kbp, vtp, qtp, zbp = _mid(
        op, zp, xp, moda_p, modkv_p, modb_p, tiles_p, tm, *mid_w, token_minor=True)
    y1s, ks, vs, kbs, vbs, qbs, zbs = _mid(
        os_, zs, xs, moda_s, modkv_s, modb_s, 1, tm, *mid_w, token_minor=False)

    w = nh_b * dv_b
    lvec = jnp.stack([lambda_q1[j], lambda_k1[j], lambda_q2[j], lambda_k2[j]]).astype(F32)
    slopes = 2.0 ** (-8.0 * jnp.arange(1, nh_b + 1, dtype=F32) / nh_b) * LOG2E
    gain_bound = jnp.max(jnp.abs(g_q[j])) * jnp.max(jnp.abs(g_k))
    score_bound = (1.02 * dqk_b ** 0.5 * LOG2E * gain_bound + 1.0).astype(F32).reshape(1)
    past = page_table.shape[1] * page
    groups = 2 * nh_b
    slope_col = jnp.repeat(slopes, 2 * nt).reshape(nt * groups, 1)
    qpos_col = (past + jnp.tile(jnp.arange(nt, dtype=jnp.int32), groups)).reshape(nt * groups, 1)
    pad_rows = lambda a, r: jnp.pad(a.reshape(nseq, nt, w), ((0, 0), (0, r - nt), (0, 0)))
    q_rep = jnp.tile(qbs.reshape(nseq, nt, w), (1, SUBLANES // nt, 1))
    k_t = cache_k.transpose(0, 2, 3, 4, 1).reshape(n_pool, w, page)
    v_rows = cache_v.reshape(n_pool, page * nh_b, dv_b)
    ap, as8 = _attn(qtp, kbp.reshape(b, t, w), vtp, slopes, score_bound, lvec,
                    q_rep, k_t, v_rows, page_table, pad_rows(kbs, 2 * SUBLANES), pad_rows(vbs, 2 * SUBLANES),
                    slope_col, qpos_col, nh_b, nt, lam_init)
    as_ = as8[:, :nt].reshape(nseq * nt, w)

    fin_w = (_tile(subln_b[j], nh_b), w_out_b[j].astype(BF16), dv_b, 1.0 - lam_init)
    yp = _final(ap.reshape(b * t, w), zbp, y1p, modb_p, t // tm_wide, tm_wide, *fin_w)
    ys = _final(as_, zbs, y1s, wide(modb_s), 1, tm_wide, *fin_w)

    kp = ktp.reshape(b, nh_b, 2, dqk_b, t).transpose(0, 4, 1, 2, 3)
    return (yp.reshape(b, t, d), ys.reshape(nseq, nt, d), state_p[None], state_s[None],
            kp, vp.reshape(b, t, nh_b, dv_b),
            ks.reshape(nseq, nt, nh_b, 2, dqk_b), vs.reshape(nseq, nt, nh_b, dv_b))
```

```python
import functools
import math

import jax
import jax.numpy as jnp
from jax import lax
from jax.experimental import pallas as pl
from jax.experimental.pallas import tpu as pltpu

F32 = jnp.float32
BF16 = jnp.bfloat16
EPS = 1e-6
GATE_NORMALIZER = 16.0
LOG2E = math.log2(math.e)
LANES = 128
SUBLANES = 8
VMEM_LIMIT = 56 * 1024 * 1024

NT_DIMS = (((1,), (1,)), ((), ()))
TN_DIMS = (((0,), (0,)), ((), ()))


def _params(*sem):
    return pltpu.CompilerParams(dimension_semantics=sem, vmem_limit_bytes=VMEM_LIMIT)


def _silu(x):
    return x / (1.0 + jnp.exp(-x))


def _rms_rows(x):
    return x * lax.rsqrt(jnp.mean(x * x, axis=-1, keepdims=True) + EPS)


def _const_spec(shape):
    zeros = (0,) * len(shape)
    return pl.BlockSpec(shape, lambda *_: zeros, pipeline_mode=pl.Buffered(1))


def _ada_kernel(c_ref, w_ref, b_ref, head_ref, rest_ref):
    s = _silu(c_ref[...]).astype(BF16)
    mod = jnp.dot(s, w_ref[...].astype(BF16), preferred_element_type=F32) + b_ref[...]
    head = head_ref.shape[0]
    head_ref[...] = mod[0:head]
    rest_ref[...] = mod[head:]


def _ada(c, w, b, head, tn=1024):
    n, d = c.shape
    m = w.shape[1]
    return pl.pallas_call(
        _ada_kernel,
        grid=(m // tn,),
        in_specs=[pl.BlockSpec((n, d), lambda j: (0, 0)),
                  pl.BlockSpec((d, tn), lambda j: (0, j)),
                  pl.BlockSpec((1, tn), lambda j: (0, j))],
        out_specs=[pl.BlockSpec((head, tn), lambda j: (0, j)), pl.BlockSpec((n - head, tn), lambda j: (0, j))],
        out_shape=[jax.ShapeDtypeStruct((head, m), F32), jax.ShapeDtypeStruct((n - head, m), F32)],
        compiler_params=_params("arbitrary"),
        name="ada_mod",
    )(c, w, b.reshape(1, m))


def _mod_spec(mod3, tm, tiles_per_group):
    r = mod3.shape[1]
    w = mod3.shape[2]
    if r == 1:
        return pl.BlockSpec((None, 1, w), lambda i: (i // tiles_per_group, 0, 0))
    return pl.BlockSpec((None, tm, w), lambda i: (i, 0, 0))


def _row_spec(tm, w):
    return pl.BlockSpec((tm, w), lambda i: (i, 0))


def _gla_project(x_ref, mod_ref, g_ref, w_ref, wg2_ref, bg_ref, *, ka, va, dk):
    x = x_ref[...]
    d = x.shape[-1]
    mod = mod_ref[...]
    h = _rms_rows(x) * g_ref[...] * (1.0 + mod[:, d:2 * d]) + mod[:, 0:d]
    p = jnp.dot(h.astype(BF16), w_ref[...], preferred_element_type=F32)
    glow = p[:, 2 * ka + 2 * va:]
    g2 = jnp.dot(glow.astype(BF16), wg2_ref[...], preferred_element_type=F32) + bg_ref[...]
    logsig = jnp.minimum(g2, 0.0) - jnp.log1p(jnp.exp(-jnp.abs(g2)))
    return (p[:, 0:ka] * (dk ** -0.5), p[:, ka:2 * ka], p[:, 2 * ka:2 * ka + va],
            p[:, 2 * ka + va:2 * ka + 2 * va], logsig * (1.0 / GATE_NORMALIZER))


def _gla_in_kernel(x_ref, mod_ref, g_ref, w_ref, wg2_ref, bg_ref,
                   q_ref, k_ref, v_ref, z_ref, lg_ref, *, ka, va, dk):
    q, k, v, z, lg = _gla_project(x_ref, mod_ref, g_ref, w_ref, wg2_ref, bg_ref, ka=ka, va=va, dk=dk)
    q_ref[...] = q
    k_ref[...] = k
    v_ref[...] = v
    z_ref[...] = z
    lg_ref[...] = lg


def _gla_in(x2, mod3, tiles_per_group, tm, g, w_pad, wg2_pad, bg, ka, va, dk):
    n, d = x2.shape
    wcols = w_pad.shape[1]
    kern = functools.partial(_gla_in_kernel, ka=ka, va=va, dk=dk)
    outs = [jax.ShapeDtypeStruct((n, ka), F32), jax.ShapeDtypeStruct((n, ka), F32),
            jax.ShapeDtypeStruct((n, va), F32), jax.ShapeDtypeStruct((n, va), F32),
            jax.ShapeDtypeStruct((n, ka), F32)]
    return pl.pallas_call(
        kern,
        grid=(n // tm,),
        in_specs=[_row_spec(tm, d), _mod_spec(mod3, tm, tiles_per_group),
                  _const_spec((1, d)), _const_spec((d, wcols)),
                  _const_spec(wg2_pad.shape), _const_spec((1, ka))],
        out_specs=[_row_spec(tm, ka), _row_spec(tm, ka), _row_spec(tm, va),
                   _row_spec(tm, va), _row_spec(tm, ka)],
        out_shape=outs,
        compiler_params=_params("arbitrary"),
        name="gla_in",
    )(x2, mod3, g.reshape(1, d), w_pad, wg2_pad, bg.reshape(1, ka))


def _cumsum_rows(tril_b, x):
    hi = x.astype(BF16)
    rest = x - hi.astype(F32)
    mid = rest.astype(BF16)
    lo = (rest - mid.astype(F32)).astype(BF16)
    dot = lambda part: jnp.dot(tril_b, part, preferred_element_type=F32)
    return dot(hi) + dot(mid) + dot(lo)


def _level_masks(row, col, max_block):
    levels = []
    n = 2
    while n <= max_block:
        sh = (n // 2).bit_length() - 1
        rb = row >> sh
        cb = col >> sh
        levels.append((n, jnp.where((cb & 1) == 0, rb - cb, 0) == 1))
        n *= 2
    return levels


def _midpoint_rows(cum, n, sub):
    c, dk = cum.shape
    if n == 2:
        return jnp.where((sub & 1) == 1, pltpu.roll(cum, 1, 0), cum)
    if n == 4:
        r = sub & 3
        return jnp.where(r == 0, pltpu.roll(cum, c - 1, 0),
                         jnp.where(r == 1, cum,
                                   jnp.where(r == 2, pltpu.roll(cum, 1, 0), pltpu.roll(cum, 2, 0))))
    half = n // 2
    return jnp.concatenate(
        [jnp.broadcast_to(cum[i * n + half - 1:i * n + half, :], (n, dk)) for i in range(c // n)], axis=0)


def _intra_operands(q, k, cum, levels, sub):
    ops = []
    for n, _ in levels:
        ref = _midpoint_rows(cum, n, sub)
        ops.append(((q * jnp.exp(cum - ref)).astype(BF16), (k * jnp.exp(ref - cum)).astype(BF16)))
    return ops, jnp.sum(q * k, axis=-1, keepdims=True)


def _intra_scores(ops, own, levels, eye):
    a = jnp.where(eye, own, 0.0)
    for (_, mask), (qs, ks) in zip(levels, ops):
        a = jnp.where(mask, lax.dot_general(qs, ks, NT_DIMS, preferred_element_type=F32), a)
    return a


def _gla_prompt_kernel(x_ref, mod_ref, g_ref, w_ref, wg2_ref, bg_ref, o_ref, z_ref, s_ref, st_scr,
                       *, c, nc, nh, dk, dv):
    ci = pl.program_id(1)

    @pl.when(ci == 0)
    def _():
        st_scr[...] = jnp.zeros_like(st_scr)

    q, k, v, z, lg = _gla_project(
        x_ref, mod_ref, g_ref, w_ref, wg2_ref, bg_ref, ka=nh * dk, va=nh * dv, dk=dk)
    z_ref[...] = z.astype(z_ref.dtype)
    row = lax.broadcasted_iota(jnp.int32, (c, c), 0)
    col = lax.broadcasted_iota(jnp.int32, (c, c), 1)
    sub = lax.broadcasted_iota(jnp.int32, (c, dk), 0)
    tril_b = jnp.where(col <= row, 1.0, 0.0).astype(BF16)
    levels = _level_masks(row, col, c)
    pairs = [(j, h) for j in range(nc) for h in range(nh)]
    rows = [slice(j * c, (j + 1) * c) for j in range(nc)]
    sk = [slice(h * dk, (h + 1) * dk) for h in range(nh)]
    sv = [slice(h * dv, (h + 1) * dv) for h in range(nh)]
    cums = [_cumsum_rows(tril_b, lg[rows[j], :]) for j in range(nc)]
    prep = {}
    for j, h in pairs:
        qh, kh, cum = q[rows[j], sk[h]], k[rows[j], sk[h]], cums[j][:, sk[h]]
        last = cum[c - 1:c, :]
        ops, own = _intra_operands(qh, kh, cum, levels, sub)
        prep[j, h] = (ops, own, (qh * jnp.exp(cum)).astype(BF16),
                      (kh * jnp.exp(last - cum)).astype(BF16), jnp.exp(last))
    local = {}
    for j, h in pairs:
        ops, own, _, kd, _ = prep[j, h]
        vb = v[rows[j], sv[h]].astype(BF16)
        a = _intra_scores(ops, own, levels, row == col)
        local[j, h] = (jnp.dot(a.astype(BF16), vb, preferred_element_type=F32),
                       lax.dot_general(vb, kd, TN_DIMS, preferred_element_type=F32))
    for h in range(nh):
        st = st_scr[h]
        for j in range(nc):
            o_intra, update = local[j, h]
            _, _, qe, _, decay = prep[j, h]
            o_ref[rows[j], sv[h]] = (o_intra + lax.dot_general(
                qe, st.astype(BF16), NT_DIMS, preferred_element_type=F32)).astype(o_ref.dtype)
            st = decay * st + update
        st_scr[h] = st

    @pl.when(ci == pl.num_programs(1) - 1)
    def _():
        for h in range(nh):
            s_ref[h] = st_scr[h].T


def _gla_prompt(x2, mod3, g, w_pad, wg2_pad, bg, b, t, nh, dk, dv, c=64, chunks_per_step=4):
    n, d = x2.shape
    rows = c * chunks_per_step
    assert t % rows == 0 and mod3.shape[1] == 1
    steps = t // rows
    ka, va = nh * dk, nh * dv
    kern = functools.partial(_gla_prompt_kernel, c=c, nc=chunks_per_step, nh=nh, dk=dk, dv=dv)
    row = lambda w: pl.BlockSpec((rows, w), lambda bi, ci: (bi * steps + ci, 0))
    return pl.pallas_call(
        kern,
        grid=(b, steps),
        in_specs=[row(d), pl.BlockSpec((None, 1, mod3.shape[2]), lambda bi, ci: (bi, 0, 0)),
                  _const_spec((1, d)), _const_spec(w_pad.shape), _const_spec(wg2_pad.shape),
                  _const_spec((1, ka))],
        out_specs=[row(va), row(va),
                   pl.BlockSpec((None, nh, dk, dv), lambda bi, ci: (bi, 0, 0, 0))],
        out_shape=[jax.ShapeDtypeStruct((n, va), BF16), jax.ShapeDtypeStruct((n, va), BF16),
                   jax.ShapeDtypeStruct((b, nh, dk, dv), F32)],
        scratch_shapes=[pltpu.VMEM((nh, dv, dk), F32)],
        compiler_params=_params("arbitrary", "arbitrary"),
        name="gla_prompt",
    )(x2, mod3, g.reshape(1, d), w_pad, wg2_pad, bg.reshape(1, ka))


def _gla_sample_kernel(q_ref, k_ref, v_ref, lg_ref, s0_ref, o_ref, s_ref, *, g, nh, dk, dv):
    r = g * SUBLANES
    row = lax.broadcasted_iota(jnp.int32, (r, r), 0)
    col = lax.broadcasted_iota(jnp.int32, (r, r), 1)
    sub = lax.broadcasted_iota(jnp.int32, (r, dk), 0)
    same_seq = (row >> 3) == (col >> 3)
    cum_all = _cumsum_rows(jnp.where(same_seq & (col <= row), 1.0, 0.0).astype(BF16), lg_ref[...])
    levels = _level_masks(row, col, SUBLANES)
    for h in range(nh):
        sk = slice(h * dk, (h + 1) * dk)
        sv = slice(h * dv, (h + 1) * dv)
        qh = q_ref[:, sk]
        kh = k_ref[:, sk]
        vh = v_ref[:, sv]
        cum = cum_all[:, sk]
        ops, own = _intra_operands(qh, kh, cum, levels, sub)
        a = _intra_scores(ops, own, levels, row == col)
        o_intra = jnp.dot(a.astype(BF16), vh.astype(BF16), preferred_element_type=F32)
        qe = qh * jnp.exp(cum)
        first_row = lax.broadcasted_iota(jnp.int32, (SUBLANES, dk), 0) == 0
        ones = jnp.ones((SUBLANES, LANES), F32)
        for i in range(g):
            rs = slice(i * SUBLANES, (i + 1) * SUBLANES)
            st = s0_ref[i, h]
            o_ref[rs, sv] = o_intra[rs] + jnp.dot(
                qe[rs].astype(BF16), st.astype(BF16), preferred_element_type=F32)
            last = cum[(i + 1) * SUBLANES - 1:(i + 1) * SUBLANES, :]
            kd = (kh[rs] * jnp.exp(last - cum[rs])).astype(BF16)
            decay_row = jnp.where(first_row, jnp.broadcast_to(jnp.exp(last), (SUBLANES, dk)), 0.0)
            decay = lax.dot_general(decay_row, ones, TN_DIMS, precision=lax.Precision.HIGHEST,
                                    preferred_element_type=F32)
            s_ref[i, h] = jnp.concatenate([decay] * (dv // LANES), axis=1) * st + lax.dot_general(
                kd, vh[rs].astype(BF16), TN_DIMS, preferred_element_type=F32)


def _gla_sample(q, k, v, lg, s0, nh, dk, dv, g=8):
    nseq = s0.shape[0]
    r = g * SUBLANES
    kern = functools.partial(_gla_sample_kernel, g=g, nh=nh, dk=dk, dv=dv)
    st_spec = pl.BlockSpec((g, nh, dk, dv), lambda i: (i, 0, 0, 0))
    return pl.pallas_call(
        kern,
        grid=(nseq // g,),
        in_specs=[_row_spec(r, nh * dk), _row_spec(r, nh * dk), _row_spec(r, nh * dv),
                  _row_spec(r, nh * dk), st_spec],
        out_specs=[_row_spec(r, nh * dv), st_spec],
        out_shape=[jax.ShapeDtypeStruct((nseq * SUBLANES, nh * dv), F32),
                   jax.ShapeDtypeStruct(s0.shape, F32)],
        compiler_params=_params("arbitrary"),
        name="gla_sample",
    )(q, k, v, lg, s0)


def _rms_groups64(x, lane_lo):
    outs = []
    for j in range(x.shape[-1] // LANES):
        xj = x[:, j * LANES:(j + 1) * LANES]
        sq = xj * xj
        s0 = jnp.sum(jnp.where(lane_lo, sq, 0.0), axis=-1, keepdims=True)
        s1 = jnp.sum(jnp.where(lane_lo, 0.0, sq), axis=-1, keepdims=True)
        ms = jnp.where(lane_lo, s0, s1) * (2.0 / LANES)
        outs.append(xj * lax.rsqrt(ms + EPS))
    return jnp.concatenate(outs, axis=-1)


def _mid_kernel(o_ref, z_ref, x_ref, moda_ref, modkv_ref, modb_ref,
                onorm_ref, wout_ref, gkv_ref, wk_ref, wv_ref, gk_ref, gb_ref, winb_ref, gq_ref,
                y_ref, k_ref, v_ref, kb_ref, vb_ref, qb_ref, zb_ref, *, nh, dv, qk_scale, token_minor):
    d = x_ref.shape[-1]
    o = o_ref[...].astype(F32)
    on = jnp.concatenate([_rms_rows(o[:, h * dv:(h + 1) * dv]) for h in range(nh)], axis=-1)
    on = on * onorm_ref[...]
    u = (on * _silu(z_ref[...].astype(F32))).astype(BF16)
    out = jnp.dot(u, wout_ref[...], preferred_element_type=F32)
    y = x_ref[...] + moda_ref[:, 2 * d:3 * d] * out
    y_ref[...] = y
    yn = _rms_rows(y)
    lane_lo = lax.broadcasted_iota(jnp.int32, (1, LANES), 1) < (LANES // 2)

    modkv = modkv_ref[...]
    h2 = (yn * gkv_ref[...] * (1.0 + modkv[:, d:2 * d]) + modkv[:, 0:d]).astype(BF16)
    kk = _rms_groups64(jnp.dot(h2, wk_ref[...], preferred_element_type=F32), lane_lo) * gk_ref[...]
    k_ref[...] = kk.T if token_minor else kk
    kb_ref[...] = kk.astype(BF16)
    vv = jnp.dot(h2, wv_ref[...], preferred_element_type=F32)
    v_ref[...] = vv
    vb_ref[...] = (vv.T if token_minor else vv).astype(BF16)

    modb = modb_ref[...]
    h3 = (yn * gb_ref[...] * (1.0 + modb[:, d:2 * d]) + modb[:, 0:d]).astype(BF16)
    qz = jnp.dot(h3, winb_ref[...], preferred_element_type=F32)
    qb = qz.shape[-1] - zb_ref.shape[-1]
    qq = _rms_groups64(qz[:, 0:qb], lane_lo) * (gq_ref[...] * qk_scale)
    qb_ref[...] = (qq.T if token_minor else qq).astype(BF16)
    zb_ref[...] = qz[:, qb:].astype(zb_ref.dtype)


def _mid(o, z, x2, moda3, modkv3, modb3, tiles_per_group, tm,
         onorm_t, wout, gkv, wk, wv, gk_t, gb, winb, gq_t, nh, dv, qk_scale, token_minor):
    n, d = x2.shape
    qb = wk.shape[1]
    zb = winb.shape[1] - qb
    kern = functools.partial(_mid_kernel, nh=nh, dv=dv, qk_scale=qk_scale, token_minor=token_minor)
    ms = lambda m: _mod_spec(m, tm, tiles_per_group)
    if token_minor:
        groups = n // (tm * tiles_per_group)
        t = tm * tiles_per_group
        feat = lambda w, dt: jax.ShapeDtypeStruct((groups, w, t), dt)
        feat_spec = lambda w: pl.BlockSpec(
            (None, w, tm), lambda i: (i // tiles_per_group, 0, i % tiles_per_group))
    else:
        feat = lambda w, dt: jax.ShapeDtypeStruct((n, w), dt)
        feat_spec = lambda w: _row_spec(tm, w)
    outs = [jax.ShapeDtypeStruct((n, d), F32), feat(qb, F32),
            jax.ShapeDtypeStruct((n, zb), F32), jax.ShapeDtypeStruct((n, qb), BF16),
            feat(zb, BF16), feat(qb, BF16),
            jax.ShapeDtypeStruct((n, zb), BF16)]
    out_specs = [_row_spec(tm, d), feat_spec(qb), _row_spec(tm, zb), _row_spec(tm, qb),
                 feat_spec(zb), feat_spec(qb), _row_spec(tm, zb)]
    return pl.pallas_call(
        kern,
        grid=(n // tm,),
        in_specs=[_row_spec(tm, o.shape[1]), _row_spec(tm, z.shape[1]), _row_spec(tm, d),
                  ms(moda3), ms(modkv3), ms(modb3),
                  _const_spec(onorm_t.shape), _const_spec(wout.shape), _const_spec(gkv.shape),
                  _const_spec(wk.shape), _const_spec(wv.shape), _const_spec(gk_t.shape),
                  _const_spec(gb.shape), _const_spec(winb.shape), _const_spec(gq_t.shape)],
        out_specs=out_specs,
        out_shape=outs,
        compiler_params=_params("arbitrary"),
        name="mid_proj",
    )(o, z, x2, moda3, modkv3, modb3, onorm_t, wout, gkv, wk, wv, gk_t, gb, winb, gq_t)


def _lambda_full(l_ref, lam_init):
    lv = l_ref[...]
    s1 = jnp.sum(lv[0:1] * lv[1:2], axis=-1, keepdims=True)
    s2 = jnp.sum(lv[2:3] * lv[3:4], axis=-1, keepdims=True)
    return jnp.exp(s1) - jnp.exp(s2) + lam_init


MAX_STATIC_SHIFT = 60.0


def _attn_prompt_body(hg, qi, slope_ref, bound_ref, l_ref, qt_ref, k_ref, vt_ref, o_ref, *, tq, tk, hps, lam_init):
    half = LANES // 2
    heads = range(hps)
    cols = [slice(hh * LANES, (hh + 1) * LANES) for hh in heads]
    zeros = jnp.zeros((half, tq), BF16)
    krow = lax.broadcasted_iota(jnp.int32, (tk, LANES), 0).astype(F32)
    ones_rows = jnp.ones((2 * SUBLANES, tk), BF16)
    slopes, qqs = [], []
    for hh in heads:
        slopes.append(slope_ref[hg * hps + hh])
        qt = qt_ref[cols[hh], :]
        qqs.append(jnp.concatenate([jnp.concatenate([qt[0:half], zeros], axis=0),
                                    jnp.concatenate([zeros, qt[half:LANES]], axis=0)], axis=1))

    def scores(j, causal):
        start = pl.multiple_of(j * tk, tk)
        offset = (j * tk - qi * tq).astype(F32)
        out = []
        for hh in heads:
            key_bias = slopes[hh] * (krow + offset)
            s = jnp.dot(k_ref[pl.ds(start, tk), cols[hh]], qqs[hh], preferred_element_type=F32)
            out.append(s + jnp.concatenate([key_bias] * (2 * tq // LANES), axis=1))
        if causal:
            kk = lax.broadcasted_iota(jnp.int32, (tk, 2 * tq), 0)
            qpos = lax.broadcasted_iota(jnp.int32, (tk, 2 * tq), 1)
            keep = kk <= jnp.where(qpos >= tq, qpos - tq, qpos)
            out = [jnp.where(keep, s, -jnp.inf) for s in out]
        return start, out

    def weighted_values(start, probs):
        return [jnp.dot(jnp.concatenate([vt_ref[cols[hh], pl.ds(start, tk)], ones_rows], axis=0), probs[hh],
                        preferred_element_type=F32) for hh in heads]

    def tile_fixed(j, accs, causal, shifts):
        start, ss = scores(j, causal)
        pvs = weighted_values(start, [jnp.exp2(ss[hh] - shifts[hh]).astype(BF16) for hh in heads])
        return tuple(accs[hh] + pvs[hh] for hh in heads)

    def tile_pair_fixed(j, accs, shifts):
        tiles = [scores(j, False), scores(j + 1, False)]
        probs = [[jnp.exp2(ss[hh] - shifts[hh]).astype(BF16) for hh in heads] for _, ss in tiles]
        pvs = [weighted_values(tiles[t][0], probs[t]) for t in range(2)]
        return tuple(accs[hh] + pvs[0][hh] + pvs[1][hh] for hh in heads)

    def tile_online(j, carries, causal):
        start, ss = scores(j, causal)
        stats, probs = [], []
        for hh in heads:
            m = carries[hh][0]
            m_new = jnp.maximum(m, jnp.max(ss[hh], axis=0, keepdims=True))
            stats.append((m_new, jnp.exp2(m - m_new)))
            probs.append(jnp.exp2(ss[hh] - m_new).astype(BF16))
        pvs = weighted_values(start, probs)
        return tuple((stats[hh][0], stats[hh][1] * carries[hh][1] + pvs[hh]) for hh in heads)

    def write(accs):
        lam = _lambda_full(l_ref, lam_init)
        for hh in heads:
            r = accs[hh][0:LANES] * (1.0 / accs[hh][LANES:LANES + 1])
            o_ref[:, cols[hh]] = (r[:, 0:tq] - lam * r[:, tq:2 * tq]).T.astype(o_ref.dtype)

    bound = bound_ref[0]
    zero_acc = tuple(jnp.zeros((LANES + 2 * SUBLANES, 2 * tq), F32) for _ in heads)

    @pl.when(bound <= MAX_STATIC_SHIFT)
    def _():
        qidx = lax.broadcasted_iota(jnp.int32, (1, 2 * tq), 1)
        qidx = jnp.where(qidx >= tq, qidx - tq, qidx).astype(F32)
        shifts = [bound + slopes[hh] * qidx for hh in heads]
        pairs = qi // 2
        accs = lax.fori_loop(0, pairs, lambda p, a: tile_pair_fixed(2 * p, a, shifts), zero_acc)
        accs = lax.fori_loop(2 * pairs, qi, lambda j, a: tile_fixed(j, a, False, shifts), accs)
        write(tile_fixed(qi, accs, True, shifts))

    @pl.when(jnp.logical_not(bound <= MAX_STATIC_SHIFT))
    def _():
        init = tuple((jnp.full((1, 2 * tq), -jnp.inf, F32), zero_acc[hh]) for hh in heads)
        carries = lax.fori_loop(0, qi, lambda j, c: tile_online(j, c, False), init)
        write([c[1] for c in tile_online(qi, carries, True)])


def _attn_sample_body(l_ref, slope_ref, qpos_ref, q_ref, k_refs, v_refs, kn_ref, vn_ref, o_ref,
                      *, nh, nt, page, lam_init):
    n_pages = len(k_refs)
    past = n_pages * page
    w = q_ref.shape[-1]
    rows = 2 * nh * nt
    hr = 2 * nt
    dv = w // nh

    q = q_ref[...].astype(F32)
    qt = jnp.concatenate([q] * (rows // SUBLANES), axis=0)
    lane_grp = lax.broadcasted_iota(jnp.int32, (rows, w), 1) // (LANES // 2)
    row_grp = lax.broadcasted_iota(jnp.int32, (rows, w), 0) // nt
    qbd = jnp.where(lane_grp == row_grp, qt, 0.0).astype(BF16)

    s_past = jnp.concatenate(
        [jnp.dot(qbd, k_refs[i][...].astype(BF16), preferred_element_type=F32) for i in range(n_pages)],
        axis=1)
    kpos = lax.broadcasted_iota(jnp.int32, (1, past), 1)
    s_past = s_past - slope_ref[...] * (qpos_ref[...] - kpos).astype(F32)
    idx = lax.broadcasted_iota(jnp.int32, (1, kn_ref.shape[0]), 1)
    dist = qpos_ref[...] - (past + idx)
    s_new = lax.dot_general(qbd, kn_ref[...], NT_DIMS, preferred_element_type=F32)
    s_new = jnp.where((idx < nt) & (dist >= 0), s_new - slope_ref[...] * dist.astype(F32), -jnp.inf)

    m = jnp.maximum(jnp.max(s_past, axis=-1, keepdims=True), jnp.max(s_new, axis=-1, keepdims=True))
    pr_past = jnp.exp2(s_past - m)
    pr_new = jnp.exp2(s_new - m)
    denom = jnp.sum(pr_past, axis=-1, keepdims=True) + jnp.sum(pr_new, axis=-1, keepdims=True)
    head_values = [
        jnp.concatenate([v_refs[i][pl.ds(h, page, stride=nh), :].astype(BF16) for i in range(n_pages)], axis=0)
        for h in range(nh)]
    pv = jnp.concatenate(
        [jnp.dot(pr_past[h * hr:(h + 1) * hr].astype(BF16), head_values[h], preferred_element_type=F32)
         for h in range(nh)], axis=0)
    pv_new = jnp.dot(pr_new.astype(BF16), vn_ref[...], preferred_element_type=F32)
    own_new = jnp.concatenate([pv_new[h * hr:(h + 1) * hr, h * dv:(h + 1) * dv] for h in range(nh)], axis=0)
    r = (pv + own_new) / denom
    diff = r - _lambda_full(l_ref, lam_init) * pltpu.roll(r, rows - nt, 0)
    for h in range(nh):
        o_ref[:, h * dv:(h + 1) * dv] = diff[h * hr:(h + 1) * hr]


def _attn_kernel(pt_ref, slope_ref, bound_ref, l_ref, qt_ref, k_ref, vt_ref,
                 slope_col_ref, qpos_ref, q_ref, *refs,
                 n_pages, prompt_steps, sample_steps, steps_per_batch, q_tiles, prompt_args, sample_args):
    del pt_ref
    k_refs, v_refs = refs[0:n_pages], refs[n_pages:2 * n_pages]
    kn_ref, vn_ref, op_ref, os_ref = refs[2 * n_pages:]
    i = pl.program_id(0)

    def prompt_part():
        within = i % steps_per_batch
        _attn_prompt_body(within // q_tiles, within % q_tiles, slope_ref, bound_ref, l_ref,
                          qt_ref, k_ref, vt_ref, op_ref, **prompt_args)

    def sample_part():
        _attn_sample_body(l_ref, slope_col_ref, qpos_ref, q_ref, k_refs, v_refs, kn_ref, vn_ref,
                          os_ref, **sample_args)

    if prompt_steps < sample_steps:
        pl.when(i < prompt_steps)(prompt_part)
    else:
        prompt_part()
    if sample_steps < prompt_steps:
        pl.when(i < sample_steps)(sample_part)
    else:
        sample_part()


def _attn(qt, kb, vt, slopes, bound, lvec, q8, k_t, v_rows, page_table, kn, vn, slope_col, qpos_col,
          nh, nt, lam_init, tq=512, hps=2):
    b, t, w = kb.shape
    nseq = q8.shape[0]
    n_pool, _, page = k_t.shape
    dv = v_rows.shape[-1]
    n_pages = page_table.shape[1]
    nrows = kn.shape[1]
    rows = 2 * nh * nt
    assert 2 * nt == SUBLANES and nh % hps == 0 and t % tq == 0
    gw = hps * LANES
    q_tiles = t // tq
    steps_per_batch = (nh // hps) * q_tiles
    prompt_steps = b * steps_per_batch
    steps = max(prompt_steps, nseq)

    def prompt_index(i):
        i = jnp.minimum(i, prompt_steps - 1)
        within = i % steps_per_batch
        return i // steps_per_batch, within // q_tiles, within % q_tiles

    def seq_index(i):
        return jnp.minimum(i, nseq - 1)

    def prompt_spec(shape, pick):
        return pl.BlockSpec(shape, lambda i, pt: pick(*prompt_index(i)))

    k_specs = [pl.BlockSpec((None, w, page), lambda i, pt, p=p: (pt[seq_index(i), p], 0, 0))
               for p in range(n_pages)]
    v_specs = [pl.BlockSpec((None, page * nh, dv), lambda i, pt, p=p: (pt[seq_index(i), p], 0, 0))
               for p in range(n_pages)]
    seq_spec = lambda r: pl.BlockSpec((None, r, w), lambda i, pt: (seq_index(i), 0, 0))
    const = lambda shape: pl.BlockSpec(shape, lambda i, pt: (0,) * len(shape))
    smem = pl.BlockSpec(memory_space=pltpu.SMEM)
    kern = functools.partial(
        _attn_kernel, n_pages=n_pages, prompt_steps=prompt_steps, sample_steps=nseq,
        steps_per_batch=steps_per_batch, q_tiles=q_tiles,
        prompt_args=dict(tq=tq, tk=tq, hps=hps, lam_init=lam_init),
        sample_args=dict(nh=nh, nt=nt, page=page, lam_init=lam_init))
    grid_spec = pltpu.PrefetchScalarGridSpec(
        num_scalar_prefetch=1,
        grid=(steps,),
        in_specs=[smem, smem, const(lvec.shape),
                  prompt_spec((None, gw, tq), lambda bi, hi, qi: (bi, hi, qi)),
                  prompt_spec((None, t, gw), lambda bi, hi, qi: (bi, 0, hi)),
                  prompt_spec((None, gw, t), lambda bi, hi, qi: (bi, hi, 0)),
                  const(slope_col.shape), const(qpos_col.shape), seq_spec(SUBLANES)]
        + k_specs + v_specs + [seq_spec(nrows), seq_spec(nrows)],
        out_specs=[prompt_spec((None, tq, gw), lambda bi, hi, qi: (bi, qi, hi)), seq_spec(SUBLANES)],
    )
    return pl.pallas_call(
        kern,
        grid_spec=grid_spec,
        out_shape=[jax.ShapeDtypeStruct((b, t, w), BF16), jax.ShapeDtypeStruct((nseq, SUBLANES, w), F32)],
        compiler_params=_params("arbitrary"),
        name="attn",
    )(page_table, slopes, bound, lvec, qt, kb, vt, slope_col, qpos_col, q8,
      *([k_t] * n_pages), *([v_rows] * n_pages), kn, vn)


def _final_kernel(o_ref, z_ref, y_ref, modb_ref, gsub_ref, wout_ref, out_ref, *, dvb, post_scale):
    d = y_ref.shape[-1]
    o = o_ref[...].astype(F32)
    on = jnp.concatenate([_rms_rows(o[:, h * dvb:(h + 1) * dvb]) for h in range(o.shape[-1] // dvb)], axis=-1)
    on = on * gsub_ref[...] * post_scale
    u = (on * _silu(z_ref[...].astype(F32))).astype(BF16)
    out = jnp.dot(u, wout_ref[...], preferred_element_type=F32)
    out_ref[...] = y_ref[...] + modb_ref[:, 2 * d:3 * d] * out


def _final(o, z, y, modb3, tiles_per_group, tm, gsub_t, wout, dvb, post_scale):
    n, d = y.shape
    kern = functools.partial(_final_kernel, dvb=dvb, post_scale=post_scale)
    return pl.pallas_call(
        kern,
        grid=(n // tm,),
        in_specs=[_row_spec(tm, o.shape[1]), _row_spec(tm, z.shape[1]), _row_spec(tm, d),
                  _mod_spec(modb3, tm, tiles_per_group), _const_spec(gsub_t.shape), _const_spec(wout.shape)],
        out_specs=_row_spec(tm, d),
        out_shape=jax.ShapeDtypeStruct((n, d), F32),
        compiler_params=_params("arbitrary"),
        name="final_proj",
    )(o, z, y, modb3, gsub_t, wout)


def _tile(vec, reps):
    return jnp.tile(vec.astype(F32), reps).reshape(1, -1)


def kernel(x_prompt, x_sample, c_prompt, c_sample, state_gla, cache_k, cache_v, page_table, norm_a, ada_w_a, ada_b_a, w_in_a, w_g2_a, b_g_a, onorm_a, w_out_a, norm_kv, ada_w_kv, ada_b_kv, w_k, w_v, g_k, norm_b, ada_w_b, ada_b_b, w_in_b, g_q, lambda_q1, lambda_k1, lambda_q2, lambda_k2, subln_b, w_out_b):
    b, t, d = x_prompt.shape
    nseq, nt, _ = x_sample.shape
    n_a, _, nh_a, dk_a, dv_a = state_gla.shape
    n_b = norm_b.shape[0]
    ka, va = nh_a * dk_a, nh_a * dv_a
    nh_b, dv_b = cache_v.shape[2], cache_v.shape[3]
    dqk_b = cache_k.shape[4]
    n_pool, page = cache_k.shape[0], cache_k.shape[1]
    tm = 256
    tm_wide = 512
    assert n_a == 1 and n_b == 1, "one GLA layer and one differential-attention layer"
    assert nt <= SUBLANES and t % tm_wide == 0 and (nseq * nt) % tm_wide == 0

    head = -(-b // SUBLANES) * SUBLANES
    c_all = jnp.concatenate([jnp.pad(c_prompt, ((0, head - b), (0, 0))), jnp.repeat(c_sample, nt, axis=0)], axis=0)
    xp = x_prompt.reshape(b * t, d)
    xs = x_sample.reshape(nseq * nt, d)
    tiles_p = t // tm

    def split_mod(mods):
        mod_head, mod_rest = mods
        return mod_head[:b].reshape(b, 1, -1), mod_rest.reshape((nseq * nt) // tm, tm, -1)

    wide = lambda ms: ms.reshape(-1, tm_wide, ms.shape[-1])

    pad8 = lambda a: jnp.pad(a.reshape(nseq, nt, -1), ((0, 0), (0, SUBLANES - nt), (0, 0)))

    l = 0
    moda_p, moda_s = split_mod(_ada(c_all, ada_w_a[l], ada_b_a[l], head))
    rank = w_g2_a.shape[1]
    w_in_pad = jnp.pad(w_in_a[l], ((0, 0), (0, LANES - rank))).astype(BF16)
    wg2_pad = jnp.pad(w_g2_a[l], ((0, LANES - rank), (0, 0))).astype(BF16)
    gla_args = (norm_a[l], w_in_pad, wg2_pad, b_g_a[l], ka, va, dk_a)
    op, zp, state_p = _gla_prompt(xp, moda_p, norm_a[l], w_in_pad, wg2_pad, b_g_a[l], b, t, nh_a, dk_a, dv_a)
    qs, ks_, vs_, zs, lgs = _gla_in(xs, wide(moda_s), 1, tm_wide, *gla_args)
    os8, state_s = _gla_sample(pad8(qs).reshape(-1, ka), pad8(ks_).reshape(-1, ka),
                               pad8(vs_).reshape(-1, va), pad8(lgs).reshape(-1, ka),
                               state_gla[l], nh_a, dk_a, dv_a)
    os_ = os8.reshape(nseq, SUBLANES, va)[:, :nt].reshape(nseq * nt, va)

    j = 0
    lam_init = 0.8 - 0.6 * math.exp(-0.3 * (n_a + j))
    modkv_p, modkv_s = split_mod(_ada(c_all, ada_w_kv, ada_b_kv, head))
    modb_p, modb_s = split_mod(_ada(c_all, ada_w_b[j], ada_b_b[j], head))
    mid_w = (_tile(onorm_a[l], nh_a), w_out_a[l].astype(BF16), norm_kv.reshape(1, d),
             w_k.astype(BF16), w_v.astype(BF16), _tile(g_k, 2 * nh_b), norm_b[j].reshape(1, d),
             w_in_b[j].astype(BF16), _tile(g_q[j], 2 * nh_b), nh_a, dv_a, dqk_b ** -0.5 * LOG2E)
    y1p, ktp, vp, kbp, vtp, qtp, zbp = _mid(
        op, zp, xp, moda_p, modkv_p, modb_p, tiles_p, tm, *mid_w, token_minor=True)
    y1s, ks, vs, kbs, vbs, qbs, zbs = _mid(
        os_, zs, xs, moda_s, modkv_s, modb_s, 1, tm, *mid_w, token_minor=False)

    w = nh_b * dv_b
    lvec = jnp.stack([lambda_q1[j], lambda_k1[j], lambda_q2[j], lambda_k2[j]]).astype(F32)
    slopes = 2.0 ** (-8.0 * jnp.arange(1, nh_b + 1, dtype=F32) / nh_b) * LOG2E
    gain_bound = jnp.max(jnp.abs(g_q[j])) * jnp.max(jnp.abs(g_k))
    score_bound = (1.02 * dqk_b ** 0.5 * LOG2E * gain_bound + 1.0).astype(F32).reshape(1)
    past = page_table.shape[1] * page
    groups = 2 * nh_b
    slope_col = jnp.repeat(slopes, 2 * nt).reshape(nt * groups, 1)
    qpos_col = (past + jnp.tile(jnp.arange(nt, dtype=jnp.int32), groups)).reshape(nt * groups, 1)
    pad_rows = lambda a, r: jnp.pad(a.reshape(nseq, nt, w), ((0, 0), (0, r - nt), (0, 0)))
    q_rep = jnp.tile(qbs.reshape(nseq, nt, w), (1, SUBLANES // nt, 1))
    k_t = cache_k.transpose(0, 2, 3, 4, 1).reshape(n_pool, w, page)
    v_rows = cache_v.reshape(n_pool, page * nh_b, dv_b)
    ap, as8 = _attn(qtp, kbp.reshape(b, t, w), vtp, slopes, score_bound, lvec,
                    q_rep, k_t, v_rows, page_table, pad_rows(kbs, 2 * SUBLANES), pad_rows(vbs, 2 * SUBLANES),
                    slope_col, qpos_col, nh_b, nt, lam_init)
    as_ = as8[:, :nt].reshape(nseq * nt, w)

    fin_w = (_tile(subln_b[j], nh_b), w_out_b[j].astype(BF16), dv_b, 1.0 - lam_init)
    yp = _final(ap.reshape(b * t, w), zbp, y1p, modb_p, t // tm_wide, tm_wide, *fin_w)
    ys = _final(as_, zbs, y1s, wide(modb_s), 1, tm_wide, *fin_w)

    kp = ktp.reshape(b, nh_b, 2, dqk_b, t).transpose(0, 4, 1, 2, 3)
    return (yp.reshape(b, t, d), ys.reshape(nseq, nt, d), state_p[None], state_s[None],
            kp, vp.reshape(b, t, nh_b, dv_b),
            ks.reshape(nseq, nt, nh_b, 2, dqk_b), vs.reshape(nseq, nt, nh_b, dv_b))
```

```python
import functools
import math

import jax
import jax.numpy as jnp
from jax import lax
from jax.experimental import pallas as pl
from jax.experimental.pallas import tpu as pltpu

F32 = jnp.float32
BF16 = jnp.bfloat16
EPS = 1e-6
GATE_NORMALIZER = 16.0
LOG2E = math.log2(math.e)
LANES = 128
SUBLANES = 8
VMEM_LIMIT = 56 * 1024 * 1024

NT_DIMS = (((1,), (1,)), ((), ()))
TN_DIMS = (((0,), (0,)), ((), ()))


def _params(*sem):
    return pltpu.CompilerParams(dimension_semantics=sem, vmem_limit_bytes=VMEM_LIMIT)


def _silu(x):
    return x / (1.0 + jnp.exp(-x))


def _rms_rows(x):
    return x * lax.rsqrt(jnp.mean(x * x, axis=-1, keepdims=True) + EPS)


def _const_spec(shape):
    zeros = (0,) * len(shape)
    return pl.BlockSpec(shape, lambda *_: zeros, pipeline_mode=pl.Buffered(1))


def _ada_kernel(c_ref, w_ref, b_ref, head_ref, rest_ref):
    s = _silu(c_ref[...]).astype(BF16)
    mod = jnp.dot(s, w_ref[...].astype(BF16), preferred_element_type=F32) + b_ref[...]
    head = head_ref.shape[0]
    head_ref[...] = mod[0:head]
    rest_ref[...] = mod[head:]


def _ada(c, w, b, head, tn=1024):
    n, d = c.shape
    m = w.shape[1]
    return pl.pallas_call(
        _ada_kernel,
        grid=(m // tn,),
        in_specs=[pl.BlockSpec((n, d), lambda j: (0, 0)),
                  pl.BlockSpec((d, tn), lambda j: (0, j)),
                  pl.BlockSpec((1, tn), lambda j: (0, j))],
        out_specs=[pl.BlockSpec((head, tn), lambda j: (0, j)), pl.BlockSpec((n - head, tn), lambda j: (0, j))],
        out_shape=[jax.ShapeDtypeStruct((head, m), F32), jax.ShapeDtypeStruct((n - head, m), F32)],
        compiler_params=_params("arbitrary"),
        name="ada_mod",
    )(c, w, b.reshape(1, m))


def _mod_spec(mod3, tm, tiles_per_group):
    r = mod3.shape[1]
    w = mod3.shape[2]
    if r == 1:
        return pl.BlockSpec((None, 1, w), lambda i: (i // tiles_per_group, 0, 0))
    return pl.BlockSpec((None, tm, w), lambda i: (i, 0, 0))


def _row_spec(tm, w):
    return pl.BlockSpec((tm, w), lambda i: (i, 0))


def _gla_project(x_ref, mod_ref, g_ref, w_ref, wg2_ref, bg_ref, *, ka, va, dk):
    x = x_ref[...]
    d = x.shape[-1]
    mod = mod_ref[...]
    h = _rms_rows(x) * g_ref[...] * (1.0 + mod[:, d:2 * d]) + mod[:, 0:d]
    p = jnp.dot(h.astype(BF16), w_ref[...], preferred_element_type=F32)
    glow = p[:, 2 * ka + 2 * va:]
    g2 = jnp.dot(glow.astype(BF16), wg2_ref[...], preferred_element_type=F32) + bg_ref[...]
    logsig = jnp.minimum(g2, 0.0) - jnp.log1p(jnp.exp(-jnp.abs(g2)))
    return (p[:, 0:ka] * (dk ** -0.5), p[:, ka:2 * ka], p[:, 2 * ka:2 * ka + va],
            p[:, 2 * ka + va:2 * ka + 2 * va], logsig * (1.0 / GATE_NORMALIZER))


def _gla_in_kernel(x_ref, mod_ref, g_ref, w_ref, wg2_ref, bg_ref,
                   q_ref, k_ref, v_ref, z_ref, lg_ref, *, ka, va, dk):
    q, k, v, z, lg = _gla_project(x_ref, mod_ref, g_ref, w_ref, wg2_ref, bg_ref, ka=ka, va=va, dk=dk)
    q_ref[...] = q
    k_ref[...] = k
    v_ref[...] = v
    z_ref[...] = z
    lg_ref[...] = lg


def _gla_in(x2, mod3, tiles_per_group, tm, g, w_pad, wg2_pad, bg, ka, va, dk):
    n, d = x2.shape
    wcols = w_pad.shape[1]
    kern = functools.partial(_gla_in_kernel, ka=ka, va=va, dk=dk)
    outs = [jax.ShapeDtypeStruct((n, ka), F32), jax.ShapeDtypeStruct((n, ka), F32),
            jax.ShapeDtypeStruct((n, va), F32), jax.ShapeDtypeStruct((n, va), F32),
            jax.ShapeDtypeStruct((n, ka), F32)]
    return pl.pallas_call(
        kern,
        grid=(n // tm,),
        in_specs=[_row_spec(tm, d), _mod_spec(mod3, tm, tiles_per_group),
                  _const_spec((1, d)), _const_spec((d, wcols)),
                  _const_spec(wg2_pad.shape), _const_spec((1, ka))],
        out_specs=[_row_spec(tm, ka), _row_spec(tm, ka), _row_spec(tm, va),
                   _row_spec(tm, va), _row_spec(tm, ka)],
        out_shape=outs,
        compiler_params=_params("arbitrary"),
        name="gla_in",
    )(x2, mod3, g.reshape(1, d), w_pad, wg2_pad, bg.reshape(1, ka))


def _cumsum_rows(tril_b, x):
    hi = x.astype(BF16)
    rest = x - hi.astype(F32)
    mid = rest.astype(BF16)
    lo = (rest - mid.astype(F32)).astype(BF16)
    dot = lambda part: jnp.dot(tril_b, part, preferred_element_type=F32)
    return dot(hi) + dot(mid) + dot(lo)


def _level_masks(row, col, max_block):
    levels = []
    n = 2
    while n <= max_block:
        sh = (n // 2).bit_length() - 1
        rb = row >> sh
        cb = col >> sh
        levels.append((n, jnp.where((cb & 1) == 0, rb - cb, 0) == 1))
        n *= 2
    return levels


def _midpoint_rows(cum, n, sub):
    c, dk = cum.shape
    if n == 2:
        return jnp.where((sub & 1) == 1, pltpu.roll(cum, 1, 0), cum)
    if n == 4:
        r = sub & 3
        return jnp.where(r == 0, pltpu.roll(cum, c - 1, 0),
                         jnp.where(r == 1, cum,
                                   jnp.where(r == 2, pltpu.roll(cum, 1, 0), pltpu.roll(cum, 2, 0))))
    half = n // 2
    return jnp.concatenate(
        [jnp.broadcast_to(cum[i * n + half - 1:i * n + half, :], (n, dk)) for i in range(c // n)], axis=0)


def _intra_operands(q, k, cum, levels, sub):
    ops = []
    for n, _ in levels:
        ref = _midpoint_rows(cum, n, sub)
        ops.append(((q * jnp.exp(cum - ref)).astype(BF16), (k * jnp.exp(ref - cum)).astype(BF16)))
    return ops, jnp.sum(q * k, axis=-1, keepdims=True)


def _intra_scores(ops, own, levels, eye):
    a = jnp.where(eye, own, 0.0)
    for (_, mask), (qs, ks) in zip(levels, ops):
        a = jnp.where(mask, lax.dot_general(qs, ks, NT_DIMS, preferred_element_type=F32), a)
    return a


def _gla_prompt_kernel(x_ref, mod_ref, g_ref, w_ref, wg2_ref, bg_ref, o_ref, z_ref, s_ref, st_scr,
                       *, c, nc, nh, dk, dv):
    ci = pl.program_id(1)

    @pl.when(ci == 0)
    def _():
        st_scr[...] = jnp.zeros_like(st_scr)

    q, k, v, z, lg = _gla_project(
        x_ref, mod_ref, g_ref, w_ref, wg2_ref, bg_ref, ka=nh * dk, va=nh * dv, dk=dk)
    z_ref[...] = z.astype(z_ref.dtype)
    row = lax.broadcasted_iota(jnp.int32, (c, c), 0)
    col = lax.broadcasted_iota(jnp.int32, (c, c), 1)
    sub = lax.broadcasted_iota(jnp.int32, (c, dk), 0)
    tril_b = jnp.where(col <= row, 1.0, 0.0).astype(BF16)
    levels = _level_masks(row, col, c)
    pairs = [(j, h) for j in range(nc) for h in range(nh)]
    rows = [slice(j * c, (j + 1) * c) for j in range(nc)]
    sk = [slice(h * dk, (h + 1) * dk) for h in range(nh)]
    sv = [slice(h * dv, (h + 1) * dv) for h in range(nh)]
    cums = [_cumsum_rows(tril_b, lg[rows[j], :]) for j in range(nc)]
    prep = {}
    for j, h in pairs:
        qh, kh, cum = q[rows[j], sk[h]], k[rows[j], sk[h]], cums[j][:, sk[h]]
        last = cum[c - 1:c, :]
        ops, own = _intra_operands(qh, kh, cum, levels, sub)
        prep[j, h] = (ops, own, (qh * jnp.exp(cum)).astype(BF16),
                      (kh * jnp.exp(last - cum)).astype(BF16), jnp.exp(last))
    local = {}
    for j, h in pairs:
        ops, own, _, kd, _ = prep[j, h]
        vb = v[rows[j], sv[h]].astype(BF16)
        a = _intra_scores(ops, own, levels, row == col)
        local[j, h] = (jnp.dot(a.astype(BF16), vb, preferred_element_type=F32),
                       lax.dot_general(vb, kd, TN_DIMS, preferred_element_type=F32))
    for h in range(nh):
        st = st_scr[h]
        for j in range(nc):
            o_intra, update = local[j, h]
            _, _, qe, _, decay = prep[j, h]
            o_ref[rows[j], sv[h]] = (o_intra + lax.dot_general(
                qe, st.astype(BF16), NT_DIMS, preferred_element_type=F32)).astype(o_ref.dtype)
            st = decay * st + update
        st_scr[h] = st

    @pl.when(ci == pl.num_programs(1) - 1)
    def _():
        for h in range(nh):
            s_ref[h] = st_scr[h].T


def _gla_prompt(x2, mod3, g, w_pad, wg2_pad, bg, b, t, nh, dk, dv, c=64, chunks_per_step=4):
    n, d = x2.shape
    rows = c * chunks_per_step
    assert t % rows == 0 and mod3.shape[1] == 1
    steps = t // rows
    ka, va = nh * dk, nh * dv
    kern = functools.partial(_gla_prompt_kernel, c=c, nc=chunks_per_step, nh=nh, dk=dk, dv=dv)
    row = lambda w: pl.BlockSpec((rows, w), lambda bi, ci: (bi * steps + ci, 0))
    return pl.pallas_call(
        kern,
        grid=(b, steps),
        in_specs=[row(d), pl.BlockSpec((None, 1, mod3.shape[2]), lambda bi, ci: (bi, 0, 0)),
                  _const_spec((1, d)), _const_spec(w_pad.shape), _const_spec(wg2_pad.shape),
                  _const_spec((1, ka))],
        out_specs=[row(va), row(va),
                   pl.BlockSpec((None, nh, dk, dv), lambda bi, ci: (bi, 0, 0, 0))],
        out_shape=[jax.ShapeDtypeStruct((n, va), BF16), jax.ShapeDtypeStruct((n, va), BF16),
                   jax.ShapeDtypeStruct((b, nh, dk, dv), F32)],
        scratch_shapes=[pltpu.VMEM((nh, dv, dk), F32)],
        compiler_params=_params("arbitrary", "arbitrary"),
        name="gla_prompt",
    )(x2, mod3, g.reshape(1, d), w_pad, wg2_pad, bg.reshape(1, ka))


def _gla_sample_kernel(q_ref, k_ref, v_ref, lg_ref, s0_ref, o_ref, s_ref, *, g, nh, dk, dv):
    r = g * SUBLANES
    row = lax.broadcasted_iota(jnp.int32, (r, r), 0)
    col = lax.broadcasted_iota(jnp.int32, (r, r), 1)
    sub = lax.broadcasted_iota(jnp.int32, (r, dk), 0)
    same_seq = (row >> 3) == (col >> 3)
    cum_all = _cumsum_rows(jnp.where(same_seq & (col <= row), 1.0, 0.0).astype(BF16), lg_ref[...])
    levels = _level_masks(row, col, SUBLANES)
    for h in range(nh):
        sk = slice(h * dk, (h + 1) * dk)
        sv = slice(h * dv, (h + 1) * dv)
        qh = q_ref[:, sk]
        kh = k_ref[:, sk]
        vh = v_ref[:, sv]
        cum = cum_all[:, sk]
        ops, own = _intra_operands(qh, kh, cum, levels, sub)
        a = _intra_scores(ops, own, levels, row == col)
        o_intra = jnp.dot(a.astype(BF16), vh.astype(BF16), preferred_element_type=F32)
        qe = qh * jnp.exp(cum)
        first_row = lax.broadcasted_iota(jnp.int32, (SUBLANES, dk), 0) == 0
        ones = jnp.ones((SUBLANES, LANES), F32)
        for i in range(g):
            rs = slice(i * SUBLANES, (i + 1) * SUBLANES)
            st = s0_ref[i, h]
            o_ref[rs, sv] = o_intra[rs] + jnp.dot(
                qe[rs].astype(BF16), st.astype(BF16), preferred_element_type=F32)
            last = cum[(i + 1) * SUBLANES - 1:(i + 1) * SUBLANES, :]
            kd = (kh[rs] * jnp.exp(last - cum[rs])).astype(BF16)
            decay_row = jnp.where(first_row, jnp.broadcast_to(jnp.exp(last), (SUBLANES, dk)), 0.0)
            decay = lax.dot_general(decay_row, ones, TN_DIMS, precision=lax.Precision.HIGHEST,
                                    preferred_element_type=F32)
            s_ref[i, h] = jnp.concatenate([decay] * (dv // LANES), axis=1) * st + lax.dot_general(
                kd, vh[rs].astype(BF16), TN_DIMS, preferred_element_type=F32)


def _gla_sample(q, k, v, lg, s0, nh, dk, dv, g=8):
    nseq = s0.shape[0]
    r = g * SUBLANES
    kern = functools.partial(_gla_sample_kernel, g=g, nh=nh, dk=dk, dv=dv)
    st_spec = pl.BlockSpec((g, nh, dk, dv), lambda i: (i, 0, 0, 0))
    return pl.pallas_call(
        kern,
        grid=(nseq // g,),
        in_specs=[_row_spec(r, nh * dk), _row_spec(r, nh * dk), _row_spec(r, nh * dv),
                  _row_spec(r, nh * dk), st_spec],
        out_specs=[_row_spec(r, nh * dv), st_spec],
        out_shape=[jax.ShapeDtypeStruct((nseq * SUBLANES, nh * dv), F32),
                   jax.ShapeDtypeStruct(s0.shape, F32)],
        compiler_params=_params("arbitrary"),
        name="gla_sample",
    )(q, k, v, lg, s0)


def _rms_groups64(x, lane_lo):
    outs = []
    for j in range(x.shape[-1] // LANES):
        xj = x[:, j * LANES:(j + 1) * LANES]
        sq = xj * xj
        s0 = jnp.sum(jnp.where(lane_lo, sq, 0.0), axis=-1, keepdims=True)
        s1 = jnp.sum(jnp.where(lane_lo, 0.0, sq), axis=-1, keepdims=True)
        ms = jnp.where(lane_lo, s0, s1) * (2.0 / LANES)
        outs.append(xj * lax.rsqrt(ms + EPS))
    return jnp.concatenate(outs, axis=-1)


def _mid_kernel(o_ref, z_ref, x_ref, moda_ref, modkv_ref, modb_ref,
                onorm_ref, wout_ref, gkv_ref, wk_ref, wv_ref, gk_ref, gb_ref, winb_ref, gq_ref,
                y_ref, k_ref, v_ref, kb_ref, vb_ref, qb_ref, zb_ref, *, nh, dv, qk_scale, token_minor):
    d = x_ref.shape[-1]
    o = o_ref[...].astype(F32)
    on = jnp.concatenate([_rms_rows(o[:, h * dv:(h + 1) * dv]) for h in range(nh)], axis=-1)
    on = on * onorm_ref[...]
    u = (on * _silu(z_ref[...].astype(F32))).astype(BF16)
    out = jnp.dot(u, wout_ref[...], preferred_element_type=F32)
    y = x_ref[...] + moda_ref[:, 2 * d:3 * d] * out
    y_ref[...] = y
    yn = _rms_rows(y)
    lane_lo = lax.broadcasted_iota(jnp.int32, (1, LANES), 1) < (LANES // 2)

    modkv = modkv_ref[...]
    h2 = (yn * gkv_ref[...] * (1.0 + modkv[:, d:2 * d]) + modkv[:, 0:d]).astype(BF16)
    kk = _rms_groups64(jnp.dot(h2, wk_ref[...], preferred_element_type=F32), lane_lo) * gk_ref[...]
    k_ref[...] = kk.T if token_minor else kk
    kb_ref[...] = kk.astype(BF16)
    vv = jnp.dot(h2, wv_ref[...], preferred_element_type=F32)
    v_ref[...] = vv
    vb_ref[...] = (vv.T if token_minor else vv).astype(BF16)

    modb = modb_ref[...]
    h3 = (yn * gb_ref[...] * (1.0 + modb[:, d:2 * d]) + modb[:, 0:d]).astype(BF16)
    qz = jnp.dot(h3, winb_ref[...], preferred_element_type=F32)
    qb = qz.shape[-1] - zb_ref.shape[-1]
    qq = _rms_groups64(qz[:, 0:qb], lane_lo) * (gq_ref[...] * qk_scale)
    qb_ref[...] = (qq.T if token_minor else qq).astype(BF16)
    zb_ref[...] = qz[:, qb:].astype(zb_ref.dtype)


def _mid(o, z, x2, moda3, modkv3, modb3, tiles_per_group, tm,
         onorm_t, wout, gkv, wk, wv, gk_t, gb, winb, gq_t, nh, dv, qk_scale, token_minor):
    n, d = x2.shape
    qb = wk.shape[1]
    zb = winb.shape[1] - qb
    kern = functools.partial(_mid_kernel, nh=nh, dv=dv, qk_scale=qk_scale, token_minor=token_minor)
    ms = lambda m: _mod_spec(m, tm, tiles_per_group)
    if token_minor:
        groups = n // (tm * tiles_per_group)
        t = tm * tiles_per_group
        feat = lambda w, dt: jax.ShapeDtypeStruct((groups, w, t), dt)
        feat_spec = lambda w: pl.BlockSpec(
            (None, w, tm), lambda i: (i // tiles_per_group, 0, i % tiles_per_group))
    else:
        feat = lambda w, dt: jax.ShapeDtypeStruct((n, w), dt)
        feat_spec = lambda w: _row_spec(tm, w)
    outs = [jax.ShapeDtypeStruct((n, d), F32), feat(qb, F32),
            jax.ShapeDtypeStruct((n, zb), F32), jax.ShapeDtypeStruct((n, qb), BF16),
            feat(zb, BF16), feat(qb, BF16),
            jax.ShapeDtypeStruct((n, zb), BF16)]
    out_specs = [_row_spec(tm, d), feat_spec(qb), _row_spec(tm, zb), _row_spec(tm, qb),
                 feat_spec(zb), feat_spec(qb), _row_spec(tm, zb)]
    return pl.pallas_call(
        kern,
        grid=(n // tm,),
        in_specs=[_row_spec(tm, o.shape[1]), _row_spec(tm, z.shape[1]), _row_spec(tm, d),
                  ms(moda3), ms(modkv3), ms(modb3),
                  _const_spec(onorm_t.shape), _const_spec(wout.shape), _const_spec(gkv.shape),
                  _const_spec(wk.shape), _const_spec(wv.shape), _const_spec(gk_t.shape),
                  _const_spec(gb.shape), _const_spec(winb.shape), _const_spec(gq_t.shape)],
        out_specs=out_specs,
        out_shape=outs,
        compiler_params=_params("arbitrary"),
        name="mid_proj",
    )(o, z, x2, moda3, modkv3, modb3, onorm_t, wout, gkv, wk, wv, gk_t, gb, winb, gq_t)


def _lambda_full(l_ref, lam_init):
    lv = l_ref[...]
    s1 = jnp.sum(lv[0:1] * lv[1:2], axis=-1, keepdims=True)
    s2 = jnp.sum(lv[2:3] * lv[3:4], axis=-1, keepdims=True)
    return jnp.exp(s1) - jnp.exp(s2) + lam_init


MAX_STATIC_SHIFT = 60.0


def _attn_prompt_body(hg, qi, slope_ref, bound_ref, l_ref, qt_ref, k_ref, vt_ref, o_ref, *, tq, tk, hps, lam_init):
    half = LANES // 2
    heads = range(hps)
    cols = [slice(hh * LANES, (hh + 1) * LANES) for hh in heads]
    zeros = jnp.zeros((half, tq), BF16)
    krow = lax.broadcasted_iota(jnp.int32, (tk, LANES), 0).astype(F32)
    ones_rows = jnp.ones((2 * SUBLANES, tk), BF16)
    slopes, qqs = [], []
    for hh in heads:
        slopes.append(slope_ref[hg * hps + hh])
        qt = qt_ref[cols[hh], :]
        qqs.append(jnp.concatenate([jnp.concatenate([qt[0:half], zeros], axis=0),
                                    jnp.concatenate([zeros, qt[half:LANES]], axis=0)], axis=1))

    def scores(j, causal):
        start = pl.multiple_of(j * tk, tk)
        offset = (j * tk - qi * tq).astype(F32)
        out = []
        for hh in heads:
            key_bias = slopes[hh] * (krow + offset)
            s = jnp.dot(k_ref[pl.ds(start, tk), cols[hh]], qqs[hh], preferred_element_type=F32)
            out.append(s + jnp.concatenate([key_bias] * (2 * tq // LANES), axis=1))
        if causal:
            kk = lax.broadcasted_iota(jnp.int32, (tk, 2 * tq), 0)
            qpos = lax.broadcasted_iota(jnp.int32, (tk, 2 * tq), 1)
            keep = kk <= jnp.where(qpos >= tq, qpos - tq, qpos)
            out = [jnp.where(keep, s, -jnp.inf) for s in out]
        return start, out

    def weighted_values(start, probs):
        return [jnp.dot(jnp.concatenate([vt_ref[cols[hh], pl.ds(start, tk)], ones_rows], axis=0), probs[hh],
                        preferred_element_type=F32) for hh in heads]

    def tile_fixed(j, accs, causal, shifts):
        start, ss = scores(j, causal)
        pvs = weighted_values(start, [jnp.exp2(ss[hh] - shifts[hh]).astype(BF16) for hh in heads])
        return tuple(accs[hh] + pvs[hh] for hh in heads)

    def diagonal_fixed(accs, shifts, late_shifts):
        hk, hq = tk // 2, tq // 2
        base = pl.multiple_of(qi * tk, tk)
        late = lambda a: jnp.concatenate([a[:, hq:tq], a[:, tq + hq:2 * tq]], axis=1)
        kk = lax.broadcasted_iota(jnp.int32, (hk, 2 * tq), 0)
        qpos = lax.broadcasted_iota(jnp.int32, (hk, 2 * tq), 1)
        keep_early = kk <= jnp.where(qpos >= tq, qpos - tq, qpos)
        klate = lax.broadcasted_iota(jnp.int32, (hk, tq), 0)
        qlate = lax.broadcasted_iota(jnp.int32, (hk, tq), 1)
        keep_late = klate <= jnp.where(qlate >= hq, qlate - hq, qlate)
        starts = [base, pl.multiple_of(base + hk, hk)]
        ones_half = jnp.ones((2 * SUBLANES, hk), BF16)
        early, late_scores = [], []
        for hh in heads:
            bias = slopes[hh] * krow[0:hk]
            s = jnp.dot(k_ref[pl.ds(starts[0], hk), cols[hh]], qqs[hh], preferred_element_type=F32)
            early.append(jnp.where(keep_early, s + jnp.concatenate([bias] * (2 * tq // LANES), axis=1), -jnp.inf))
            s = jnp.dot(k_ref[pl.ds(starts[1], hk), cols[hh]], late(qqs[hh]), preferred_element_type=F32)
            bias = bias + slopes[hh] * hk
            late_scores.append(jnp.where(keep_late, s + jnp.concatenate([bias] * (tq // LANES), axis=1), -jnp.inf))
        p_early = [jnp.exp2(early[hh] - shifts[hh]).astype(BF16) for hh in heads]
        p_late = [jnp.exp2(late_scores[hh] - late_shifts[hh]).astype(BF16) for hh in heads]
        values = lambda hh, t: jnp.concatenate([vt_ref[cols[hh], pl.ds(starts[t], hk)], ones_half], axis=0)
        pv_early = [jnp.dot(values(hh, 0), p_early[hh], preferred_element_type=F32) for hh in heads]
        pv_late = [jnp.dot(values(hh, 1), p_late[hh], preferred_element_type=F32) for hh in heads]
        out = []
        for hh in heads:
            acc = accs[hh] + pv_early[hh]
            out.append(jnp.concatenate(
                [acc[:, 0:hq], acc[:, hq:tq] + pv_late[hh][:, 0:hq],
                 acc[:, tq:tq + hq], acc[:, tq + hq:2 * tq] + pv_late[hh][:, hq:tq]], axis=1))
        return tuple(out)

    def tile_pair_fixed(j, accs, shifts):
        tiles = [scores(j, False), scores(j + 1, False)]
        probs = [[jnp.exp2(ss[hh] - shifts[hh]).astype(BF16) for hh in heads] for _, ss in tiles]
        pvs = [weighted_values(tiles[t][0], probs[t]) for t in range(2)]
        return tuple(accs[hh] + pvs[0][hh] + pvs[1][hh] for hh in heads)

    def tile_online(j, carries, causal):
        start, ss = scores(j, causal)
        stats, probs = [], []
        for hh in heads:
            m = carries[hh][0]
            m_new = jnp.maximum(m, jnp.max(ss[hh], axis=0, keepdims=True))
            stats.append((m_new, jnp.exp2(m - m_new)))
            probs.append(jnp.exp2(ss[hh] - m_new).astype(BF16))
        pvs = weighted_values(start, probs)
        return tuple((stats[hh][0], stats[hh][1] * carries[hh][1] + pvs[hh]) for hh in heads)

    def write(accs):
        lam = _lambda_full(l_ref, lam_init)
        for hh in heads:
            r = accs[hh][0:LANES] * (1.0 / accs[hh][LANES:LANES + 1])
            o_ref[:, cols[hh]] = (r[:, 0:tq] - lam * r[:, tq:2 * tq]).T.astype(o_ref.dtype)

    bound = bound_ref[0]
    zero_acc = tuple(jnp.zeros((LANES + 2 * SUBLANES, 2 * tq), F32) for _ in heads)

    @pl.when(bound <= MAX_STATIC_SHIFT)
    def _():
        qidx = lax.broadcasted_iota(jnp.int32, (1, 2 * tq), 1)
        qidx = jnp.where(qidx >= tq, qidx - tq, qidx).astype(F32)
        shifts = [bound + slopes[hh] * qidx for hh in heads]
        pairs = qi // 2
        accs = lax.fori_loop(0, pairs, lambda p, a: tile_pair_fixed(2 * p, a, shifts), zero_acc)
        accs = lax.fori_loop(2 * pairs, qi, lambda j, a: tile_fixed(j, a, False, shifts), accs)
        qlate = lax.broadcasted_iota(jnp.int32, (1, tq), 1)
        qlate = (jnp.where(qlate >= tq // 2, qlate - tq // 2, qlate) + tq // 2).astype(F32)
        late_shifts = [bound + slopes[hh] * qlate for hh in heads]
        write(diagonal_fixed(accs, shifts, late_shifts))

    @pl.when(jnp.logical_not(bound <= MAX_STATIC_SHIFT))
    def _():
        init = tuple((jnp.full((1, 2 * tq), -jnp.inf, F32), zero_acc[hh]) for hh in heads)
        carries = lax.fori_loop(0, qi, lambda j, c: tile_online(j, c, False), init)
        write([c[1] for c in tile_online(qi, carries, True)])


def _attn_sample_body(l_ref, slope_ref, qpos_ref, q_ref, k_refs, v_refs, kn_ref, vn_ref, o_ref,
                      *, nh, nt, page, lam_init):
    n_pages = len(k_refs)
    past = n_pages * page
    w = q_ref.shape[-1]
    rows = 2 * nh * nt
    hr = 2 * nt
    dv = w // nh

    q = q_ref[...].astype(F32)
    qt = jnp.concatenate([q] * (rows // SUBLANES), axis=0)
    lane_grp = lax.broadcasted_iota(jnp.int32, (rows, w), 1) // (LANES // 2)
    row_grp = lax.broadcasted_iota(jnp.int32, (rows, w), 0) // nt
    qbd = jnp.where(lane_grp == row_grp, qt, 0.0).astype(BF16)

    s_past = jnp.concatenate(
        [jnp.dot(qbd, k_refs[i][...].astype(BF16), preferred_element_type=F32) for i in range(n_pages)],
        axis=1)
    kpos = lax.broadcasted_iota(jnp.int32, (1, past), 1)
    s_past = s_past - slope_ref[...] * (qpos_ref[...] - kpos).astype(F32)
    idx = lax.broadcasted_iota(jnp.int32, (1, kn_ref.shape[0]), 1)
    dist = qpos_ref[...] - (past + idx)
    s_new = lax.dot_general(qbd, kn_ref[...], NT_DIMS, preferred_element_type=F32)
    s_new = jnp.where((idx < nt) & (dist >= 0), s_new - slope_ref[...] * dist.astype(F32), -jnp.inf)

    m = jnp.maximum(jnp.max(s_past, axis=-1, keepdims=True), jnp.max(s_new, axis=-1, keepdims=True))
    pr_past = jnp.exp2(s_past - m)
    pr_new = jnp.exp2(s_new - m)
    denom = jnp.sum(pr_past, axis=-1, keepdims=True) + jnp.sum(pr_new, axis=-1, keepdims=True)
    head_values = [
        jnp.concatenate([v_refs[i][pl.ds(h, page, stride=nh), :].astype(BF16) for i in range(n_pages)], axis=0)
        for h in range(nh)]
    pv = jnp.concatenate(
        [jnp.dot(pr_past[h * hr:(h + 1) * hr].astype(BF16), head_values[h], preferred_element_type=F32)
         for h in range(nh)], axis=0)
    pv_new = jnp.dot(pr_new.astype(BF16), vn_ref[...], preferred_element_type=F32)
    own_new = jnp.concatenate([pv_new[h * hr:(h + 1) * hr, h * dv:(h + 1) * dv] for h in range(nh)], axis=0)
    r = (pv + own_new) / denom
    diff = r - _lambda_full(l_ref, lam_init) * pltpu.roll(r, rows - nt, 0)
    for h in range(nh):
        o_ref[:, h * dv:(h + 1) * dv] = diff[h * hr:(h + 1) * hr]


def _attn_kernel(pt_ref, slope_ref, bound_ref, l_ref, qt_ref, k_ref, vt_ref,
                 slope_col_ref, qpos_ref, q_ref, *refs,
                 n_pages, prompt_steps, sample_steps, steps_per_batch, q_tiles, prompt_args, sample_args):
    del pt_ref
    k_refs, v_refs = refs[0:n_pages], refs[n_pages:2 * n_pages]
    kn_ref, vn_ref, op_ref, os_ref = refs[2 * n_pages:]
    i = pl.program_id(0)

    def prompt_part():
        within = i % steps_per_batch
        _attn_prompt_body(within // q_tiles, within % q_tiles, slope_ref, bound_ref, l_ref,
                          qt_ref, k_ref, vt_ref, op_ref, **prompt_args)

    def sample_part():
        _attn_sample_body(l_ref, slope_col_ref, qpos_ref, q_ref, k_refs, v_refs, kn_ref, vn_ref,
                          os_ref, **sample_args)

    if prompt_steps < sample_steps:
        pl.when(i < prompt_steps)(prompt_part)
    else:
        prompt_part()
    if sample_steps < prompt_steps:
        pl.when(i < sample_steps)(sample_part)
    else:
        sample_part()


def _attn(qt, kb, vt, slopes, bound, lvec, q8, k_t, v_rows, page_table, kn, vn, slope_col, qpos_col,
          nh, nt, lam_init, tq=512, hps=2):
    b, t, w = kb.shape
    nseq = q8.shape[0]
    n_pool, _, page = k_t.shape
    dv = v_rows.shape[-1]
    n_pages = page_table.shape[1]
    nrows = kn.shape[1]
    rows = 2 * nh * nt
    assert 2 * nt == SUBLANES and nh % hps == 0 and t % tq == 0
    gw = hps * LANES
    q_tiles = t // tq
    steps_per_batch = (nh // hps) * q_tiles
    prompt_steps = b * steps_per_batch
    steps = max(prompt_steps, nseq)

    def prompt_index(i):
        i = jnp.minimum(i, prompt_steps - 1)
        within = i % steps_per_batch
        return i // steps_per_batch, within // q_tiles, within % q_tiles

    def seq_index(i):
        return jnp.minimum(i, nseq - 1)

    def prompt_spec(shape, pick):
        return pl.BlockSpec(shape, lambda i, pt: pick(*prompt_index(i)))

    k_specs = [pl.BlockSpec((None, w, page), lambda i, pt, p=p: (pt[seq_index(i), p], 0, 0))
               for p in range(n_pages)]
    v_specs = [pl.BlockSpec((None, page * nh, dv), lambda i, pt, p=p: (pt[seq_index(i), p], 0, 0))
               for p in range(n_pages)]
    seq_spec = lambda r: pl.BlockSpec((None, r, w), lambda i, pt: (seq_index(i), 0, 0))
    const = lambda shape: pl.BlockSpec(shape, lambda i, pt: (0,) * len(shape))
    smem = pl.BlockSpec(memory_space=pltpu.SMEM)
    kern = functools.partial(
        _attn_kernel, n_pages=n_pages, prompt_steps=prompt_steps, sample_steps=nseq,
        steps_per_batch=steps_per_batch, q_tiles=q_tiles,
        prompt_args=dict(tq=tq, tk=tq, hps=hps, lam_init=lam_init),
        sample_args=dict(nh=nh, nt=nt, page=page, lam_init=lam_init))
    grid_spec = pltpu.PrefetchScalarGridSpec(
        num_scalar_prefetch=1,
        grid=(steps,),
        in_specs=[smem, smem, const(lvec.shape),
                  prompt_spec((None, gw, tq), lambda bi, hi, qi: (bi, hi, qi)),
                  prompt_spec((None, t, gw), lambda bi, hi, qi: (bi, 0, hi)),
                  prompt_spec((None, gw, t), lambda bi, hi, qi: (bi, hi, 0)),
                  const(slope_col.shape), const(qpos_col.shape), seq_spec(SUBLANES)]
        + k_specs + v_specs + [seq_spec(nrows), seq_spec(nrows)],
        out_specs=[prompt_spec((None, tq, gw), lambda bi, hi, qi: (bi, qi, hi)), seq_spec(SUBLANES)],
    )
    return pl.pallas_call(
        kern,
        grid_spec=grid_spec,
        out_shape=[jax.ShapeDtypeStruct((b, t, w), BF16), jax.ShapeDtypeStruct((nseq, SUBLANES, w), F32)],
        compiler_params=_params("arbitrary"),
        name="attn",
    )(page_table, slopes, bound, lvec, qt, kb, vt, slope_col, qpos_col, q8,
      *([k_t] * n_pages), *([v_rows] * n_pages), kn, vn)


def _final_kernel(o_ref, z_ref, y_ref, modb_ref, gsub_ref, wout_ref, out_ref, *, dvb, post_scale):
    d = y_ref.shape[-1]
    o = o_ref[...].astype(F32)
    on = jnp.concatenate([_rms_rows(o[:, h * dvb:(h + 1) * dvb]) for h in range(o.shape[-1] // dvb)], axis=-1)
    on = on * gsub_ref[...] * post_scale
    u = (on * _silu(z_ref[...].astype(F32))).astype(BF16)
    out = jnp.dot(u, wout_ref[...], preferred_element_type=F32)
    out_ref[...] = y_ref[...] + modb_ref[:, 2 * d:3 * d] * out


def _final(o, z, y, modb3, tiles_per_group, tm, gsub_t, wout, dvb, post_scale):
    n, d = y.shape
    kern = functools.partial(_final_kernel, dvb=dvb, post_scale=post_scale)
    return pl.pallas_call(
        kern,
        grid=(n // tm,),
        in_specs=[_row_spec(tm, o.shape[1]), _row_spec(tm, z.shape[1]), _row_spec(tm, d),
                  _mod_spec(modb3, tm, tiles_per_group), _const_spec(gsub_t.shape), _const_spec(wout.shape)],
        out_specs=_row_spec(tm, d),
        out_shape=jax.ShapeDtypeStruct((n, d), F32),
        compiler_params=_params("arbitrary"),
        name="final_proj",
    )(o, z, y, modb3, gsub_t, wout)


def _tile(vec, reps):
    return jnp.tile(vec.astype(F32), reps).reshape(1, -1)


def kernel(x_prompt, x_sample, c_prompt, c_sample, state_gla, cache_k, cache_v, page_table, norm_a, ada_w_a, ada_b_a, w_in_a, w_g2_a, b_g_a, onorm_a, w_out_a, norm_kv, ada_w_kv, ada_b_kv, w_k, w_v, g_k, norm_b, ada_w_b, ada_b_b, w_in_b, g_q, lambda_q1, lambda_k1, lambda_q2, lambda_k2, subln_b, w_out_b):
    b, t, d = x_prompt.shape
    nseq, nt, _ = x_sample.shape
    n_a, _, nh_a, dk_a, dv_a = state_gla.shape
    n_b = norm_b.shape[0]
    ka, va = nh_a * dk_a, nh_a * dv_a
    nh_b, dv_b = cache_v.shape[2], cache_v.shape[3]
    dqk_b = cache_k.shape[4]
    n_pool, page = cache_k.shape[0], cache_k.shape[1]
    tm = 256
    tm_wide = 512
    assert n_a == 1 and n_b == 1, "one GLA layer and one differential-attention layer"
    assert nt <= SUBLANES and t % tm_wide == 0 and (nseq * nt) % tm_wide == 0

    head = -(-b // SUBLANES) * SUBLANES
    c_all = jnp.concatenate([jnp.pad(c_prompt, ((0, head - b), (0, 0))), jnp.repeat(c_sample, nt, axis=0)], axis=0)
    xp = x_prompt.reshape(b * t, d)
    xs = x_sample.reshape(nseq * nt, d)
    tiles_p = t // tm

    def split_mod(mods):
        mod_head, mod_rest = mods
        return mod_head[:b].reshape(b, 1, -1), mod_rest.reshape((nseq * nt) // tm, tm, -1)

    wide = lambda ms: ms.reshape(-1, tm_wide, ms.shape[-1])

    pad8 = lambda a: jnp.pad(a.reshape(nseq, nt, -1), ((0, 0), (0, SUBLANES - nt), (0, 0)))

    l = 0
    moda_p, moda_s = split_mod(_ada(c_all, ada_w_a[l], ada_b_a[l], head))
    rank = w_g2_a.shape[1]
    w_in_pad = jnp.pad(w_in_a[l], ((0, 0), (0, LANES - rank))).astype(BF16)
    wg2_pad = jnp.pad(w_g2_a[l], ((0, LANES - rank), (0, 0))).astype(BF16)
    gla_args = (norm_a[l], w_in_pad, wg2_pad, b_g_a[l], ka, va, dk_a)
    op, zp, state_p = _gla_prompt(xp, moda_p, norm_a[l], w_in_pad, wg2_pad, b_g_a[l], b, t, nh_a, dk_a, dv_a)
    qs, ks_, vs_, zs, lgs = _gla_in(xs, wide(moda_s), 1, tm_wide, *gla_args)
    os8, state_s = _gla_sample(pad8(qs).reshape(-1, ka), pad8(ks_).reshape(-1, ka),
                               pad8(vs_).reshape(-1, va), pad8(lgs).reshape(-1, ka),
                               state_gla[l], nh_a, dk_a, dv_a)
    os_ = os8.reshape(nseq, SUBLANES, va)[:, :nt].reshape(nseq * nt, va)

    j = 0
    lam_init = 0.8 - 0.6 * math.exp(-0.3 * (n_a + j))
    modkv_p, modkv_s = split_mod(_ada(c_all, ada_w_kv, ada_b_kv, head))
    modb_p, modb_s = split_mod(_ada(c_all, ada_w_b[j], ada_b_b[j], head))
    mid_w = (_tile(onorm_a[l], nh_a), w_out_a[l].astype(BF16), norm_kv.reshape(1, d),
             w_k.astype(BF16), w_v.astype(BF16), _tile(g_k, 2 * nh_b), norm_b[j].reshape(1, d),
             w_in_b[j].astype(BF16), _tile(g_q[j], 2 * nh_b), nh_a, dv_a, dqk_b ** -0.5 * LOG2E)
    y1p, ktp, vp, kbp, vtp, qtp, zbp = _mid(
        op, zp, xp, moda_p, modkv_p, modb_p, tiles_p, tm, *mid_w, token_minor=True)
    y1s, ks, vs, kbs, vbs, qbs, zbs = _mid(
        os_, zs, xs, moda_s, modkv_s, modb_s, 1, tm, *mid_w, token_minor=False)

    w = nh_b * dv_b
    lvec = jnp.stack([lambda_q1[j], lambda_k1[j], lambda_q2[j], lambda_k2[j]]).astype(F32)
    slopes = 2.0 ** (-8.0 * jnp.arange(1, nh_b + 1, dtype=F32) / nh_b) * LOG2E
    gain_bound = jnp.max(jnp.abs(g_q[j])) * jnp.max(jnp.abs(g_k))
    score_bound = (1.02 * dqk_b ** 0.5 * LOG2E * gain_bound + 1.0).astype(F32).reshape(1)
    past = page_table.shape[1] * page
    groups = 2 * nh_b
    slope_col = jnp.repeat(slopes, 2 * nt).reshape(nt * groups, 1)
    qpos_col = (past + jnp.tile(jnp.arange(nt, dtype=jnp.int32), groups)).reshape(nt * groups, 1)
    pad_rows = lambda a, r: jnp.pad(a.reshape(nseq, nt, w), ((0, 0), (0, r - nt), (0, 0)))
    q_rep = jnp.tile(qbs.reshape(nseq, nt, w), (1, SUBLANES // nt, 1))
    k_t = cache_k.transpose(0, 2, 3, 4, 1).reshape(n_pool, w, page)
    v_rows = cache_v.reshape(n_pool, page * nh_b, dv_b)
    ap, as8 = _attn(qtp, kbp.reshape(b, t, w), vtp, slopes, score_bound, lvec,
                    q_rep, k_t, v_rows, page_table, pad_rows(kbs, 2 * SUBLANES), pad_rows(vbs, 2 * SUBLANES),
                    slope_col, qpos_col, nh_b, nt, lam_init)
    as_ = as8[:, :nt].reshape(nseq * nt, w)

    fin_w = (_tile(subln_b[j], nh_b), w_out_b[j].astype(BF16), dv_b, 1.0 - lam_init)
    yp = _final(ap.reshape(b * t, w), zbp, y1p, modb_p, t // tm_wide, tm_wide, *fin_w)
    ys = _final(as_, zbs, y1s, wide(modb_s), 1, tm_wide, *fin_w)

    kp = ktp.reshape(b, nh_b, 2, dqk_b, t).transpose(0, 4, 1, 2, 3)
    return (yp.reshape(b, t, d), ys.reshape(nseq, nt, d), state_p[None], state_s[None],
            kp, vp.reshape(b, t, nh_b, dv_b),
            ks.reshape(nseq, nt, nh_b, 2, dqk_b), vs.reshape(nseq, nt, nh_b, dv_b))
```
